```python
import jax, jax.numpy as jnp
from jax import lax
import numpy as np

D_MODEL = 1024
BATCH = 8
SEQ = 2048
DEPTH = 4
DEC_BATCH = 32
DEC_SEQ = 8
PAST_LEN = 8192
PAGE_SIZE = 128

N_A = DEPTH // 2
N_B = DEPTH - N_A
GLA_HEADS = 4
GLA_DK = D_MODEL // 2
GLA_DV = D_MODEL
GLA_DKH = GLA_DK // GLA_HEADS
GLA_DVH = GLA_DV // GLA_HEADS
GLA_GATE_RANK = 16
GLA_GATE_TAU = 16.0
GLA_CHUNK = 64
GLA_IN = 2 * GLA_DK + 2 * GLA_DV + GLA_GATE_RANK
FOX_HEADS = 8
FOX_HD = D_MODEL // FOX_HEADS
FOX_DIM = FOX_HEADS * FOX_HD
FOX_BLOCK_Q = 128
FOX_BF_MIN = 2.0
FOX_BF_MAX = 12.0
KV_OUT = 2 * FOX_DIM + FOX_HEADS
N_EXPERTS = 64
TOP_K = 8
N_GROUPS = 8
TOPK_GROUPS = 4
D_EXPERT = 256
D_SHARED = 256
ROUTED_SCALE = 2.5
EXPERT_BLOCK = 128
DEEPNORM_ALPHA = (2 * DEPTH) ** 0.25
DEEPNORM_BETA = (8 * DEPTH) ** -0.25
LN_EPS = 1e-5
RMS_EPS = 1e-6

kernel_name = 'yoco_gla_fox_moe_decoder_step'


def _layer_norm(x, g, b):
    xf = x.astype(jnp.float32)
    xc = xf - jnp.mean(xf, -1, keepdims=True)
    var = jnp.mean(xc * xc, -1, keepdims=True)
    return (xc * lax.rsqrt(var + LN_EPS) * g.astype(jnp.float32) + b.astype(jnp.float32)).astype(x.dtype)


def _gla_chunked(q, k, v, log_a, s0):
    B, T, H, dk = q.shape
    dv = v.shape[-1]
    c = min(GLA_CHUNK, T)
    n = -(-T // c)
    pad = n * c - T

    def blocks(a):
        a = jnp.pad(a, ((0, 0), (0, pad), (0, 0), (0, 0)))
        return a.reshape(B, n, c, H, a.shape[-1]).transpose(1, 0, 3, 2, 4)

    qb, kb, vb, ab = blocks(q), blocks(k), blocks(v), blocks(log_a)
    bcum = jnp.cumsum(ab, axis=3)
    causal = jnp.tril(jnp.ones((c, c), dtype=bool))

    def step(S, inp):
        qc, kc, vc, bc = inp
        b_last = bc[:, :, -1:, :]
        q_dec = qc * jnp.exp(bc)
        k_dec = kc * jnp.exp(-bc)
        inter = jnp.einsum('bhcd,bhde->bhce', q_dec, S)
        att = jnp.where(causal, jnp.einsum('bhcd,bhsd->bhcs', q_dec, k_dec), 0.0)
        intra = jnp.einsum('bhcs,bhse->bhce', att, vc)
        S_new = S * jnp.exp(b_last[:, :, 0, :, None]) + jnp.einsum('bhsd,bhse->bhde', kc * jnp.exp(b_last - bc), vc)
        return S_new, inter + intra

    S_T, o = lax.scan(step, s0, (qb, kb, vb, bcum))
    o = o.transpose(1, 0, 3, 2, 4).reshape(B, n * c, H, dv)[:, :T]
    return o, S_T


def _gla_mixer(u, s0, w_in, w_a2, b_a, norm_g, w_o):
    B, T, _ = u.shape
    p = u @ w_in
    q, k, v, g, a_lr = jnp.split(p, [GLA_DK, 2 * GLA_DK, 2 * GLA_DK + GLA_DV, 2 * GLA_DK + 2 * GLA_DV], axis=-1)
    log_a = jax.nn.log_sigmoid((a_lr @ w_a2 + b_a).astype(jnp.float32)) / GLA_GATE_TAU
    qh = q.astype(jnp.float32).reshape(B, T, GLA_HEADS, GLA_DKH) * (GLA_DKH ** -0.5)
    kh = k.astype(jnp.float32).reshape(B, T, GLA_HEADS, GLA_DKH)
    vh = v.astype(jnp.float32).reshape(B, T, GLA_HEADS, GLA_DVH)
    o, s_T = _gla_chunked(qh, kh, vh, log_a.reshape(B, T, GLA_HEADS, GLA_DKH), s0.astype(jnp.float32))
    o = o * lax.rsqrt(jnp.mean(o * o, -1, keepdims=True) + RMS_EPS) * norm_g.astype(jnp.float32)
    o = o.reshape(B, T, GLA_DV) * jax.nn.silu(g.astype(jnp.float32))
    return o.astype(u.dtype) @ w_o, s_T


def _shared_kv(x, silu_c, kv_w_mod, kv_b_mod, kv_w, fox_b_f):
    B, T, _ = x.shape
    m = (silu_c @ kv_w_mod + kv_b_mod)[:, None, :]
    sh, sc = jnp.split(m, 2, axis=-1)
    p = (x * (1 + sc) + sh) @ kv_w
    k = p[..., :FOX_DIM].reshape(B, T, FOX_HEADS, FOX_HD)
    v = p[..., FOX_DIM:2 * FOX_DIM].reshape(B, T, FOX_HEADS, FOX_HD)
    logf = jax.nn.log_sigmoid(p[..., 2 * FOX_DIM:].astype(jnp.float32) + fox_b_f.astype(jnp.float32))
    return k, v, logf


def _fox_context(k_new, v_new, logf_new, past):
    f_new = jnp.cumsum(logf_new, axis=1)
    if past is None:
        return k_new, v_new, f_new, f_new, 0
    k_past, v_past, logf_past = past
    lf = logf_past.astype(jnp.float32)
    suffix = lax.cumsum(lf, axis=1, reverse=True) - lf
    k_all = jnp.concatenate([k_past.astype(k_new.dtype), k_new], axis=1)
    v_all = jnp.concatenate([v_past.astype(v_new.dtype), v_new], axis=1)
    return k_all, v_all, f_new, jnp.concatenate([-suffix, f_new], axis=1), k_past.shape[1]


def _forgetting_attention(q, k, v, fq, fk, q_off):
    Tq, hd = q.shape[1], q.shape[3]
    scale = hd ** -0.5
    outs = []
    for a in range(0, Tq, FOX_BLOCK_Q):
        b = min(a + FOX_BLOCK_Q, Tq)
        nk = q_off + b
        s = jnp.einsum('bqhd,bkhd->bhqk', q[:, a:b], k[:, :nk], preferred_element_type=jnp.float32) * scale
        s = s + jnp.swapaxes(fq[:, a:b], 1, 2)[..., None] - jnp.swapaxes(fk[:, :nk], 1, 2)[:, :, None, :]
        mask = jnp.arange(nk)[None, :] <= (q_off + jnp.arange(a, b))[:, None]
        p = jax.nn.softmax(jnp.where(mask, s, -jnp.inf), axis=-1)
        outs.append(jnp.einsum('bhqk,bkhd->bqhd', p.astype(v.dtype), v[:, :nk], preferred_element_type=jnp.float32))
    return jnp.concatenate(outs, axis=1)


def _fox_mixer(u, ctx, w_q, w_o):
    k, v, fq, fk, q_off = ctx
    B, T, _ = u.shape
    q = (u @ w_q).reshape(B, T, FOX_HEADS, FOX_HD)
    o = _forgetting_attention(q, k, v, fq, fk, q_off)
    return o.reshape(B, T, FOX_DIM).astype(u.dtype) @ w_o


def _swiglu(x, wg, wu, wd):
    return (jax.nn.silu(x @ wg) * (x @ wu)) @ wd


def _route(xf, w_router, router_bias):
    N = xf.shape[0]
    s = jax.nn.sigmoid((xf @ w_router).astype(jnp.float32))
    sb = s + router_bias.astype(jnp.float32)
    g_score = lax.top_k(sb.reshape(N, N_GROUPS, N_EXPERTS // N_GROUPS), 2)[0].sum(-1)
    _, g_idx = lax.top_k(g_score, TOPK_GROUPS)
    g_mask = (g_idx[..., None] == jnp.arange(N_GROUPS)).any(axis=1)
    e_mask = jnp.repeat(g_mask, N_EXPERTS // N_GROUPS, axis=1)
    _, idx = lax.top_k(jnp.where(e_mask, sb, -jnp.inf), TOP_K)
    w = jnp.take_along_axis(s, idx, axis=-1)
    w = w / jnp.sum(w, -1, keepdims=True) * ROUTED_SCALE
    return idx, w


def _routed_experts(xf, idx, w, w_gate, w_up, w_down):
    N, D = xf.shape
    M = N * TOP_K
    flat_e = idx.reshape(-1)
    order = jnp.argsort(flat_e)
    se = flat_e[order]
    counts = jnp.bincount(flat_e, length=N_EXPERTS)
    padded = (counts + EXPERT_BLOCK - 1) // EXPERT_BLOCK * EXPERT_BLOCK
    pad_end = jnp.cumsum(padded)
    pad_start = pad_end - padded
    grp_start = jnp.cumsum(counts) - counts
    dest = pad_start[se] + jnp.arange(M) - grp_start[se]
    n_blocks = -(-M // EXPERT_BLOCK) + N_EXPERTS
    R = n_blocks * EXPERT_BLOCK
    row_tok = jnp.zeros((R,), jnp.int32).at[dest].set((order // TOP_K).astype(jnp.int32))
    row_w = jnp.zeros((R,), jnp.float32).at[dest].set(w.reshape(-1)[order])
    block_e = jnp.minimum(jnp.searchsorted(pad_end, jnp.arange(n_blocks) * EXPERT_BLOCK, side='right'), N_EXPERTS - 1)

    def block_fn(args):
        toks, e = args
        xb = xf[toks]
        return _swiglu(xb, w_gate[e], w_up[e], w_down[e])

    yb = lax.map(block_fn, (row_tok.reshape(n_blocks, EXPERT_BLOCK), block_e))
    return jax.ops.segment_sum(yb.reshape(R, D) * row_w[:, None], row_tok, num_segments=N)


def _moe(u, router_w, router_bias, w_gate, w_up, w_down, sh_gate, sh_up, sh_down):
    B, T, D = u.shape
    xf = u.reshape(B * T, D)
    idx, w = _route(xf, router_w, router_bias)
    routed = _routed_experts(xf, idx, w, w_gate, w_up, w_down)
    return (routed.astype(u.dtype) + _swiglu(xf, sh_gate, sh_up, sh_down)).reshape(B, T, D)


def _trunk(x, c, gla_s0, past, W):
    silu_c = jax.nn.silu(c.astype(jnp.float32)).astype(x.dtype)
    new_gla = []
    ctx = None
    for l in range(DEPTH):
        mod = (silu_c @ W['ada_w'][l] + W['ada_b'][l])[:, None, :]
        sh_a, sc_a, g_a, sh_m, sc_m, g_m = jnp.split(mod, 6, axis=-1)
        u = x * (1 + sc_a) + sh_a
        if l < N_A:
            out, s_T = _gla_mixer(u, gla_s0[l], W['gla_w_in'][l], W['gla_w_a2'][l], W['gla_b_a'][l],
                                  W['gla_norm_g'][l], W['gla_w_o'][l])
            new_gla.append(s_T)
        else:
            out = _fox_mixer(u, ctx, W['fox_w_q'][l - N_A], W['fox_w_o'][l - N_A])
        x = _layer_norm(DEEPNORM_ALPHA * x + (1 + g_a) * out, W['ln_g'][l, 0], W['ln_b'][l, 0])
        u = x * (1 + sc_m) + sh_m
        y = _moe(u, W['router_w'][l], W['router_bias'][l], W['exp_w_gate'][l], W['exp_w_up'][l],
                 W['exp_w_down'][l], W['sh_w_gate'][l], W['sh_w_up'][l], W['sh_w_down'][l])
        x = _layer_norm(DEEPNORM_ALPHA * x + (1 + g_m) * y, W['ln_g'][l, 1], W['ln_b'][l, 1])
        if l == N_A - 1:
            k_new, v_new, logf_new = _shared_kv(x, silu_c, W['kv_w_mod'], W['kv_b_mod'], W['kv_w'], W['fox_b_f'])
            ctx = _fox_context(k_new, v_new, logf_new, past)
    return x, jnp.stack(new_gla), k_new, v_new, logf_new


def setup_inputs(seed: int = 0) -> dict:
    key = jax.random.key(seed)
    ks = jax.random.split(key, 32)
    f32 = jnp.float32

    def nrm(i, shape, scale=1.0):
        return jax.random.normal(ks[i], shape, f32) * scale

    n_pages = PAST_LEN // PAGE_SIZE
    n_used = DEC_BATCH * n_pages
    n_phys = n_used + max(n_used // 4, 1)
    d = D_MODEL
    head_bias = jnp.linspace(FOX_BF_MIN, FOX_BF_MAX, FOX_HEADS, dtype=f32)
    return {
        'x_prompt': nrm(0, (BATCH, SEQ, d)),
        'x_sample': nrm(1, (DEC_BATCH, DEC_SEQ, d)),
        'c_prompt': nrm(2, (BATCH, d)),
        'c_sample': nrm(3, (DEC_BATCH, d)),
        'state_gla': nrm(4, (N_A, DEC_BATCH, GLA_HEADS, GLA_DKH, GLA_DVH), 0.5),
        'cache_k': nrm(5, (n_phys, PAGE_SIZE, FOX_HEADS, FOX_HD)),
        'cache_v': nrm(6, (n_phys, PAGE_SIZE, FOX_HEADS, FOX_HD)),
        'cache_logf': jax.nn.log_sigmoid(head_bias + nrm(7, (n_phys, PAGE_SIZE, FOX_HEADS), 0.5)),
        'page_table': jax.random.permutation(ks[8], n_phys)[:n_used].reshape(DEC_BATCH, n_pages).astype(jnp.int32),
        'gla_w_in': nrm(9, (N_A, d, GLA_IN), d ** -0.5),
        'gla_w_a2': nrm(10, (N_A, GLA_GATE_RANK, GLA_DK), GLA_GATE_RANK ** -0.5),
        'gla_b_a': nrm(11, (N_A, GLA_DK), 0.1),
        'gla_norm_g': 1.0 + nrm(12, (N_A, GLA_DVH), 0.02),
        'gla_w_o': nrm(13, (N_A, GLA_DV, d), GLA_DV ** -0.5 * DEEPNORM_BETA),
        'kv_w_mod': nrm(14, (d, 2 * d), 0.1 * d ** -0.5),
        'kv_b_mod': nrm(15, (2 * d,), 0.01),
        'kv_w': nrm(16, (d, KV_OUT), d ** -0.5),
        'fox_b_f': head_bias + nrm(17, (FOX_HEADS,), 0.1),
        'fox_w_q': nrm(18, (N_B, d, FOX_DIM), d ** -0.5),
        'fox_w_o': nrm(19, (N_B, FOX_DIM, d), FOX_DIM ** -0.5 * DEEPNORM_BETA),
        'ada_w': nrm(20, (DEPTH, d, 6 * d), 0.1 * d ** -0.5),
        'ada_b': nrm(21, (DEPTH, 6 * d), 0.01),
        'ln_g': 1.0 + nrm(22, (DEPTH, 2, d), 0.02),
        'ln_b': nrm(23, (DEPTH, 2, d), 0.01),
        'router_w': nrm(24, (DEPTH, d, N_EXPERTS), d ** -0.5),
        'router_bias': nrm(25, (DEPTH, N_EXPERTS), 0.01),
        'exp_w_gate': nrm(26, (DEPTH, N_EXPERTS, d, D_EXPERT), d ** -0.5),
        'exp_w_up': nrm(27, (DEPTH, N_EXPERTS, d, D_EXPERT), d ** -0.5),
        'exp_w_down': nrm(28, (DEPTH, N_EXPERTS, D_EXPERT, d), D_EXPERT ** -0.5 * DEEPNORM_BETA),
        'sh_w_gate': nrm(29, (DEPTH, d, D_SHARED), d ** -0.5),
        'sh_w_up': nrm(30, (DEPTH, d, D_SHARED), d ** -0.5),
        'sh_w_down': nrm(31, (DEPTH, D_SHARED, d), D_SHARED ** -0.5 * DEEPNORM_BETA),
    }


def reference(x_prompt, x_sample, c_prompt, c_sample, state_gla, cache_k, cache_v, cache_logf, page_table,
              gla_w_in, gla_w_a2, gla_b_a, gla_norm_g, gla_w_o, kv_w_mod, kv_b_mod, kv_w, fox_b_f,
              fox_w_q, fox_w_o, ada_w, ada_b, ln_g, ln_b, router_w, router_bias,
              exp_w_gate, exp_w_up, exp_w_down, sh_w_gate, sh_w_up, sh_w_down):
    W = dict(gla_w_in=gla_w_in, gla_w_a2=gla_w_a2, gla_b_a=gla_b_a, gla_norm_g=gla_norm_g, gla_w_o=gla_w_o,
             kv_w_mod=kv_w_mod, kv_b_mod=kv_b_mod, kv_w=kv_w, fox_b_f=fox_b_f, fox_w_q=fox_w_q, fox_w_o=fox_w_o,
             ada_w=ada_w, ada_b=ada_b, ln_g=ln_g, ln_b=ln_b, router_w=router_w, router_bias=router_bias,
             exp_w_gate=exp_w_gate, exp_w_up=exp_w_up, exp_w_down=exp_w_down,
             sh_w_gate=sh_w_gate, sh_w_up=sh_w_up, sh_w_down=sh_w_down)
    s0_prompt = jnp.zeros((N_A, x_prompt.shape[0], GLA_HEADS, GLA_DKH, GLA_DVH), jnp.float32)
    y_prompt, gla_prompt, k_prompt, v_prompt, logf_prompt = _trunk(x_prompt, c_prompt, s0_prompt, None, W)
    db, n_pages = page_table.shape
    past_len = n_pages * PAGE_SIZE
    past = (cache_k[page_table].reshape(db, past_len, FOX_HEADS, FOX_HD),
            cache_v[page_table].reshape(db, past_len, FOX_HEADS, FOX_HD),
            cache_logf[page_table].reshape(db, past_len, FOX_HEADS))
    y_sample, gla_sample, k_sample, v_sample, logf_sample = _trunk(x_sample, c_sample, state_gla, past, W)
    return (y_prompt, y_sample, gla_prompt, gla_sample, k_prompt, v_prompt, logf_prompt, k_sample, v_sample, logf_sample)
```

```python
import functools

import jax
import jax.numpy as jnp
from jax import lax
from jax.experimental import pallas as pl
from jax.experimental.pallas import tpu as pltpu

F32 = jnp.float32
BF16 = jnp.bfloat16

D_MODEL = 1024
DEPTH = 4
N_A = DEPTH // 2
PAGE_SIZE = 128
GLA_HEADS = 4
GLA_DK = D_MODEL // 2
GLA_DV = D_MODEL
GLA_DKH = GLA_DK // GLA_HEADS
GLA_DVH = GLA_DV // GLA_HEADS
GLA_GATE_RANK = 16
GLA_GATE_TAU = 16.0
GLA_CHUNK = 64
FOX_HEADS = 8
FOX_HD = D_MODEL // FOX_HEADS
FOX_DIM = FOX_HEADS * FOX_HD
N_EXPERTS = 64
TOP_K = 8
N_GROUPS = 8
GROUP_SIZE = N_EXPERTS // N_GROUPS
TOPK_GROUPS = 4
D_EXPERT = 256
D_SHARED = 256
ROUTED_SCALE = 2.5
DEEPNORM_ALPHA = (2 * DEPTH) ** 0.25
LN_EPS = 1e-5
RMS_EPS = 1e-6

LANES = 128
GLA_IN_MAIN = 2 * GLA_DK + 2 * GLA_DV
GLA_IN_PAD = GLA_IN_MAIN + LANES
KV_PAD = 2 * FOX_DIM + LANES
MASK_NEG = -1e30
VMEM_LIMIT = 52 * 1024 * 1024


def _cparams(*sem):
    return pltpu.CompilerParams(dimension_semantics=sem, vmem_limit_bytes=VMEM_LIMIT)


def _dot(a, b):
    return jnp.dot(a, b, preferred_element_type=F32)


def _dot_nt(a, b):
    return lax.dot_general(a, b, (((1,), (1,)), ((), ())), preferred_element_type=F32)


def _dot_tn(a, b):
    return lax.dot_general(a, b, (((0,), (0,)), ((), ())), preferred_element_type=F32)


def _sigmoid(x):
    return 1.0 / (1.0 + jnp.exp(-x))


def _silu(x):
    return x * _sigmoid(x)


def _log_sigmoid(x):
    return jnp.minimum(x, 0.0) - jnp.log1p(jnp.exp(-jnp.abs(x)))


def _split3(x):
    hi = x.astype(BF16)
    r = x - hi.astype(F32)
    mid = r.astype(BF16)
    lo = (r - mid.astype(F32)).astype(BF16)
    return hi, mid, lo


def _dot_exact_lhs(mat01, x):
    hi, mid, lo = _split3(x)
    return _dot(mat01, hi) + _dot(mat01, mid) + _dot(mat01, lo)


def _dot_exact_rhs(x, mat01):
    hi, mid, lo = _split3(x)
    return _dot(hi, mat01) + _dot(mid, mat01) + _dot(lo, mat01)


def _layer_norm(h, g, b):
    mu = jnp.mean(h, axis=-1, keepdims=True)
    xc = h - mu
    var = jnp.mean(xc * xc, axis=-1, keepdims=True)
    return xc * lax.rsqrt(var + LN_EPS) * g + b


def _row_mod(m, T, tm):
    B, D = m.shape
    if T % tm == 0:
        tiles_per_batch = T // tm
        return m.reshape(B, 1, D), pl.BlockSpec((1, 1, D), lambda i: (i // tiles_per_batch, 0, 0))
    assert tm % T == 0
    e = jnp.repeat(m, T, axis=0).reshape(B * T // tm, tm, D)
    return e, pl.BlockSpec((1, tm, D), lambda i: (i, 0, 0))


def _cond_kernel(c_ref, w_ref, b_ref, o_ref):
    s = _silu(c_ref[...]).astype(BF16)
    o_ref[0] = _dot(s, w_ref[0].astype(BF16)) + b_ref[0]


def _cond_linear(c, w, b, tn):
    M, D = c.shape
    L, _, N = w.shape
    return pl.pallas_call(
        _cond_kernel,
        grid=(L, N // tn),
        in_specs=[
            pl.BlockSpec((M, D), lambda l, j: (0, 0)),
            pl.BlockSpec((1, D, tn), lambda l, j: (l, 0, j)),
            pl.BlockSpec((1, 1, tn), lambda l, j: (l, 0, j)),
        ],
        out_specs=pl.BlockSpec((1, M, tn), lambda l, j: (l, 0, j)),
        out_shape=jax.ShapeDtypeStruct((L, M, N), F32),
        compiler_params=_cparams("parallel", "parallel"),
        name="cond_linear",
    )(c, w, b.reshape(L, 1, N))


def _modproj_kernel(x_ref, sc_ref, sh_ref, w_ref, o_ref):
    u = (x_ref[...] * (1.0 + sc_ref[0]) + sh_ref[0]).astype(BF16)
    o_ref[...] = _dot(u, w_ref[...]).astype(o_ref.dtype)


def _modproj(x, sc, sh, w, T, tm, out_dtype):
    N, D = x.shape
    Dout = w.shape[1]
    sc3, sc_spec = _row_mod(sc, T, tm)
    sh3, sh_spec = _row_mod(sh, T, tm)
    return pl.pallas_call(
        _modproj_kernel,
        grid=(N // tm,),
        in_specs=[
            pl.BlockSpec((tm, D), lambda i: (i, 0)),
            sc_spec,
            sh_spec,
            pl.BlockSpec((D, Dout), lambda i: (0, 0)),
        ],
        out_specs=pl.BlockSpec((tm, Dout), lambda i: (i, 0)),
        out_shape=jax.ShapeDtypeStruct((N, Dout), out_dtype),
        compiler_params=_cparams("parallel"),
        name="modproj",
    )(x, sc3, sh3, w)


def _kv_kernel(x_ref, sc_ref, sh_ref, w_ref, bf_ref, k_ref, v_ref, kb_ref, vb_ref, lf_ref, cum_ref,
               carry_ref, *, T, tm):
    u = (x_ref[...] * (1.0 + sc_ref[0]) + sh_ref[0]).astype(BF16)
    p = _dot(u, w_ref[...])
    k = p[:, :FOX_DIM]
    v = p[:, FOX_DIM:2 * FOX_DIM]
    k_ref[...] = k
    v_ref[...] = v
    kb_ref[...] = k.astype(BF16)
    vb_ref[...] = v.astype(BF16)
    lf = _log_sigmoid(p[:, 2 * FOX_DIM:] + bf_ref[...])
    lf_ref[...] = lf
    if T % tm == 0:
        @pl.when(pl.program_id(0) % (T // tm) == 0)
        def _():
            carry_ref[...] = jnp.zeros_like(carry_ref)

        sb = LANES
        r = lax.broadcasted_iota(jnp.int32, (sb, sb), 0)
        c = lax.broadcasted_iota(jnp.int32, (sb, sb), 1)
        tri = (c <= r).astype(BF16)
        carry = carry_ref[...]
        for s in range(tm // sb):
            cs = _dot_exact_lhs(tri, lf[s * sb:(s + 1) * sb]) + carry
            cum_ref[s * sb:(s + 1) * sb, :] = cs
            carry = cs[sb - 1:sb, :]
        carry_ref[...] = carry
    else:
        r = lax.broadcasted_iota(jnp.int32, (tm, tm), 0)
        c = lax.broadcasted_iota(jnp.int32, (tm, tm), 1)
        tri = ((c <= r) & ((r // T) == (c // T))).astype(BF16)
        cum_ref[...] = _dot_exact_lhs(tri, lf)


def _kv_proj(x, sc, sh, w, bf, T, tm):
    N, D = x.shape
    sc3, sc_spec = _row_mod(sc, T, tm)
    sh3, sh_spec = _row_mod(sh, T, tm)
    row = lambda i: (i, 0)
    return pl.pallas_call(
        functools.partial(_kv_kernel, T=T, tm=tm),
        grid=(N // tm,),
        in_specs=[
            pl.BlockSpec((tm, D), row),
            sc_spec,
            sh_spec,
            pl.BlockSpec((D, KV_PAD), lambda i: (0, 0)),
            pl.BlockSpec((1, LANES), lambda i: (0, 0)),
        ],
        out_specs=[
            pl.BlockSpec((tm, FOX_DIM), row),
            pl.BlockSpec((tm, FOX_DIM), row),
            pl.BlockSpec((tm, FOX_DIM), row),
            pl.BlockSpec((tm, FOX_DIM), row),
            pl.BlockSpec((tm, LANES), row),
            pl.BlockSpec((tm, LANES), row),
        ],
        out_shape=[
            jax.ShapeDtypeStruct((N, FOX_DIM), F32),
            jax.ShapeDtypeStruct((N, FOX_DIM), F32),
            jax.ShapeDtypeStruct((N, FOX_DIM), BF16),
            jax.ShapeDtypeStruct((N, FOX_DIM), BF16),
            jax.ShapeDtypeStruct((N, LANES), F32),
            jax.ShapeDtypeStruct((N, LANES), F32),
        ],
        scratch_shapes=[pltpu.VMEM((1, LANES), F32)],
        compiler_params=_cparams("arbitrary"),
        name="kv_proj",
    )(x, sc3, sh3, w, bf)


def _gla_kernel(q_ref, k_ref, v_ref, g_ref, a_ref, wa_ref, ba_ref, ng_ref, s0_ref, o_ref, st_ref,
                state_ref, *, c, nsub, t_valid):
    t = pl.program_id(2)

    @pl.when(t == 0)
    def _():
        state_ref[...] = s0_ref[0, 0].T

    r = lax.broadcasted_iota(jnp.int32, (c, c), 0)
    cc = lax.broadcasted_iota(jnp.int32, (c, c), 1)
    causal = cc <= r
    tri = causal.astype(BF16)
    scale = GLA_DKH ** -0.5
    for i in range(nsub):
        sl = slice(i * c, (i + 1) * c)
        z = _dot(a_ref[sl, :].astype(BF16), wa_ref[...]) + ba_ref[...]
        la = _log_sigmoid(z) / GLA_GATE_TAU
        if t_valid < c:
            la = jnp.where(lax.broadcasted_iota(jnp.int32, la.shape, 0) < t_valid, la, 0.0)
        bc = _dot_exact_lhs(tri, la)
        b_last = bc[c - 1:c, :]
        k = k_ref[sl, :]
        v = v_ref[sl, :].astype(BF16)
        q_dec = ((q_ref[sl, :] * scale) * jnp.exp(bc)).astype(BF16)
        k_dec = (k * jnp.exp(-bc)).astype(BF16)
        k_end = (k * jnp.exp(b_last - bc)).astype(BF16)
        st = state_ref[...]
        inter = _dot_nt(q_dec, st.astype(BF16))
        att = jnp.where(causal, _dot_nt(q_dec, k_dec), 0.0).astype(BF16)
        o = inter + _dot(att, v)
        state_ref[...] = st * jnp.exp(b_last) + _dot_tn(v, k_end)
        o = o * lax.rsqrt(jnp.mean(o * o, axis=-1, keepdims=True) + RMS_EPS) * ng_ref[...]
        o_ref[sl, :] = (o * _silu(g_ref[sl, :])).astype(o_ref.dtype)

    @pl.when(t == pl.num_programs(2) - 1)
    def _():
        st_ref[0, 0] = state_ref[...].T


def _gla(p, wa, ba, ng, s0, B, Tp, tc, t_valid):
    nt = Tp // tc
    c = GLA_CHUNK
    row = lambda b, h, t: b * nt + t
    kq = GLA_DK // GLA_DKH
    kv = 2 * GLA_DK // GLA_DVH
    kg = kv + GLA_HEADS
    ka = GLA_IN_MAIN // LANES
    return pl.pallas_call(
        functools.partial(_gla_kernel, c=c, nsub=tc // c, t_valid=t_valid),
        grid=(B, GLA_HEADS, nt),
        in_specs=[
            pl.BlockSpec((tc, GLA_DKH), lambda b, h, t: (row(b, h, t), h)),
            pl.BlockSpec((tc, GLA_DKH), lambda b, h, t: (row(b, h, t), kq + h)),
            pl.BlockSpec((tc, GLA_DVH), lambda b, h, t: (row(b, h, t), kv + h)),
            pl.BlockSpec((tc, GLA_DVH), lambda b, h, t: (row(b, h, t), kg + h)),
            pl.BlockSpec((tc, LANES), lambda b, h, t: (row(b, h, t), ka)),
            pl.BlockSpec((LANES, GLA_DKH), lambda b, h, t: (0, h)),
            pl.BlockSpec((1, GLA_DKH), lambda b, h, t: (0, h)),
            pl.BlockSpec((1, GLA_DVH), lambda b, h, t: (0, 0)),
            pl.BlockSpec((1, 1, GLA_DKH, GLA_DVH), lambda b, h, t: (b, h, 0, 0)),
        ],
        out_specs=[
            pl.BlockSpec((tc, GLA_DVH), lambda b, h, t: (row(b, h, t), h)),
            pl.BlockSpec((1, 1, GLA_DKH, GLA_DVH), lambda b, h, t: (b, h, 0, 0)),
        ],
        out_shape=[
            jax.ShapeDtypeStruct((B * Tp, GLA_DV), BF16),
            jax.ShapeDtypeStruct((B, GLA_HEADS, GLA_DKH, GLA_DVH), F32),
        ],
        scratch_shapes=[pltpu.VMEM((GLA_DVH, GLA_DKH), F32)],
        compiler_params=_cparams("parallel", "parallel", "arbitrary"),
        name="gla",
    )(p, p, p, p, p, wa, ba, ng, s0)


def _proj_res_ln_kernel(a_ref, w_ref, x_ref, gate_ref, lg_ref, lb_ref, sc_ref, sh_ref, xo_ref, uo_ref):
    y = _dot(a_ref[...].astype(BF16), w_ref[...])
    h = DEEPNORM_ALPHA * x_ref[...] + (1.0 + gate_ref[0]) * y
    xn = _layer_norm(h, lg_ref[...], lb_ref[...])
    xo_ref[...] = xn
    uo_ref[...] = (xn * (1.0 + sc_ref[0]) + sh_ref[0]).astype(uo_ref.dtype)


def _proj_res_ln(a, w, x, gate, lg, lb, sc, sh, T, tm):
    N, D = x.shape
    g3, g_spec = _row_mod(gate, T, tm)
    sc3, sc_spec = _row_mod(sc, T, tm)
    sh3, sh_spec = _row_mod(sh, T, tm)
    row = lambda i: (i, 0)
    fixed = lambda i: (0, 0)
    return pl.pallas_call(
        _proj_res_ln_kernel,
        grid=(N // tm,),
        in_specs=[
            pl.BlockSpec((tm, a.shape[1]), row),
            pl.BlockSpec(w.shape, fixed),
            pl.BlockSpec((tm, D), row),
            g_spec,
            pl.BlockSpec((1, D), fixed),
            pl.BlockSpec((1, D), fixed),
            sc_spec,
            sh_spec,
        ],
        out_specs=[pl.BlockSpec((tm, D), row), pl.BlockSpec((tm, D), row)],
        out_shape=[jax.ShapeDtypeStruct((N, D), F32), jax.ShapeDtypeStruct((N, D), BF16)],
        compiler_params=_cparams("parallel"),
        name="proj_res_ln",
    )(a, w, x, g3, lg.reshape(1, D), lb.reshape(1, D), sc3, sh3)


def _router_kernel(u_ref, wr_ref, rb_ref, idx_ref, w_ref, rank_ref, cnt_ref, carry_ref, *, tm):
    @pl.when(pl.program_id(0) == 0)
    def _():
        carry_ref[...] = jnp.zeros_like(carry_ref)

    s = _sigmoid(_dot_nt(wr_ref[...], u_ref[...]))
    sb = s + rb_ref[...]
    neg = -jnp.inf
    sub = lax.broadcasted_iota(jnp.int32, (GROUP_SIZE, tm), 0)
    gs = GROUP_SIZE
    s_g = [s[g * gs:(g + 1) * gs] for g in range(N_GROUPS)]
    sb_g = [sb[g * gs:(g + 1) * gs] for g in range(N_GROUPS)]
    eid_g = [sub + g * gs for g in range(N_GROUPS)]

    score = []
    for g in range(N_GROUPS):
        m1 = jnp.max(sb_g[g], axis=0, keepdims=True)
        first = jnp.min(jnp.where(sb_g[g] == m1, sub, gs), axis=0, keepdims=True)
        m2 = jnp.max(jnp.where(sub == first, neg, sb_g[g]), axis=0, keepdims=True)
        score.append(m1 + m2)

    chosen = [jnp.zeros((1, tm), jnp.int32) for _ in range(N_GROUPS)]
    for _ in range(TOPK_GROUPS):
        m = score[0]
        for g in range(1, N_GROUPS):
            m = jnp.maximum(m, score[g])
        gi = jnp.full((1, tm), N_GROUPS, jnp.int32)
        for g in range(N_GROUPS - 1, -1, -1):
            gi = jnp.where(score[g] == m, g, gi)
        for g in range(N_GROUPS):
            hit = gi == g
            chosen[g] = jnp.where(hit, 1, chosen[g])
            score[g] = jnp.where(hit, neg, score[g])
    cand = [jnp.where(chosen[g] > 0, sb_g[g], neg) for g in range(N_GROUPS)]

    onehot = [jnp.zeros((gs, tm), F32) for _ in range(N_GROUPS)]
    idxs, ws = [], []
    for _ in range(TOP_K):
        mx = cand[0]
        for g in range(1, N_GROUPS):
            mx = jnp.maximum(mx, cand[g])
        m = jnp.max(mx, axis=0, keepdims=True)
        lo = jnp.where(cand[0] == m, eid_g[0], N_EXPERTS)
        for g in range(1, N_GROUPS):
            lo = jnp.minimum(lo, jnp.where(cand[g] == m, eid_g[g], N_EXPERTS))
        idx = jnp.min(lo, axis=0, keepdims=True)
        wsum = jnp.zeros((gs, tm), F32)
        for g in range(N_GROUPS):
            hit = eid_g[g] == idx
            wsum = wsum + jnp.where(hit, s_g[g], 0.0)
            cand[g] = jnp.where(hit, neg, cand[g])
            onehot[g] = jnp.where(hit, 1.0, onehot[g])
        idxs.append(idx)
        ws.append(jnp.sum(wsum, axis=0, keepdims=True))
    total = ws[0]
    for k in range(1, TOP_K):
        total = total + ws[k]
    for k in range(TOP_K):
        idx_ref[k:k + 1, :] = idxs[k]
        w_ref[k:k + 1, :] = ws[k] / total * ROUTED_SCALE

    oh = jnp.concatenate(onehot, axis=0)
    r = lax.broadcasted_iota(jnp.int32, (tm, tm), 0)
    c = lax.broadcasted_iota(jnp.int32, (tm, tm), 1)
    before = (r < c).astype(BF16)
    pre = _dot(oh.astype(BF16), before) + carry_ref[...]
    for k in range(TOP_K):
        acc = jnp.zeros((gs, tm), F32)
        for g in range(N_GROUPS):
            acc = acc + jnp.where(eid_g[g] == idxs[k], pre[g * gs:(g + 1) * gs], 0.0)
        rank_ref[k:k + 1, :] = jnp.sum(acc, axis=0, keepdims=True).astype(jnp.int32)
    carry_ref[...] = carry_ref[...] + jnp.sum(oh, axis=1, keepdims=True)
    cnt_ref[...] = jnp.broadcast_to(carry_ref[...], cnt_ref.shape).astype(jnp.int32)


def _router(u, wr_t, rb, tm):
    N, D = u.shape
    col = lambda i: (0, i)
    fixed = lambda i: (0, 0)
    return pl.pallas_call(
        functools.partial(_router_kernel, tm=tm),
        grid=(N // tm,),
        in_specs=[
            pl.BlockSpec((tm, D), lambda i: (i, 0)),
            pl.BlockSpec((N_EXPERTS, D), fixed),
            pl.BlockSpec((N_EXPERTS, 1), fixed),
        ],
        out_specs=[
            pl.BlockSpec((TOP_K, tm), col),
            pl.BlockSpec((TOP_K, tm), col),
            pl.BlockSpec((TOP_K, tm), col),
            pl.BlockSpec((N_EXPERTS, LANES), fixed),
        ],
        out_shape=[
            jax.ShapeDtypeStruct((TOP_K, N), jnp.int32),
            jax.ShapeDtypeStruct((TOP_K, N), F32),
            jax.ShapeDtypeStruct((TOP_K, N), jnp.int32),
            jax.ShapeDtypeStruct((N_EXPERTS, LANES), jnp.int32),
        ],
        scratch_shapes=[pltpu.VMEM((N_EXPERTS, 1), F32)],
        compiler_params=_cparams("arbitrary"),
        name="router",
    )(u, wr_t, rb)


def _experts_kernel(be_ref, live_ref, x_ref, wg_ref, wu_ref, wd_ref, o_ref, wgb_ref, wub_ref, wdb_ref):
    i = pl.program_id(0)
    e = be_ref[i]
    prev = be_ref[jnp.maximum(i - 1, 0)]

    @pl.when((i == 0) | (e != prev))
    def _():
        wgb_ref[...] = wg_ref[...].astype(BF16)
        wub_ref[...] = wu_ref[...].astype(BF16)
        wdb_ref[...] = wd_ref[...].astype(BF16)

    @pl.when(live_ref[i] > 0)
    def _():
        x = x_ref[...]
        h = (_silu(_dot(x, wgb_ref[...])) * _dot(x, wub_ref[...])).astype(BF16)
        o_ref[...] = _dot(h, wdb_ref[...])

    @pl.when(live_ref[i] == 0)
    def _():
        o_ref[...] = jnp.zeros_like(o_ref)


def _experts(xs, block_e, live, wg, wu, wd, layer, bm):
    R, D = xs.shape
    return pl.pallas_call(
        _experts_kernel,
        grid_spec=pltpu.PrefetchScalarGridSpec(
            num_scalar_prefetch=2,
            grid=(R // bm,),
            in_specs=[
                pl.BlockSpec((bm, D), lambda i, be, lv: (i, 0)),
                pl.BlockSpec((None, None, D, D_EXPERT), lambda i, be, lv: (layer, be[i], 0, 0)),
                pl.BlockSpec((None, None, D, D_EXPERT), lambda i, be, lv: (layer, be[i], 0, 0)),
                pl.BlockSpec((None, None, D_EXPERT, D), lambda i, be, lv: (layer, be[i], 0, 0)),
            ],
            out_specs=pl.BlockSpec((bm, D), lambda i, be, lv: (i, 0)),
            scratch_shapes=[
                pltpu.VMEM((D, D_EXPERT), BF16),
                pltpu.VMEM((D, D_EXPERT), BF16),
                pltpu.VMEM((D_EXPERT, D), BF16),
            ],
        ),
        out_shape=jax.ShapeDtypeStruct((R, D), F32),
        compiler_params=_cparams("arbitrary"),
        name="experts",
    )(block_e, live, xs, wg, wu, wd)


def _moe_final_kernel(u_ref, x_ref, r_ref, wgu_ref, wd_ref, gate_ref, lg_ref, lb_ref, xo_ref):
    gu = _dot(u_ref[...], wgu_ref[...])
    h = (_silu(gu[:, :D_SHARED]) * gu[:, D_SHARED:]).astype(BF16)
    y = r_ref[...] + _dot(h, wd_ref[...])
    hres = DEEPNORM_ALPHA * x_ref[...] + (1.0 + gate_ref[0]) * y
    xo_ref[...] = _layer_norm(hres, lg_ref[...], lb_ref[...])


def _moe_final(u, x, routed, wgu, wd, gate, lg, lb, T, tm):
    N, D = x.shape
    g3, g_spec = _row_mod(gate, T, tm)
    row = lambda i: (i, 0)
    fixed = lambda i: (0, 0)
    return pl.pallas_call(
        _moe_final_kernel,
        grid=(N // tm,),
        in_specs=[
            pl.BlockSpec((tm, D), row),
            pl.BlockSpec((tm, D), row),
            pl.BlockSpec((tm, D), row),
            pl.BlockSpec(wgu.shape, fixed),
            pl.BlockSpec(wd.shape, fixed),
            g_spec,
            pl.BlockSpec((1, D), fixed),
            pl.BlockSpec((1, D), fixed),
        ],
        out_specs=pl.BlockSpec((tm, D), row),
        out_shape=jax.ShapeDtypeStruct((N, D), F32),
        compiler_params=_cparams("parallel"),
        name="moe_final",
    )(u, x, routed, wgu, wd, g3, lg.reshape(1, D), lb.reshape(1, D))


def _fox_kernel(q_ref, k_ref, v_ref, fq_ref, fk_ref, o_ref, m_ref, l_ref, acc_ref, *, bq):
    i = pl.program_id(1)
    j = pl.program_id(2)

    @pl.when(j == 0)
    def _():
        m_ref[...] = jnp.full_like(m_ref, MASK_NEG)
        l_ref[...] = jnp.zeros_like(l_ref)
        acc_ref[...] = jnp.zeros_like(acc_ref)

    @pl.when(j <= i)
    def _():
        r = i * bq + lax.broadcasted_iota(jnp.int32, (bq, bq), 0)
        c = j * bq + lax.broadcasted_iota(jnp.int32, (bq, bq), 1)
        visible = c <= r
        scale = FOX_HD ** -0.5
        for h in range(FOX_HEADS):
            hs = slice(h * FOX_HD, (h + 1) * FOX_HD)
            s = _dot_nt(q_ref[:, hs], k_ref[:, hs]) * scale + fq_ref[:, h:h + 1] - fk_ref[h:h + 1, :]
            s = jnp.where(visible, s, MASK_NEG)
            m_prev = m_ref[h]
            m_new = jnp.maximum(m_prev, jnp.max(s, axis=1, keepdims=True))
            alpha = jnp.exp(m_prev - m_new)
            p = jnp.exp(s - m_new)
            l_ref[h] = alpha * l_ref[h] + jnp.sum(p, axis=1, keepdims=True)
            acc_ref[:, hs] = alpha * acc_ref[:, hs] + _dot(p.astype(BF16), v_ref[:, hs])
            m_ref[h] = m_new

    @pl.when(j == pl.num_programs(2) - 1)
    def _():
        for h in range(FOX_HEADS):
            hs = slice(h * FOX_HD, (h + 1) * FOX_HD)
            o_ref[:, hs] = (acc_ref[:, hs] / l_ref[h]).astype(o_ref.dtype)


def _fox_attention(q, kb, vb, cum, cum_t, B, T, bq):
    nq = T // bq
    kv_map = lambda b, i, j: (b * nq + jnp.minimum(j, i), 0)
    return pl.pallas_call(
        functools.partial(_fox_kernel, bq=bq),
        grid=(B, nq, nq),
        in_specs=[
            pl.BlockSpec((bq, FOX_DIM), lambda b, i, j: (b * nq + i, 0)),
            pl.BlockSpec((bq, FOX_DIM), kv_map),
            pl.BlockSpec((bq, FOX_DIM), kv_map),
            pl.BlockSpec((bq, LANES), lambda b, i, j: (b * nq + i, 0)),
            pl.BlockSpec((FOX_HEADS, bq), lambda b, i, j: (b, jnp.minimum(j, i))),
        ],
        out_specs=pl.BlockSpec((bq, FOX_DIM), lambda b, i, j: (b * nq + i, 0)),
        out_shape=jax.ShapeDtypeStruct((B * T, FOX_DIM), BF16),
        scratch_shapes=[
            pltpu.VMEM((FOX_HEADS, bq, 1), F32),
            pltpu.VMEM((FOX_HEADS, bq, 1), F32),
            pltpu.VMEM((bq, FOX_DIM), F32),
        ],
        compiler_params=_cparams("parallel", "parallel", "arbitrary"),
        name="fox_attention",
    )(q, kb, vb, cum, cum_t)


def _decode_kernel(pt_ref, q_ref, kn_ref, vn_ref, fq_ref, fkn_ref, kc_ref, vc_ref, lf_ref, o_ref,
                   m_ref, l_ref, acc_ref, carry_ref, *, tq):
    j = pl.program_id(1)
    scale = FOX_HD ** -0.5

    def update(h, s, v):
        hs = slice(h * FOX_HD, (h + 1) * FOX_HD)
        m_prev = m_ref[h]
        m_new = jnp.maximum(m_prev, jnp.max(s, axis=1, keepdims=True))
        alpha = jnp.exp(m_prev - m_new)
        p = jnp.exp(s - m_new)
        l_ref[h] = alpha * l_ref[h] + jnp.sum(p, axis=1, keepdims=True)
        acc_ref[:, hs] = alpha * acc_ref[:, hs] + _dot(p.astype(BF16), v)
        m_ref[h] = m_new

    @pl.when(j == 0)
    def _():
        m_ref[...] = jnp.full_like(m_ref, MASK_NEG)
        l_ref[...] = jnp.zeros_like(l_ref)
        acc_ref[...] = jnp.zeros_like(acc_ref)
        carry_ref[...] = jnp.zeros_like(carry_ref)
        r = lax.broadcasted_iota(jnp.int32, (tq, PAGE_SIZE), 0)
        c = lax.broadcasted_iota(jnp.int32, (tq, PAGE_SIZE), 1)
        visible = c <= r
        for h in range(FOX_HEADS):
            hs = slice(h * FOX_HD, (h + 1) * FOX_HD)
            s = _dot_nt(q_ref[:, hs].astype(BF16), kn_ref[0, :, hs]) * scale + fq_ref[:, h:h + 1] - fkn_ref[0, h:h + 1, :]
            update(h, jnp.where(visible, s, MASK_NEG), vn_ref[0, :, hs])

    lf = lf_ref[...]
    r = lax.broadcasted_iota(jnp.int32, (PAGE_SIZE, PAGE_SIZE), 0)
    c = lax.broadcasted_iota(jnp.int32, (PAGE_SIZE, PAGE_SIZE), 1)
    after = (r > c).astype(BF16)
    suffix = _dot_exact_rhs(lf, after) + carry_ref[...]
    for h in range(FOX_HEADS):
        hs = slice(h * FOX_HD, (h + 1) * FOX_HD)
        s = _dot_nt(q_ref[:, hs].astype(BF16), kc_ref[:, hs].astype(BF16)) * scale + fq_ref[:, h:h + 1] + suffix[h:h + 1, :]
        update(h, s, vc_ref[:, hs].astype(BF16))
    carry_ref[...] = carry_ref[...] + jnp.sum(lf, axis=1, keepdims=True)

    @pl.when(j == pl.num_programs(1) - 1)
    def _():
        for h in range(FOX_HEADS):
            hs = slice(h * FOX_HD, (h + 1) * FOX_HD)
            o_ref[:, hs] = (acc_ref[:, hs] / l_ref[h]).astype(o_ref.dtype)


def _decode_attention(page_table, q, k_new, v_new, cum, cum_new_t, cache_k, cache_v, cache_lf_t, B, tq):
    n_pages = page_table.shape[1]
    page = lambda b, j, pt: (pt[b * n_pages + (n_pages - 1 - j)], 0, 0)
    return pl.pallas_call(
        functools.partial(_decode_kernel, tq=tq),
        grid_spec=pltpu.PrefetchScalarGridSpec(
            num_scalar_prefetch=1,
            grid=(B, n_pages),
            in_specs=[
                pl.BlockSpec((tq, FOX_DIM), lambda b, j, pt: (b, 0)),
                pl.BlockSpec((1, PAGE_SIZE, FOX_DIM), lambda b, j, pt: (b, 0, 0)),
                pl.BlockSpec((1, PAGE_SIZE, FOX_DIM), lambda b, j, pt: (b, 0, 0)),
                pl.BlockSpec((tq, LANES), lambda b, j, pt: (b, 0)),
                pl.BlockSpec((1, FOX_HEADS, PAGE_SIZE), lambda b, j, pt: (b, 0, 0)),
                pl.BlockSpec((None, PAGE_SIZE, FOX_DIM), page),
                pl.BlockSpec((None, PAGE_SIZE, FOX_DIM), page),
                pl.BlockSpec((None, FOX_HEADS, PAGE_SIZE), page),
            ],
            out_specs=pl.BlockSpec((tq, FOX_DIM), lambda b, j, pt: (b, 0)),
            scratch_shapes=[
                pltpu.VMEM((FOX_HEADS, tq, 1), F32),
                pltpu.VMEM((FOX_HEADS, tq, 1), F32),
                pltpu.VMEM((tq, FOX_DIM), F32),
                pltpu.VMEM((FOX_HEADS, 1), F32),
            ],
        ),
        out_shape=jax.ShapeDtypeStruct((B * tq, FOX_DIM), F32),
        compiler_params=_cparams("parallel", "arbitrary"),
        name="decode_attention",
    )(page_table.reshape(-1), q, k_new, v_new, cum, cum_new_t, cache_k, cache_v, cache_lf_t)


def _moe(u, x, W, l, gate, T, tm, bm):
    N, D = u.shape
    idx_t, w_t, rank_t, cnt = _router(u, W["router_wt"][l], W["router_bias"][l].reshape(N_EXPERTS, 1), tm)
    counts = cnt[:, 0]
    padded = (counts + bm - 1) // bm * bm
    pad_end = jnp.cumsum(padded)
    pad_start = pad_end - padded
    dest = (pad_start[idx_t] + rank_t).T
    n_blocks = N * TOP_K // bm + N_EXPERTS
    R = n_blocks * bm
    starts = jnp.arange(n_blocks, dtype=jnp.int32) * bm
    block_e = jnp.minimum(jnp.searchsorted(pad_end, starts, side="right"), N_EXPERTS - 1).astype(jnp.int32)
    live = (starts < pad_end[-1]).astype(jnp.int32)
    tok = jnp.broadcast_to(jnp.arange(N, dtype=jnp.int32)[:, None], (N, TOP_K))
    row_tok = jnp.zeros((R,), jnp.int32).at[dest.reshape(-1)].set(tok.reshape(-1), unique_indices=True)
    xs = u[row_tok]
    yb = _experts(xs, block_e, live, W["exp_w_gate"], W["exp_w_up"], W["exp_w_down"], l, bm)
    routed = jnp.sum(yb[dest] * w_t.T[:, :, None], axis=1)
    return _moe_final(u, x, routed, W["sh_wgu"][l], W["sh_wd"][l], gate, W["ln_g"][l, 1], W["ln_b"][l, 1], T, tm)


def _trunk(x, mods, kv_mod, s0, past, W, B, T):
    N = B * T
    tm = min(512, N)
    bm = 256 if N * TOP_K >= 256 * N_EXPERTS else 128
    x = x.reshape(N, D_MODEL)
    new_gla = []
    for l in range(DEPTH):
        sh_a, sc_a, g_a, sh_m, sc_m, g_m = jnp.split(mods[l], 6, axis=-1)
        if l < N_A:
            p = _modproj(x, sc_a, sh_a, W["gla_w_in"][l], T, tm, F32)
            if T >= GLA_CHUNK:
                Tp, tc = T, min(512, T)
            else:
                Tp = tc = GLA_CHUNK
                p = jnp.pad(p.reshape(B, T, -1), ((0, 0), (0, Tp - T), (0, 0))).reshape(B * Tp, -1)
            o, s_T = _gla(p, W["gla_w_a2"][l], W["gla_b_a"][l], W["gla_norm_g"][l], s0[l], B, Tp, tc, T)
            if Tp != T:
                o = o.reshape(B, Tp, -1)[:, :T].reshape(N, -1)
            new_gla.append(s_T)
            w_o = W["gla_w_o"][l]
        else:
            q = _modproj(x, sc_a, sh_a, W["fox_w_q"][l - N_A], T, tm, BF16 if past is None else F32)
            if past is None:
                o = _fox_attention(q, kb, vb, cum, cum_t, B, T, min(256, T))
            else:
                o = _decode_attention(past["page_table"], q, kb_pad, vb_pad, cum, cum_t, past["k"], past["v"],
                                      past["lf_t"], B, T)
            w_o = W["fox_w_o"][l - N_A]
        x, u = _proj_res_ln(o, w_o, x, g_a, W["ln_g"][l, 0], W["ln_b"][l, 0], sc_m, sh_m, T, tm)
        x = _moe(u, x, W, l, g_m, T, tm, bm)
        if l == N_A - 1:
            sh_kv, sc_kv = jnp.split(kv_mod, 2, axis=-1)
            k_new, v_new, kb, vb, lf, cum = _kv_proj(x, sc_kv, sh_kv, W["kv_w"], W["fox_b_f"], T, tm)
            cum_h = cum[:, :FOX_HEADS].reshape(B, T, FOX_HEADS)
            if past is None:
                cum_t = cum_h.transpose(0, 2, 1).reshape(B * FOX_HEADS, T)
            else:
                rows = ((0, 0), (0, PAGE_SIZE - T), (0, 0))
                cum_t = jnp.pad(cum_h, rows).transpose(0, 2, 1)
                kb_pad = jnp.pad(kb.reshape(B, T, FOX_DIM), rows)
                vb_pad = jnp.pad(vb.reshape(B, T, FOX_DIM), rows)
    return (x.reshape(B, T, D_MODEL), jnp.stack(new_gla),
            k_new.reshape(B, T, FOX_HEADS, FOX_HD), v_new.reshape(B, T, FOX_HEADS, FOX_HD),
            lf[:, :FOX_HEADS].reshape(B, T, FOX_HEADS))


def kernel(x_prompt, x_sample, c_prompt, c_sample, state_gla, cache_k, cache_v, cache_logf, page_table, gla_w_in, gla_w_a2, gla_b_a, gla_norm_g, gla_w_o, kv_w_mod, kv_b_mod, kv_w, fox_b_f, fox_w_q, fox_w_o, ada_w, ada_b, ln_g, ln_b, router_w, router_bias, exp_w_gate, exp_w_up, exp_w_down, sh_w_gate, sh_w_up, sh_w_down):
    Bp, Tp, D = x_prompt.shape
    Bs, Ts, _ = x_sample.shape
    lane_pad = lambda w, n: jnp.pad(w, [(0, 0)] * (w.ndim - 1) + [(0, n - w.shape[-1])])
    W = dict(
        gla_w_in=jnp.concatenate(
            [gla_w_in[..., :GLA_IN_MAIN], lane_pad(gla_w_in[..., GLA_IN_MAIN:], LANES)], axis=-1).astype(BF16),
        gla_w_a2=jnp.pad(gla_w_a2, ((0, 0), (0, LANES - GLA_GATE_RANK), (0, 0))).astype(BF16),
        gla_b_a=gla_b_a.reshape(N_A, 1, GLA_DK),
        gla_norm_g=gla_norm_g.reshape(N_A, 1, GLA_DVH),
        gla_w_o=gla_w_o.astype(BF16),
        kv_w=jnp.concatenate([kv_w[:, :2 * FOX_DIM], lane_pad(kv_w[:, 2 * FOX_DIM:], LANES)], axis=-1).astype(BF16),
        fox_b_f=lane_pad(fox_b_f.reshape(1, FOX_HEADS), LANES),
        fox_w_q=fox_w_q.astype(BF16),
        fox_w_o=fox_w_o.astype(BF16),
        ln_g=ln_g, ln_b=ln_b,
        router_wt=jnp.swapaxes(router_w, 1, 2).astype(BF16),
        router_bias=router_bias,
        exp_w_gate=exp_w_gate, exp_w_up=exp_w_up, exp_w_down=exp_w_down,
        sh_wgu=jnp.concatenate([sh_w_gate, sh_w_up], axis=-1).astype(BF16),
        sh_wd=sh_w_down.astype(BF16),
    )
    c_all = jnp.concatenate([c_prompt, c_sample], axis=0)
    mods = _cond_linear(c_all, ada_w, ada_b, 1536)
    kv_mod = _cond_linear(c_all, kv_w_mod[None], kv_b_mod[None], 1024)[0]

    s0_prompt = jnp.zeros((N_A, Bp, GLA_HEADS, GLA_DKH, GLA_DVH), F32)
    out_p = _trunk(x_prompt, mods[:, :Bp], kv_mod[:Bp], s0_prompt, None, W, Bp, Tp)

    n_phys = cache_k.shape[0]
    past = dict(
        page_table=page_table,
        k=cache_k.reshape(n_phys, PAGE_SIZE, FOX_DIM),
        v=cache_v.reshape(n_phys, PAGE_SIZE, FOX_DIM),
        lf_t=jnp.swapaxes(cache_logf, 1, 2),
    )
    out_s = _trunk(x_sample, mods[:, Bp:], kv_mod[Bp:], state_gla, past, W, Bs, Ts)

    y_p, gla_p, k_p, v_p, lf_p = out_p
    y_s, gla_s, k_s, v_s, lf_s = out_s
    return (y_p, y_s, gla_p, gla_s, k_p, v_p, lf_p, k_s, v_s, lf_s)
```

```python
import functools

import jax
import jax.numpy as jnp
from jax import lax
from jax.experimental import pallas as pl
from jax.experimental.pallas import tpu as pltpu

F32 = jnp.float32
BF16 = jnp.bfloat16

D_MODEL = 1024
DEPTH = 4
N_A = DEPTH // 2
PAGE_SIZE = 128
GLA_HEADS = 4
GLA_DK = D_MODEL // 2
GLA_DV = D_MODEL
GLA_DKH = GLA_DK // GLA_HEADS
GLA_DVH = GLA_DV // GLA_HEADS
GLA_GATE_RANK = 16
GLA_GATE_TAU = 16.0
GLA_CHUNK = 64
FOX_HEADS = 8
FOX_HD = D_MODEL // FOX_HEADS
FOX_DIM = FOX_HEADS * FOX_HD
N_EXPERTS = 64
TOP_K = 8
N_GROUPS = 8
GROUP_SIZE = N_EXPERTS // N_GROUPS
TOPK_GROUPS = 4
D_EXPERT = 256
D_SHARED = 256
ROUTED_SCALE = 2.5
DEEPNORM_ALPHA = (2 * DEPTH) ** 0.25
LN_EPS = 1e-5
RMS_EPS = 1e-6

LANES = 128
GLA_IN_MAIN = 2 * GLA_DK + 2 * GLA_DV
GLA_IN_PAD = GLA_IN_MAIN + LANES
KV_PAD = 2 * FOX_DIM + LANES
MASK_NEG = -1e30
VMEM_LIMIT = 52 * 1024 * 1024


def _cparams(*sem):
    return pltpu.CompilerParams(dimension_semantics=sem, vmem_limit_bytes=VMEM_LIMIT)


def _dot(a, b):
    return jnp.dot(a, b, preferred_element_type=F32)


def _dot_nt(a, b):
    return lax.dot_general(a, b, (((1,), (1,)), ((), ())), preferred_element_type=F32)


def _dot_tn(a, b):
    return lax.dot_general(a, b, (((0,), (0,)), ((), ())), preferred_element_type=F32)


def _sigmoid(x):
    return 1.0 / (1.0 + jnp.exp(-x))


def _silu(x):
    return x * _sigmoid(x)


def _log_sigmoid(x):
    return jnp.minimum(x, 0.0) - jnp.log1p(jnp.exp(-jnp.abs(x)))


def _split3(x):
    hi = x.astype(BF16)
    r = x - hi.astype(F32)
    mid = r.astype(BF16)
    lo = (r - mid.astype(F32)).astype(BF16)
    return hi, mid, lo


def _dot_exact_lhs(mat01, x):
    hi, mid, lo = _split3(x)
    return _dot(mat01, hi) + _dot(mat01, mid) + _dot(mat01, lo)


def _dot_exact_rhs(x, mat01):
    hi, mid, lo = _split3(x)
    return _dot(hi, mat01) + _dot(mid, mat01) + _dot(lo, mat01)


def _layer_norm(h, g, b):
    mu = jnp.mean(h, axis=-1, keepdims=True)
    xc = h - mu
    var = jnp.mean(xc * xc, axis=-1, keepdims=True)
    return xc * lax.rsqrt(var + LN_EPS) * g + b


def _row_mod(m, T, tm):
    B, D = m.shape
    if T % tm == 0:
        tiles_per_batch = T // tm
        return m.reshape(B, 1, D), pl.BlockSpec((1, 1, D), lambda i: (i // tiles_per_batch, 0, 0))
    assert tm % T == 0
    e = jnp.repeat(m, T, axis=0).reshape(B * T // tm, tm, D)
    return e, pl.BlockSpec((1, tm, D), lambda i: (i, 0, 0))


def _cond_kernel(c_ref, w_ref, b_ref, o_ref):
    s = _silu(c_ref[...]).astype(BF16)
    o_ref[0] = _dot(s, w_ref[0].astype(BF16)) + b_ref[0]


def _cond_linear(c, w, b, tn):
    M, D = c.shape
    L, _, N = w.shape
    return pl.pallas_call(
        _cond_kernel,
        grid=(L, N // tn),
        in_specs=[
            pl.BlockSpec((M, D), lambda l, j: (0, 0)),
            pl.BlockSpec((1, D, tn), lambda l, j: (l, 0, j)),
            pl.BlockSpec((1, 1, tn), lambda l, j: (l, 0, j)),
        ],
        out_specs=pl.BlockSpec((1, M, tn), lambda l, j: (l, 0, j)),
        out_shape=jax.ShapeDtypeStruct((L, M, N), F32),
        compiler_params=_cparams("parallel", "parallel"),
        name="cond_linear",
    )(c, w, b.reshape(L, 1, N))


def _modproj_kernel(x_ref, sc_ref, sh_ref, w_ref, o_ref):
    u = (x_ref[...] * (1.0 + sc_ref[0]) + sh_ref[0]).astype(BF16)
    o_ref[...] = _dot(u, w_ref[...]).astype(o_ref.dtype)


def _modproj(x, sc, sh, w, T, tm, out_dtype):
    N, D = x.shape
    Dout = w.shape[1]
    sc3, sc_spec = _row_mod(sc, T, tm)
    sh3, sh_spec = _row_mod(sh, T, tm)
    return pl.pallas_call(
        _modproj_kernel,
        grid=(N // tm,),
        in_specs=[
            pl.BlockSpec((tm, D), lambda i: (i, 0)),
            sc_spec,
            sh_spec,
            pl.BlockSpec((D, Dout), lambda i: (0, 0)),
        ],
        out_specs=pl.BlockSpec((tm, Dout), lambda i: (i, 0)),
        out_shape=jax.ShapeDtypeStruct((N, Dout), out_dtype),
        compiler_params=_cparams("parallel"),
        name="modproj",
    )(x, sc3, sh3, w)


def _kv_kernel(x_ref, sc_ref, sh_ref, w_ref, bf_ref, k_ref, v_ref, kb_ref, vb_ref, lf_ref, cum_ref,
               carry_ref, *, T, tm):
    u = (x_ref[...] * (1.0 + sc_ref[0]) + sh_ref[0]).astype(BF16)
    p = _dot(u, w_ref[...])
    k = p[:, :FOX_DIM]
    v = p[:, FOX_DIM:2 * FOX_DIM]
    k_ref[...] = k
    v_ref[...] = v
    kb_ref[...] = k.astype(BF16)
    vb_ref[...] = v.astype(BF16)
    lf = _log_sigmoid(p[:, 2 * FOX_DIM:] + bf_ref[...])
    lf_ref[...] = lf
    if T % tm == 0:
        @pl.when(pl.program_id(0) % (T // tm) == 0)
        def _():
            carry_ref[...] = jnp.zeros_like(carry_ref)

        sb = LANES
        r = lax.broadcasted_iota(jnp.int32, (sb, sb), 0)
        c = lax.broadcasted_iota(jnp.int32, (sb, sb), 1)
        tri = (c <= r).astype(BF16)
        carry = carry_ref[...]
        for s in range(tm // sb):
            cs = _dot_exact_lhs(tri, lf[s * sb:(s + 1) * sb]) + carry
            cum_ref[s * sb:(s + 1) * sb, :] = cs
            carry = cs[sb - 1:sb, :]
        carry_ref[...] = carry
    else:
        r = lax.broadcasted_iota(jnp.int32, (tm, tm), 0)
        c = lax.broadcasted_iota(jnp.int32, (tm, tm), 1)
        tri = ((c <= r) & ((r // T) == (c // T))).astype(BF16)
        cum_ref[...] = _dot_exact_lhs(tri, lf)


def _kv_proj(x, sc, sh, w, bf, T, tm):
    N, D = x.shape
    sc3, sc_spec = _row_mod(sc, T, tm)
    sh3, sh_spec = _row_mod(sh, T, tm)
    row = lambda i: (i, 0)
    return pl.pallas_call(
        functools.partial(_kv_kernel, T=T, tm=tm),
        grid=(N // tm,),
        in_specs=[
            pl.BlockSpec((tm, D), row),
            sc_spec,
            sh_spec,
            pl.BlockSpec((D, KV_PAD), lambda i: (0, 0)),
            pl.BlockSpec((1, LANES), lambda i: (0, 0)),
        ],
        out_specs=[
            pl.BlockSpec((tm, FOX_DIM), row),
            pl.BlockSpec((tm, FOX_DIM), row),
            pl.BlockSpec((tm, FOX_DIM), row),
            pl.BlockSpec((tm, FOX_DIM), row),
            pl.BlockSpec((tm, LANES), row),
            pl.BlockSpec((tm, LANES), row),
        ],
        out_shape=[
            jax.ShapeDtypeStruct((N, FOX_DIM), F32),
            jax.ShapeDtypeStruct((N, FOX_DIM), F32),
            jax.ShapeDtypeStruct((N, FOX_DIM), BF16),
            jax.ShapeDtypeStruct((N, FOX_DIM), BF16),
            jax.ShapeDtypeStruct((N, LANES), F32),
            jax.ShapeDtypeStruct((N, LANES), F32),
        ],
        scratch_shapes=[pltpu.VMEM((1, LANES), F32)],
        compiler_params=_cparams("arbitrary"),
        name="kv_proj",
    )(x, sc3, sh3, w, bf)


def _gla_kernel(q_ref, k_ref, v_ref, g_ref, a_ref, wa_ref, ba_ref, ng_ref, s0_ref, o_ref, st_ref,
                state_ref, *, c, nsub, t_valid):
    t = pl.program_id(2)

    @pl.when(t == 0)
    def _():
        state_ref[...] = s0_ref[0, 0].T

    r = lax.broadcasted_iota(jnp.int32, (c, c), 0)
    cc = lax.broadcasted_iota(jnp.int32, (c, c), 1)
    causal = cc <= r
    tri = causal.astype(BF16)
    scale = GLA_DKH ** -0.5
    for i in range(nsub):
        sl = slice(i * c, (i + 1) * c)
        z = _dot(a_ref[sl, :].astype(BF16), wa_ref[...]) + ba_ref[...]
        la = _log_sigmoid(z) / GLA_GATE_TAU
        if t_valid < c:
            la = jnp.where(lax.broadcasted_iota(jnp.int32, la.shape, 0) < t_valid, la, 0.0)
        bc = _dot_exact_lhs(tri, la)
        b_last = bc[c - 1:c, :]
        k = k_ref[sl, :]
        v = v_ref[sl, :].astype(BF16)
        q_dec = ((q_ref[sl, :] * scale) * jnp.exp(bc)).astype(BF16)
        k_dec = (k * jnp.exp(-bc)).astype(BF16)
        k_end = (k * jnp.exp(b_last - bc)).astype(BF16)
        st = state_ref[...]
        inter = _dot_nt(q_dec, st.astype(BF16))
        att = jnp.where(causal, _dot_nt(q_dec, k_dec), 0.0).astype(BF16)
        o = inter + _dot(att, v)
        state_ref[...] = st * jnp.exp(b_last) + _dot_tn(v, k_end)
        o = o * lax.rsqrt(jnp.mean(o * o, axis=-1, keepdims=True) + RMS_EPS) * ng_ref[...]
        o_ref[sl, :] = (o * _silu(g_ref[sl, :])).astype(o_ref.dtype)

    @pl.when(t == pl.num_programs(2) - 1)
    def _():
        st_ref[0, 0] = state_ref[...].T


def _gla(p, wa, ba, ng, s0, B, Tp, tc, t_valid):
    nt = Tp // tc
    c = GLA_CHUNK
    row = lambda b, h, t: b * nt + t
    kq = GLA_DK // GLA_DKH
    kv = 2 * GLA_DK // GLA_DVH
    kg = kv + GLA_HEADS
    ka = GLA_IN_MAIN // LANES
    return pl.pallas_call(
        functools.partial(_gla_kernel, c=c, nsub=tc // c, t_valid=t_valid),
        grid=(B, GLA_HEADS, nt),
        in_specs=[
            pl.BlockSpec((tc, GLA_DKH), lambda b, h, t: (row(b, h, t), h)),
            pl.BlockSpec((tc, GLA_DKH), lambda b, h, t: (row(b, h, t), kq + h)),
            pl.BlockSpec((tc, GLA_DVH), lambda b, h, t: (row(b, h, t), kv + h)),
            pl.BlockSpec((tc, GLA_DVH), lambda b, h, t: (row(b, h, t), kg + h)),
            pl.BlockSpec((tc, LANES), lambda b, h, t: (row(b, h, t), ka)),
            pl.BlockSpec((LANES, GLA_DKH), lambda b, h, t: (0, h)),
            pl.BlockSpec((1, GLA_DKH), lambda b, h, t: (0, h)),
            pl.BlockSpec((1, GLA_DVH), lambda b, h, t: (0, 0)),
            pl.BlockSpec((1, 1, GLA_DKH, GLA_DVH), lambda b, h, t: (b, h, 0, 0)),
        ],
        out_specs=[
            pl.BlockSpec((tc, GLA_DVH), lambda b, h, t: (row(b, h, t), h)),
            pl.BlockSpec((1, 1, GLA_DKH, GLA_DVH), lambda b, h, t: (b, h, 0, 0)),
        ],
        out_shape=[
            jax.ShapeDtypeStruct((B * Tp, GLA_DV), BF16),
            jax.ShapeDtypeStruct((B, GLA_HEADS, GLA_DKH, GLA_DVH), F32),
        ],
        scratch_shapes=[pltpu.VMEM((GLA_DVH, GLA_DKH), F32)],
        compiler_params=_cparams("parallel", "parallel", "arbitrary"),
        name="gla",
    )(p, p, p, p, p, wa, ba, ng, s0)


def _proj_res_ln_kernel(a_ref, w_ref, x_ref, gate_ref, lg_ref, lb_ref, sc_ref, sh_ref, xo_ref, uo_ref):
    y = _dot(a_ref[...].astype(BF16), w_ref[...])
    h = DEEPNORM_ALPHA * x_ref[...] + (1.0 + gate_ref[0]) * y
    xn = _layer_norm(h, lg_ref[...], lb_ref[...])
    xo_ref[...] = xn
    uo_ref[...] = (xn * (1.0 + sc_ref[0]) + sh_ref[0]).astype(uo_ref.dtype)


def _proj_res_ln(a, w, x, gate, lg, lb, sc, sh, T, tm):
    N, D = x.shape
    g3, g_spec = _row_mod(gate, T, tm)
    sc3, sc_spec = _row_mod(sc, T, tm)
    sh3, sh_spec = _row_mod(sh, T, tm)
    row = lambda i: (i, 0)
    fixed = lambda i: (0, 0)
    return pl.pallas_call(
        _proj_res_ln_kernel,
        grid=(N // tm,),
        in_specs=[
            pl.BlockSpec((tm, a.shape[1]), row),
            pl.BlockSpec(w.shape, fixed),
            pl.BlockSpec((tm, D), row),
            g_spec,
            pl.BlockSpec((1, D), fixed),
            pl.BlockSpec((1, D), fixed),
            sc_spec,
            sh_spec,
        ],
        out_specs=[pl.BlockSpec((tm, D), row), pl.BlockSpec((tm, D), row)],
        out_shape=[jax.ShapeDtypeStruct((N, D), F32), jax.ShapeDtypeStruct((N, D), BF16)],
        compiler_params=_cparams("parallel"),
        name="proj_res_ln",
    )(a, w, x, g3, lg.reshape(1, D), lb.reshape(1, D), sc3, sh3)


def _router_kernel(u_ref, wr_ref, rb_ref, idx_ref, w_ref, rank_ref, cnt_ref, carry_ref, *, tm):
    @pl.when(pl.program_id(0) == 0)
    def _():
        carry_ref[...] = jnp.zeros_like(carry_ref)

    s = _sigmoid(_dot_nt(wr_ref[...], u_ref[...]))
    sb = s + rb_ref[...]
    neg = -jnp.inf
    sub = lax.broadcasted_iota(jnp.int32, (GROUP_SIZE, tm), 0)
    gs = GROUP_SIZE
    s_g = [s[g * gs:(g + 1) * gs] for g in range(N_GROUPS)]
    sb_g = [sb[g * gs:(g + 1) * gs] for g in range(N_GROUPS)]
    eid_g = [sub + g * gs for g in range(N_GROUPS)]

    score = []
    for g in range(N_GROUPS):
        m1 = jnp.max(sb_g[g], axis=0, keepdims=True)
        first = jnp.min(jnp.where(sb_g[g] == m1, sub, gs), axis=0, keepdims=True)
        m2 = jnp.max(jnp.where(sub == first, neg, sb_g[g]), axis=0, keepdims=True)
        score.append(m1 + m2)

    chosen = [jnp.zeros((1, tm), jnp.int32) for _ in range(N_GROUPS)]
    for _ in range(TOPK_GROUPS):
        m = score[0]
        for g in range(1, N_GROUPS):
            m = jnp.maximum(m, score[g])
        gi = jnp.full((1, tm), N_GROUPS, jnp.int32)
        for g in range(N_GROUPS - 1, -1, -1):
            gi = jnp.where(score[g] == m, g, gi)
        for g in range(N_GROUPS):
            hit = gi == g
            chosen[g] = jnp.where(hit, 1, chosen[g])
            score[g] = jnp.where(hit, neg, score[g])
    cand = [jnp.where(chosen[g] > 0, sb_g[g], neg) for g in range(N_GROUPS)]

    onehot = [jnp.zeros((gs, tm), F32) for _ in range(N_GROUPS)]
    idxs, ws = [], []
    for _ in range(TOP_K):
        mx = cand[0]
        for g in range(1, N_GROUPS):
            mx = jnp.maximum(mx, cand[g])
        m = jnp.max(mx, axis=0, keepdims=True)
        lo = jnp.where(cand[0] == m, eid_g[0], N_EXPERTS)
        for g in range(1, N_GROUPS):
            lo = jnp.minimum(lo, jnp.where(cand[g] == m, eid_g[g], N_EXPERTS))
        idx = jnp.min(lo, axis=0, keepdims=True)
        wsum = jnp.zeros((gs, tm), F32)
        for g in range(N_GROUPS):
            hit = eid_g[g] == idx
            wsum = wsum + jnp.where(hit, s_g[g], 0.0)
            cand[g] = jnp.where(hit, neg, cand[g])
            onehot[g] = jnp.where(hit, 1.0, onehot[g])
        idxs.append(idx)
        ws.append(jnp.sum(wsum, axis=0, keepdims=True))
    total = ws[0]
    for k in range(1, TOP_K):
        total = total + ws[k]
    for k in range(TOP_K):
        idx_ref[k:k + 1, :] = idxs[k]
        w_ref[k:k + 1, :] = ws[k] / total * ROUTED_SCALE

    oh = jnp.concatenate(onehot, axis=0)
    r = lax.broadcasted_iota(jnp.int32, (tm, tm), 0)
    c = lax.broadcasted_iota(jnp.int32, (tm, tm), 1)
    before = (r < c).astype(BF16)
    pre = _dot(oh.astype(BF16), before) + carry_ref[...]
    for k in range(TOP_K):
        acc = jnp.zeros((gs, tm), F32)
        for g in range(N_GROUPS):
            acc = acc + jnp.where(eid_g[g] == idxs[k], pre[g * gs:(g + 1) * gs], 0.0)
        rank_ref[k:k + 1, :] = jnp.sum(acc, axis=0, keepdims=True).astype(jnp.int32)
    carry_ref[...] = carry_ref[...] + jnp.sum(oh, axis=1, keepdims=True)
    cnt_ref[...] = jnp.broadcast_to(carry_ref[...], cnt_ref.shape).astype(jnp.int32)


def _router(u, wr_t, rb, tm):
    N, D = u.shape
    col = lambda i: (0, i)
    fixed = lambda i: (0, 0)
    return pl.pallas_call(
        functools.partial(_router_kernel, tm=tm),
        grid=(N // tm,),
        in_specs=[
            pl.BlockSpec((tm, D), lambda i: (i, 0)),
            pl.BlockSpec((N_EXPERTS, D), fixed),
            pl.BlockSpec((N_EXPERTS, 1), fixed),
        ],
        out_specs=[
            pl.BlockSpec((TOP_K, tm), col),
            pl.BlockSpec((TOP_K, tm), col),
            pl.BlockSpec((TOP_K, tm), col),
            pl.BlockSpec((N_EXPERTS, LANES), fixed),
        ],
        out_shape=[
            jax.ShapeDtypeStruct((TOP_K, N), jnp.int32),
            jax.ShapeDtypeStruct((TOP_K, N), F32),
            jax.ShapeDtypeStruct((TOP_K, N), jnp.int32),
            jax.ShapeDtypeStruct((N_EXPERTS, LANES), jnp.int32),
        ],
        scratch_shapes=[pltpu.VMEM((N_EXPERTS, 1), F32)],
        compiler_params=_cparams("arbitrary"),
        name="router",
    )(u, wr_t, rb)


def _experts_kernel(be_ref, live_ref, x_ref, wg_ref, wu_ref, wd_ref, o_ref, wgb_ref, wub_ref, wdb_ref):
    i = pl.program_id(0)
    e = be_ref[i]
    prev = be_ref[jnp.maximum(i - 1, 0)]

    @pl.when((i == 0) | (e != prev))
    def _():
        wgb_ref[...] = wg_ref[...].astype(BF16)
        wub_ref[...] = wu_ref[...].astype(BF16)
        wdb_ref[...] = wd_ref[...].astype(BF16)

    @pl.when(live_ref[i] > 0)
    def _():
        x = x_ref[...]
        h = (_silu(_dot(x, wgb_ref[...])) * _dot(x, wub_ref[...])).astype(BF16)
        o_ref[...] = _dot(h, wdb_ref[...])

    @pl.when(live_ref[i] == 0)
    def _():
        o_ref[...] = jnp.zeros_like(o_ref)


def _experts(xs, block_e, live, wg, wu, wd, layer, bm):
    R, D = xs.shape
    return pl.pallas_call(
        _experts_kernel,
        grid_spec=pltpu.PrefetchScalarGridSpec(
            num_scalar_prefetch=2,
            grid=(R // bm,),
            in_specs=[
                pl.BlockSpec((bm, D), lambda i, be, lv: (i, 0)),
                pl.BlockSpec((None, None, D, D_EXPERT), lambda i, be, lv: (layer, be[i], 0, 0)),
                pl.BlockSpec((None, None, D, D_EXPERT), lambda i, be, lv: (layer, be[i], 0, 0)),
                pl.BlockSpec((None, None, D_EXPERT, D), lambda i, be, lv: (layer, be[i], 0, 0)),
            ],
            out_specs=pl.BlockSpec((bm, D), lambda i, be, lv: (i, 0)),
            scratch_shapes=[
                pltpu.VMEM((D, D_EXPERT), BF16),
                pltpu.VMEM((D, D_EXPERT), BF16),
                pltpu.VMEM((D_EXPERT, D), BF16),
            ],
        ),
        out_shape=jax.ShapeDtypeStruct((R, D), F32),
        compiler_params=_cparams("arbitrary"),
        name="experts",
    )(block_e, live, xs, wg, wu, wd)


def _moe_final_kernel(u_ref, x_ref, r_ref, wgu_ref, wd_ref, gate_ref, lg_ref, lb_ref, xo_ref):
    gu = _dot(u_ref[...], wgu_ref[...])
    h = (_silu(gu[:, :D_SHARED]) * gu[:, D_SHARED:]).astype(BF16)
    y = r_ref[...] + _dot(h, wd_ref[...])
    hres = DEEPNORM_ALPHA * x_ref[...] + (1.0 + gate_ref[0]) * y
    xo_ref[...] = _layer_norm(hres, lg_ref[...], lb_ref[...])


def _moe_final(u, x, routed, wgu, wd, gate, lg, lb, T, tm):
    N, D = x.shape
    g3, g_spec = _row_mod(gate, T, tm)
    row = lambda i: (i, 0)
    fixed = lambda i: (0, 0)
    return pl.pallas_call(
        _moe_final_kernel,
        grid=(N // tm,),
        in_specs=[
            pl.BlockSpec((tm, D), row),
            pl.BlockSpec((tm, D), row),
            pl.BlockSpec((tm, D), row),
            pl.BlockSpec(wgu.shape, fixed),
            pl.BlockSpec(wd.shape, fixed),
            g_spec,
            pl.BlockSpec((1, D), fixed),
            pl.BlockSpec((1, D), fixed),
        ],
        out_specs=pl.BlockSpec((tm, D), row),
        out_shape=jax.ShapeDtypeStruct((N, D), F32),
        compiler_params=_cparams("parallel"),
        name="moe_final",
    )(u, x, routed, wgu, wd, g3, lg.reshape(1, D), lb.reshape(1, D))


def _fox_kernel(q_ref, k_ref, v_ref, fq_ref, fk_ref, o_ref, m_ref, l_ref, acc_ref, *, bq):
    i = pl.program_id(1)
    j = pl.program_id(2)

    @pl.when(j == 0)
    def _():
        m_ref[...] = jnp.full_like(m_ref, MASK_NEG)
        l_ref[...] = jnp.zeros_like(l_ref)
        acc_ref[...] = jnp.zeros_like(acc_ref)

    @pl.when(j <= i)
    def _():
        r = i * bq + lax.broadcasted_iota(jnp.int32, (bq, bq), 0)
        c = j * bq + lax.broadcasted_iota(jnp.int32, (bq, bq), 1)
        visible = c <= r
        scale = FOX_HD ** -0.5
        for h in range(FOX_HEADS):
            hs = slice(h * FOX_HD, (h + 1) * FOX_HD)
            s = _dot_nt(q_ref[:, hs], k_ref[:, hs]) * scale + fq_ref[:, h:h + 1] - fk_ref[h:h + 1, :]
            s = jnp.where(visible, s, MASK_NEG)
            m_prev = m_ref[h]
            m_new = jnp.maximum(m_prev, jnp.max(s, axis=1, keepdims=True))
            alpha = jnp.exp(m_prev - m_new)
            p = jnp.exp(s - m_new)
            l_ref[h] = alpha * l_ref[h] + jnp.sum(p, axis=1, keepdims=True)
            acc_ref[:, hs] = alpha * acc_ref[:, hs] + _dot(p.astype(BF16), v_ref[:, hs])
            m_ref[h] = m_new

    @pl.when(j == pl.num_programs(2) - 1)
    def _():
        for h in range(FOX_HEADS):
            hs = slice(h * FOX_HD, (h + 1) * FOX_HD)
            o_ref[:, hs] = (acc_ref[:, hs] / l_ref[h]).astype(o_ref.dtype)


def _fox_attention(q, kb, vb, cum, cum_t, B, T, bq):
    nq = T // bq
    kv_map = lambda b, i, j: (b * nq + jnp.minimum(j, i), 0)
    return pl.pallas_call(
        functools.partial(_fox_kernel, bq=bq),
        grid=(B, nq, nq),
        in_specs=[
            pl.BlockSpec((bq, FOX_DIM), lambda b, i, j: (b * nq + i, 0)),
            pl.BlockSpec((bq, FOX_DIM), kv_map),
            pl.BlockSpec((bq, FOX_DIM), kv_map),
            pl.BlockSpec((bq, LANES), lambda b, i, j: (b * nq + i, 0)),
            pl.BlockSpec((FOX_HEADS, bq), lambda b, i, j: (b, jnp.minimum(j, i))),
        ],
        out_specs=pl.BlockSpec((bq, FOX_DIM), lambda b, i, j: (b * nq + i, 0)),
        out_shape=jax.ShapeDtypeStruct((B * T, FOX_DIM), BF16),
        scratch_shapes=[
            pltpu.VMEM((FOX_HEADS, bq, 1), F32),
            pltpu.VMEM((FOX_HEADS, bq, 1), F32),
            pltpu.VMEM((bq, FOX_DIM), F32),
        ],
        compiler_params=_cparams("parallel", "parallel", "arbitrary"),
        name="fox_attention",
    )(q, kb, vb, cum, cum_t)


def _decode_kernel(pt_ref, q_ref, kn_ref, vn_ref, fq_ref, fkn_ref, *rest, tq, npg):
    kc, vc, lfc = rest[:npg], rest[npg:2 * npg], rest[2 * npg:3 * npg]
    o_ref, m_ref, l_ref, acc_ref, carry_ref = rest[3 * npg:]
    j = pl.program_id(1)
    H = FOX_HEADS
    scale = FOX_HD ** -0.5
    q = [q_ref[:, h * FOX_HD:(h + 1) * FOX_HD].astype(BF16) for h in range(H)]
    fq = jnp.concatenate([fq_ref[:, h:h + 1] for h in range(H)], axis=0)

    def per_head_rows(x):
        return jnp.concatenate([jnp.broadcast_to(x[h:h + 1, :], (tq, x.shape[1])) for h in range(H)], axis=0)

    def attend(s, values, state):
        m_prev, l_prev, acc_prev = state
        m_new = jnp.maximum(m_prev, jnp.max(s, axis=1, keepdims=True))
        alpha = jnp.exp(m_prev - m_new)
        p = jnp.exp(s - m_new)
        l_new = alpha * l_prev + jnp.sum(p, axis=1, keepdims=True)
        pv = []
        for h in range(H):
            o = None
            for i, value in enumerate(values):
                ph = p[h * tq:(h + 1) * tq, i * PAGE_SIZE:(i + 1) * PAGE_SIZE].astype(BF16)
                t = _dot(ph, value(h))
                o = t if o is None else o + t
            pv.append(o)
        return m_new, l_new, alpha * acc_prev + jnp.concatenate(pv, axis=0)

    @pl.when(j == 0)
    def _():
        carry_ref[...] = jnp.zeros_like(carry_ref)
        r = lax.broadcasted_iota(jnp.int32, (H * tq, PAGE_SIZE), 0) % tq
        c = lax.broadcasted_iota(jnp.int32, (H * tq, PAGE_SIZE), 1)
        s = jnp.concatenate(
            [_dot_nt(q[h], kn_ref[0, :, h * FOX_HD:(h + 1) * FOX_HD]) for h in range(H)], axis=0)
        s = s * scale + fq - per_head_rows(fkn_ref[0])
        s = jnp.where(c <= r, s, MASK_NEG)
        state = (jnp.full((H * tq, 1), MASK_NEG, F32), jnp.zeros((H * tq, 1), F32),
                 jnp.zeros((H * tq, FOX_HD), F32))
        m_ref[...], l_ref[...], acc_ref[...] = attend(
            s, [lambda h: vn_ref[0, :, h * FOX_HD:(h + 1) * FOX_HD]], state)

    r = lax.broadcasted_iota(jnp.int32, (PAGE_SIZE, PAGE_SIZE), 0)
    c = lax.broadcasted_iota(jnp.int32, (PAGE_SIZE, PAGE_SIZE), 1)
    after = (r > c).astype(BF16)
    carry = carry_ref[...]
    blocks = []
    for i in range(npg):
        lf = lfc[i][...]
        suffix = _dot_exact_rhs(lf, after) + carry
        carry = carry + jnp.sum(lf, axis=1, keepdims=True)
        kt = jnp.swapaxes(kc[i][...], 0, 1)
        sk = jnp.concatenate([_dot_nt(q[h], kt[h].astype(BF16)) for h in range(H)], axis=0)
        blocks.append(sk * scale + per_head_rows(suffix))
    carry_ref[...] = carry
    s = jnp.concatenate(blocks, axis=1) + fq
    vts = [jnp.swapaxes(vc[i][...], 0, 1) for i in range(npg)]
    values = [lambda h, i=i: vts[i][h].astype(BF16) for i in range(npg)]
    m_new, l_new, acc_new = attend(s, values, (m_ref[...], l_ref[...], acc_ref[...]))
    m_ref[...] = m_new
    l_ref[...] = l_new
    acc_ref[...] = acc_new

    @pl.when(j == pl.num_programs(1) - 1)
    def _():
        out = acc_new / l_new
        for h in range(H):
            o_ref[:, h * FOX_HD:(h + 1) * FOX_HD] = out[h * tq:(h + 1) * tq, :]


DECODE_PAGES_PER_STEP = 8


def _decode_attention(page_table, q, k_new, v_new, cum, cum_new_t, cache_k, cache_v, cache_lf_t, B, tq):
    n_pages = page_table.shape[1]
    npg = DECODE_PAGES_PER_STEP
    assert n_pages % npg == 0

    def page(i, nd):
        return lambda b, j, pt: (pt[b * n_pages + (n_pages - 1 - (j * npg + i))],) + (0,) * nd

    kv_specs = [pl.BlockSpec((None, PAGE_SIZE, FOX_HEADS, FOX_HD), page(i, 3)) for i in range(npg)]
    lf_specs = [pl.BlockSpec((None, FOX_HEADS, PAGE_SIZE), page(i, 2)) for i in range(npg)]
    return pl.pallas_call(
        functools.partial(_decode_kernel, tq=tq, npg=npg),
        grid_spec=pltpu.PrefetchScalarGridSpec(
            num_scalar_prefetch=1,
            grid=(B, n_pages // npg),
            in_specs=[
                pl.BlockSpec((tq, FOX_DIM), lambda b, j, pt: (b, 0)),
                pl.BlockSpec((1, PAGE_SIZE, FOX_DIM), lambda b, j, pt: (b, 0, 0)),
                pl.BlockSpec((1, PAGE_SIZE, FOX_DIM), lambda b, j, pt: (b, 0, 0)),
                pl.BlockSpec((tq, LANES), lambda b, j, pt: (b, 0)),
                pl.BlockSpec((1, FOX_HEADS, PAGE_SIZE), lambda b, j, pt: (b, 0, 0)),
            ] + kv_specs + kv_specs + lf_specs,
            out_specs=pl.BlockSpec((tq, FOX_DIM), lambda b, j, pt: (b, 0)),
            scratch_shapes=[
                pltpu.VMEM((FOX_HEADS * tq, 1), F32),
                pltpu.VMEM((FOX_HEADS * tq, 1), F32),
                pltpu.VMEM((FOX_HEADS * tq, FOX_HD), F32),
                pltpu.VMEM((FOX_HEADS, 1), F32),
            ],
        ),
        out_shape=jax.ShapeDtypeStruct((B * tq, FOX_DIM), F32),
        compiler_params=_cparams("parallel", "arbitrary"),
        name="decode_attention",
    )(page_table.reshape(-1), q, k_new, v_new, cum, cum_new_t, *([cache_k] * npg), *([cache_v] * npg),
      *([cache_lf_t] * npg))


def _moe(u, x, W, l, gate, T, tm, bm):
    N, D = u.shape
    idx_t, w_t, rank_t, cnt = _router(u, W["router_wt"][l], W["router_bias"][l].reshape(N_EXPERTS, 1), tm)
    counts = cnt[:, 0]
    padded = (counts + bm - 1) // bm * bm
    pad_end = jnp.cumsum(padded)
    pad_start = pad_end - padded
    dest = (pad_start[idx_t] + rank_t).T
    n_blocks = N * TOP_K // bm + N_EXPERTS
    R = n_blocks * bm
    starts = jnp.arange(n_blocks, dtype=jnp.int32) * bm
    block_e = jnp.minimum(jnp.searchsorted(pad_end, starts, side="right"), N_EXPERTS - 1).astype(jnp.int32)
    live = (starts < pad_end[-1]).astype(jnp.int32)
    tok = jnp.broadcast_to(jnp.arange(N, dtype=jnp.int32)[:, None], (N, TOP_K))
    row_tok = jnp.zeros((R,), jnp.int32).at[dest.reshape(-1)].set(tok.reshape(-1), unique_indices=True)
    xs = u[row_tok]
    yb = _experts(xs, block_e, live, W["exp_w_gate"], W["exp_w_up"], W["exp_w_down"], l, bm)
    routed = jnp.sum(yb[dest] * w_t.T[:, :, None], axis=1)
    return _moe_final(u, x, routed, W["sh_wgu"][l], W["sh_wd"][l], gate, W["ln_g"][l, 1], W["ln_b"][l, 1], T, tm)


def _trunk(x, mods, kv_mod, s0, past, W, B, T):
    N = B * T
    tm = min(512, N)
    bm = 256 if N * TOP_K >= 256 * N_EXPERTS else 128
    x = x.reshape(N, D_MODEL)
    new_gla = []
    for l in range(DEPTH):
        sh_a, sc_a, g_a, sh_m, sc_m, g_m = jnp.split(mods[l], 6, axis=-1)
        if l < N_A:
            p = _modproj(x, sc_a, sh_a, W["gla_w_in"][l], T, tm, F32)
            if T >= GLA_CHUNK:
                Tp, tc = T, min(512, T)
            else:
                Tp = tc = GLA_CHUNK
                p = jnp.pad(p.reshape(B, T, -1), ((0, 0), (0, Tp - T), (0, 0))).reshape(B * Tp, -1)
            o, s_T = _gla(p, W["gla_w_a2"][l], W["gla_b_a"][l], W["gla_norm_g"][l], s0[l], B, Tp, tc, T)
            if Tp != T:
                o = o.reshape(B, Tp, -1)[:, :T].reshape(N, -1)
            new_gla.append(s_T)
            w_o = W["gla_w_o"][l]
        else:
            q = _modproj(x, sc_a, sh_a, W["fox_w_q"][l - N_A], T, tm, BF16 if past is None else F32)
            if past is None:
                o = _fox_attention(q, kb, vb, cum, cum_t, B, T, min(256, T))
            else:
                o = _decode_attention(past["page_table"], q, kb_pad, vb_pad, cum, cum_t, past["k"], past["v"],
                                      past["lf_t"], B, T)
            w_o = W["fox_w_o"][l - N_A]
        x, u = _proj_res_ln(o, w_o, x, g_a, W["ln_g"][l, 0], W["ln_b"][l, 0], sc_m, sh_m, T, tm)
        x = _moe(u, x, W, l, g_m, T, tm, bm)
        if l == N_A - 1:
            sh_kv, sc_kv = jnp.split(kv_mod, 2, axis=-1)
            k_new, v_new, kb, vb, lf, cum = _kv_proj(x, sc_kv, sh_kv, W["kv_w"], W["fox_b_f"], T, tm)
            cum_h = cum[:, :FOX_HEADS].reshape(B, T, FOX_HEADS)
            if past is None:
                cum_t = cum_h.transpose(0, 2, 1).reshape(B * FOX_HEADS, T)
            else:
                rows = ((0, 0), (0, PAGE_SIZE - T), (0, 0))
                cum_t = jnp.pad(cum_h, rows).transpose(0, 2, 1)
                kb_pad = jnp.pad(kb.reshape(B, T, FOX_DIM), rows)
                vb_pad = jnp.pad(vb.reshape(B, T, FOX_DIM), rows)
    return (x.reshape(B, T, D_MODEL), jnp.stack(new_gla),
            k_new.reshape(B, T, FOX_HEADS, FOX_HD), v_new.reshape(B, T, FOX_HEADS, FOX_HD),
            lf[:, :FOX_HEADS].reshape(B, T, FOX_HEADS))


def kernel(x_prompt, x_sample, c_prompt, c_sample, state_gla, cache_k, cache_v, cache_logf, page_table, gla_w_in, gla_w_a2, gla_b_a, gla_norm_g, gla_w_o, kv_w_mod, kv_b_mod, kv_w, fox_b_f, fox_w_q, fox_w_o, ada_w, ada_b, ln_g, ln_b, router_w, router_bias, exp_w_gate, exp_w_up, exp_w_down, sh_w_gate, sh_w_up, sh_w_down):
    Bp, Tp, D = x_prompt.shape
    Bs, Ts, _ = x_sample.shape
    lane_pad = lambda w, n: jnp.pad(w, [(0, 0)] * (w.ndim - 1) + [(0, n - w.shape[-1])])
    W = dict(
        gla_w_in=jnp.concatenate(
            [gla_w_in[..., :GLA_IN_MAIN], lane_pad(gla_w_in[..., GLA_IN_MAIN:], LANES)], axis=-1).astype(BF16),
        gla_w_a2=jnp.pad(gla_w_a2, ((0, 0), (0, LANES - GLA_GATE_RANK), (0, 0))).astype(BF16),
        gla_b_a=gla_b_a.reshape(N_A, 1, GLA_DK),
        gla_norm_g=gla_norm_g.reshape(N_A, 1, GLA_DVH),
        gla_w_o=gla_w_o.astype(BF16),
        kv_w=jnp.concatenate([kv_w[:, :2 * FOX_DIM], lane_pad(kv_w[:, 2 * FOX_DIM:], LANES)], axis=-1).astype(BF16),
        fox_b_f=lane_pad(fox_b_f.reshape(1, FOX_HEADS), LANES),
        fox_w_q=fox_w_q.astype(BF16),
        fox_w_o=fox_w_o.astype(BF16),
        ln_g=ln_g, ln_b=ln_b,
        router_wt=jnp.swapaxes(router_w, 1, 2).astype(BF16),
        router_bias=router_bias,
        exp_w_gate=exp_w_gate, exp_w_up=exp_w_up, exp_w_down=exp_w_down,
        sh_wgu=jnp.concatenate([sh_w_gate, sh_w_up], axis=-1).astype(BF16),
        sh_wd=sh_w_down.astype(BF16),
    )
    c_all = jnp.concatenate([c_prompt, c_sample], axis=0)
    mods = _cond_linear(c_all, ada_w, ada_b, 1536)
    kv_mod = _cond_linear(c_all, kv_w_mod[None], kv_b_mod[None], 1024)[0]

    s0_prompt = jnp.zeros((N_A, Bp, GLA_HEADS, GLA_DKH, GLA_DVH), F32)
    out_p = _trunk(x_prompt, mods[:, :Bp], kv_mod[:Bp], s0_prompt, None, W, Bp, Tp)

    past = dict(page_table=page_table, k=cache_k, v=cache_v, lf_t=jnp.swapaxes(cache_logf, 1, 2))
    out_s = _trunk(x_sample, mods[:, Bp:], kv_mod[Bp:], state_gla, past, W, Bs, Ts)

    y_p, gla_p, k_p, v_p, lf_p = out_p
    y_s, gla_s, k_s, v_s, lf_s = out_s
    return (y_p, y_s, gla_p, gla_s, k_p, v_p, lf_p, k_s, v_s, lf_s)
```

```python
import functools

import jax
import jax.numpy as jnp
from jax import lax
from jax.experimental import pallas as pl
from jax.experimental.pallas import tpu as pltpu

F32 = jnp.float32
BF16 = jnp.bfloat16

D_MODEL = 1024
DEPTH = 4
N_A = DEPTH // 2
PAGE_SIZE = 128
GLA_HEADS = 4
GLA_DK = D_MODEL // 2
GLA_DV = D_MODEL
GLA_DKH = GLA_DK // GLA_HEADS
GLA_DVH = GLA_DV // GLA_HEADS
GLA_GATE_RANK = 16
GLA_GATE_TAU = 16.0
GLA_CHUNK = 64
FOX_HEADS = 8
FOX_HD = D_MODEL // FOX_HEADS
FOX_DIM = FOX_HEADS * FOX_HD
N_EXPERTS = 64
TOP_K = 8
N_GROUPS = 8
GROUP_SIZE = N_EXPERTS // N_GROUPS
TOPK_GROUPS = 4
D_EXPERT = 256
D_SHARED = 256
ROUTED_SCALE = 2.5
DEEPNORM_ALPHA = (2 * DEPTH) ** 0.25
LN_EPS = 1e-5
RMS_EPS = 1e-6

LANES = 128
GLA_IN_MAIN = 2 * GLA_DK + 2 * GLA_DV
GLA_IN_PAD = GLA_IN_MAIN + LANES
KV_PAD = 2 * FOX_DIM + LANES
MASK_NEG = -1e30
VMEM_LIMIT = 52 * 1024 * 1024


def _cparams(*sem):
    return pltpu.CompilerParams(dimension_semantics=sem, vmem_limit_bytes=VMEM_LIMIT)


def _dot(a, b):
    return jnp.dot(a, b, preferred_element_type=F32)


def _dot_nt(a, b):
    return lax.dot_general(a, b, (((1,), (1,)), ((), ())), preferred_element_type=F32)


def _dot_tn(a, b):
    return lax.dot_general(a, b, (((0,), (0,)), ((), ())), preferred_element_type=F32)


def _sigmoid(x):
    return 1.0 / (1.0 + jnp.exp(-x))


def _silu(x):
    return x * _sigmoid(x)


def _log_sigmoid(x):
    return jnp.minimum(x, 0.0) - jnp.log1p(jnp.exp(-jnp.abs(x)))


def _split3(x):
    hi = x.astype(BF16)
    r = x - hi.astype(F32)
    mid = r.astype(BF16)
    lo = (r - mid.astype(F32)).astype(BF16)
    return hi, mid, lo


def _dot_exact_lhs(mat01, x):
    hi, mid, lo = _split3(x)
    return _dot(mat01, hi) + _dot(mat01, mid) + _dot(mat01, lo)


def _dot_exact_rhs(x, mat01):
    hi, mid, lo = _split3(x)
    return _dot(hi, mat01) + _dot(mid, mat01) + _dot(lo, mat01)


def _layer_norm(h, g, b):
    mu = jnp.mean(h, axis=-1, keepdims=True)
    xc = h - mu
    var = jnp.mean(xc * xc, axis=-1, keepdims=True)
    return xc * lax.rsqrt(var + LN_EPS) * g + b


def _row_mod(m, T, tm):
    B, D = m.shape
    if T % tm == 0:
        tiles_per_batch = T // tm
        return m.reshape(B, 1, D), pl.BlockSpec((1, 1, D), lambda i: (i // tiles_per_batch, 0, 0))
    assert tm % T == 0
    e = jnp.repeat(m, T, axis=0).reshape(B * T // tm, tm, D)
    return e, pl.BlockSpec((1, tm, D), lambda i: (i, 0, 0))


def _cond_kernel(c_ref, w_ref, b_ref, o_ref):
    s = _silu(c_ref[...]).astype(BF16)
    o_ref[0] = _dot(s, w_ref[0].astype(BF16)) + b_ref[0]


def _cond_linear(c, w, b, tn):
    M, D = c.shape
    L, _, N = w.shape
    return pl.pallas_call(
        _cond_kernel,
        grid=(L, N // tn),
        in_specs=[
            pl.BlockSpec((M, D), lambda l, j: (0, 0)),
            pl.BlockSpec((1, D, tn), lambda l, j: (l, 0, j)),
            pl.BlockSpec((1, 1, tn), lambda l, j: (l, 0, j)),
        ],
        out_specs=pl.BlockSpec((1, M, tn), lambda l, j: (l, 0, j)),
        out_shape=jax.ShapeDtypeStruct((L, M, N), F32),
        compiler_params=_cparams("parallel", "parallel"),
        name="cond_linear",
    )(c, w, b.reshape(L, 1, N))


def _modproj_kernel(x_ref, sc_ref, sh_ref, w_ref, o_ref):
    u = (x_ref[...] * (1.0 + sc_ref[0]) + sh_ref[0]).astype(BF16)
    o_ref[...] = _dot(u, w_ref[...]).astype(o_ref.dtype)


def _modproj(x, sc, sh, w, T, tm, out_dtype):
    N, D = x.shape
    Dout = w.shape[1]
    sc3, sc_spec = _row_mod(sc, T, tm)
    sh3, sh_spec = _row_mod(sh, T, tm)
    return pl.pallas_call(
        _modproj_kernel,
        grid=(N // tm,),
        in_specs=[
            pl.BlockSpec((tm, D), lambda i: (i, 0)),
            sc_spec,
            sh_spec,
            pl.BlockSpec((D, Dout), lambda i: (0, 0)),
        ],
        out_specs=pl.BlockSpec((tm, Dout), lambda i: (i, 0)),
        out_shape=jax.ShapeDtypeStruct((N, Dout), out_dtype),
        compiler_params=_cparams("parallel"),
        name="modproj",
    )(x, sc3, sh3, w)


def _kv_kernel(x_ref, sc_ref, sh_ref, w_ref, bf_ref, k_ref, v_ref, kb_ref, vb_ref, lf_ref, cum_ref,
               carry_ref, *, T, tm):
    u = (x_ref[...] * (1.0 + sc_ref[0]) + sh_ref[0]).astype(BF16)
    p = _dot(u, w_ref[...])
    k = p[:, :FOX_DIM]
    v = p[:, FOX_DIM:2 * FOX_DIM]
    k_ref[...] = k
    v_ref[...] = v
    kb_ref[...] = k.astype(BF16)
    vb_ref[...] = v.astype(BF16)
    lf = _log_sigmoid(p[:, 2 * FOX_DIM:] + bf_ref[...])
    lf_ref[...] = lf
    if T % tm == 0:
        @pl.when(pl.program_id(0) % (T // tm) == 0)
        def _():
            carry_ref[...] = jnp.zeros_like(carry_ref)

        sb = LANES
        r = lax.broadcasted_iota(jnp.int32, (sb, sb), 0)
        c = lax.broadcasted_iota(jnp.int32, (sb, sb), 1)
        tri = (c <= r).astype(BF16)
        carry = carry_ref[...]
        for s in range(tm // sb):
            cs = _dot_exact_lhs(tri, lf[s * sb:(s + 1) * sb]) + carry
            cum_ref[s * sb:(s + 1) * sb, :] = cs
            carry = cs[sb - 1:sb, :]
        carry_ref[...] = carry
    else:
        r = lax.broadcasted_iota(jnp.int32, (tm, tm), 0)
        c = lax.broadcasted_iota(jnp.int32, (tm, tm), 1)
        tri = ((c <= r) & ((r // T) == (c // T))).astype(BF16)
        cum_ref[...] = _dot_exact_lhs(tri, lf)


def _kv_proj(x, sc, sh, w, bf, T, tm):
    N, D = x.shape
    sc3, sc_spec = _row_mod(sc, T, tm)
    sh3, sh_spec = _row_mod(sh, T, tm)
    row = lambda i: (i, 0)
    return pl.pallas_call(
        functools.partial(_kv_kernel, T=T, tm=tm),
        grid=(N // tm,),
        in_specs=[
            pl.BlockSpec((tm, D), row),
            sc_spec,
            sh_spec,
            pl.BlockSpec((D, KV_PAD), lambda i: (0, 0)),
            pl.BlockSpec((1, LANES), lambda i: (0, 0)),
        ],
        out_specs=[
            pl.BlockSpec((tm, FOX_DIM), row),
            pl.BlockSpec((tm, FOX_DIM), row),
            pl.BlockSpec((tm, FOX_DIM), row),
            pl.BlockSpec((tm, FOX_DIM), row),
            pl.BlockSpec((tm, LANES), row),
            pl.BlockSpec((tm, LANES), row),
        ],
        out_shape=[
            jax.ShapeDtypeStruct((N, FOX_DIM), F32),
            jax.ShapeDtypeStruct((N, FOX_DIM), F32),
            jax.ShapeDtypeStruct((N, FOX_DIM), BF16),
            jax.ShapeDtypeStruct((N, FOX_DIM), BF16),
            jax.ShapeDtypeStruct((N, LANES), F32),
            jax.ShapeDtypeStruct((N, LANES), F32),
        ],
        scratch_shapes=[pltpu.VMEM((1, LANES), F32)],
        compiler_params=_cparams("arbitrary"),
        name="kv_proj",
    )(x, sc3, sh3, w, bf)


def _gla_kernel(q_ref, k_ref, v_ref, g_ref, a_ref, wa_ref, ba_ref, ng_ref, s0_ref, o_ref, st_ref,
                state_ref, *, c, nsub, t_valid):
    t = pl.program_id(2)

    @pl.when(t == 0)
    def _():
        state_ref[...] = s0_ref[0, 0].T

    r = lax.broadcasted_iota(jnp.int32, (c, c), 0)
    cc = lax.broadcasted_iota(jnp.int32, (c, c), 1)
    causal = cc <= r
    tri = causal.astype(BF16)
    scale = GLA_DKH ** -0.5
    for i in range(nsub):
        sl = slice(i * c, (i + 1) * c)
        z = _dot(a_ref[sl, :].astype(BF16), wa_ref[...]) + ba_ref[...]
        la = _log_sigmoid(z) / GLA_GATE_TAU
        if t_valid < c:
            la = jnp.where(lax.broadcasted_iota(jnp.int32, la.shape, 0) < t_valid, la, 0.0)
        bc = _dot_exact_lhs(tri, la)
        b_last = bc[c - 1:c, :]
        k = k_ref[sl, :]
        v = v_ref[sl, :].astype(BF16)
        q_dec = ((q_ref[sl, :] * scale) * jnp.exp(bc)).astype(BF16)
        k_dec = (k * jnp.exp(-bc)).astype(BF16)
        k_end = (k * jnp.exp(b_last - bc)).astype(BF16)
        st = state_ref[...]
        inter = _dot_nt(q_dec, st.astype(BF16))
        att = jnp.where(causal, _dot_nt(q_dec, k_dec), 0.0).astype(BF16)
        o = inter + _dot(att, v)
        state_ref[...] = st * jnp.exp(b_last) + _dot_tn(v, k_end)
        o = o * lax.rsqrt(jnp.mean(o * o, axis=-1, keepdims=True) + RMS_EPS) * ng_ref[...]
        o_ref[sl, :] = (o * _silu(g_ref[sl, :])).astype(o_ref.dtype)

    @pl.when(t == pl.num_programs(2) - 1)
    def _():
        st_ref[0, 0] = state_ref[...].T


def _gla(p, wa, ba, ng, s0, B, Tp, tc, t_valid):
    nt = Tp // tc
    c = GLA_CHUNK
    row = lambda b, h, t: b * nt + t
    kq = GLA_DK // GLA_DKH
    kv = 2 * GLA_DK // GLA_DVH
    kg = kv + GLA_HEADS
    ka = GLA_IN_MAIN // LANES
    return pl.pallas_call(
        functools.partial(_gla_kernel, c=c, nsub=tc // c, t_valid=t_valid),
        grid=(B, GLA_HEADS, nt),
        in_specs=[
            pl.BlockSpec((tc, GLA_DKH), lambda b, h, t: (row(b, h, t), h)),
            pl.BlockSpec((tc, GLA_DKH), lambda b, h, t: (row(b, h, t), kq + h)),
            pl.BlockSpec((tc, GLA_DVH), lambda b, h, t: (row(b, h, t), kv + h)),
            pl.BlockSpec((tc, GLA_DVH), lambda b, h, t: (row(b, h, t), kg + h)),
            pl.BlockSpec((tc, LANES), lambda b, h, t: (row(b, h, t), ka)),
            pl.BlockSpec((LANES, GLA_DKH), lambda b, h, t: (0, h)),
            pl.BlockSpec((1, GLA_DKH), lambda b, h, t: (0, h)),
            pl.BlockSpec((1, GLA_DVH), lambda b, h, t: (0, 0)),
            pl.BlockSpec((1, 1, GLA_DKH, GLA_DVH), lambda b, h, t: (b, h, 0, 0)),
        ],
        out_specs=[
            pl.BlockSpec((tc, GLA_DVH), lambda b, h, t: (row(b, h, t), h)),
            pl.BlockSpec((1, 1, GLA_DKH, GLA_DVH), lambda b, h, t: (b, h, 0, 0)),
        ],
        out_shape=[
            jax.ShapeDtypeStruct((B * Tp, GLA_DV), BF16),
            jax.ShapeDtypeStruct((B, GLA_HEADS, GLA_DKH, GLA_DVH), F32),
        ],
        scratch_shapes=[pltpu.VMEM((GLA_DVH, GLA_DKH), F32)],
        compiler_params=_cparams("parallel", "parallel", "arbitrary"),
        name="gla",
    )(p, p, p, p, p, wa, ba, ng, s0)


def _to_row_tiles(x):
    parts = [x[:, c * LANES:(c + 1) * LANES] for c in range(x.shape[1] // LANES)]
    return jnp.swapaxes(jnp.stack(parts, axis=0), 0, 1)


def _from_row_tiles(x3):
    xt = jnp.swapaxes(x3, 0, 1)
    return jnp.concatenate([xt[c] for c in range(xt.shape[0])], axis=1)


def _proj_res_ln_kernel(a_ref, w_ref, x_ref, gate_ref, lg_ref, lb_ref, sc_ref, sh_ref, xo_ref, uo_ref, u3_ref):
    y = _dot(a_ref[...].astype(BF16), w_ref[...])
    h = DEEPNORM_ALPHA * x_ref[...] + (1.0 + gate_ref[0]) * y
    xn = _layer_norm(h, lg_ref[...], lb_ref[...])
    xo_ref[...] = xn
    u = xn * (1.0 + sc_ref[0]) + sh_ref[0]
    uo_ref[...] = u.astype(uo_ref.dtype)
    u3_ref[...] = _to_row_tiles(u)


def _proj_res_ln(a, w, x, gate, lg, lb, sc, sh, T, tm):
    N, D = x.shape
    g3, g_spec = _row_mod(gate, T, tm)
    sc3, sc_spec = _row_mod(sc, T, tm)
    sh3, sh_spec = _row_mod(sh, T, tm)
    row = lambda i: (i, 0)
    fixed = lambda i: (0, 0)
    return pl.pallas_call(
        _proj_res_ln_kernel,
        grid=(N // tm,),
        in_specs=[
            pl.BlockSpec((tm, a.shape[1]), row),
            pl.BlockSpec(w.shape, fixed),
            pl.BlockSpec((tm, D), row),
            g_spec,
            pl.BlockSpec((1, D), fixed),
            pl.BlockSpec((1, D), fixed),
            sc_spec,
            sh_spec,
        ],
        out_specs=[pl.BlockSpec((tm, D), row), pl.BlockSpec((tm, D), row),
                   pl.BlockSpec((tm, D // LANES, LANES), lambda i: (i, 0, 0))],
        out_shape=[jax.ShapeDtypeStruct((N, D), F32), jax.ShapeDtypeStruct((N, D), BF16),
                   jax.ShapeDtypeStruct((N, D // LANES, LANES), F32)],
        compiler_params=_cparams("parallel"),
        name="proj_res_ln",
    )(a, w, x, g3, lg.reshape(1, D), lb.reshape(1, D), sc3, sh3)


def _router_kernel(u_ref, wr_ref, rb_ref, idx_ref, w_ref, rank_ref, cnt_ref, carry_ref, *, tm):
    @pl.when(pl.program_id(0) == 0)
    def _():
        carry_ref[...] = jnp.zeros_like(carry_ref)

    s = _sigmoid(_dot_nt(wr_ref[...], u_ref[...]))
    sb = s + rb_ref[...]
    neg = -jnp.inf
    sub = lax.broadcasted_iota(jnp.int32, (GROUP_SIZE, tm), 0)
    gs = GROUP_SIZE
    s_g = [s[g * gs:(g + 1) * gs] for g in range(N_GROUPS)]
    sb_g = [sb[g * gs:(g + 1) * gs] for g in range(N_GROUPS)]
    eid_g = [sub + g * gs for g in range(N_GROUPS)]

    score = []
    for g in range(N_GROUPS):
        m1 = jnp.max(sb_g[g], axis=0, keepdims=True)
        first = jnp.min(jnp.where(sb_g[g] == m1, sub, gs), axis=0, keepdims=True)
        m2 = jnp.max(jnp.where(sub == first, neg, sb_g[g]), axis=0, keepdims=True)
        score.append(m1 + m2)

    chosen = [jnp.zeros((1, tm), jnp.int32) for _ in range(N_GROUPS)]
    for _ in range(TOPK_GROUPS):
        m = score[0]
        for g in range(1, N_GROUPS):
            m = jnp.maximum(m, score[g])
        gi = jnp.full((1, tm), N_GROUPS, jnp.int32)
        for g in range(N_GROUPS - 1, -1, -1):
            gi = jnp.where(score[g] == m, g, gi)
        for g in range(N_GROUPS):
            hit = gi == g
            chosen[g] = jnp.where(hit, 1, chosen[g])
            score[g] = jnp.where(hit, neg, score[g])
    cand = [jnp.where(chosen[g] > 0, sb_g[g], neg) for g in range(N_GROUPS)]

    onehot = [jnp.zeros((gs, tm), F32) for _ in range(N_GROUPS)]
    idxs, ws = [], []
    for _ in range(TOP_K):
        mx = cand[0]
        for g in range(1, N_GROUPS):
            mx = jnp.maximum(mx, cand[g])
        m = jnp.max(mx, axis=0, keepdims=True)
        lo = jnp.where(cand[0] == m, eid_g[0], N_EXPERTS)
        for g in range(1, N_GROUPS):
            lo = jnp.minimum(lo, jnp.where(cand[g] == m, eid_g[g], N_EXPERTS))
        idx = jnp.min(lo, axis=0, keepdims=True)
        wsum = jnp.zeros((gs, tm), F32)
        for g in range(N_GROUPS):
            hit = eid_g[g] == idx
            wsum = wsum + jnp.where(hit, s_g[g], 0.0)
            cand[g] = jnp.where(hit, neg, cand[g])
            onehot[g] = jnp.where(hit, 1.0, onehot[g])
        idxs.append(idx)
        ws.append(jnp.sum(wsum, axis=0, keepdims=True))
    total = ws[0]
    for k in range(1, TOP_K):
        total = total + ws[k]
    for k in range(TOP_K):
        idx_ref[k:k + 1, :] = idxs[k]
        w_ref[k:k + 1, :] = ws[k] / total * ROUTED_SCALE

    oh = jnp.concatenate(onehot, axis=0)
    r = lax.broadcasted_iota(jnp.int32, (tm, tm), 0)
    c = lax.broadcasted_iota(jnp.int32, (tm, tm), 1)
    before = (r < c).astype(BF16)
    pre = _dot(oh.astype(BF16), before) + carry_ref[...]
    for k in range(TOP_K):
        acc = jnp.zeros((gs, tm), F32)
        for g in range(N_GROUPS):
            acc = acc + jnp.where(eid_g[g] == idxs[k], pre[g * gs:(g + 1) * gs], 0.0)
        rank_ref[k:k + 1, :] = jnp.sum(acc, axis=0, keepdims=True).astype(jnp.int32)
    carry_ref[...] = carry_ref[...] + jnp.sum(oh, axis=1, keepdims=True)
    cnt_ref[...] = jnp.broadcast_to(carry_ref[...], cnt_ref.shape).astype(jnp.int32)


def _router(u, wr_t, rb, tm):
    N, D = u.shape
    col = lambda i: (0, i)
    fixed = lambda i: (0, 0)
    return pl.pallas_call(
        functools.partial(_router_kernel, tm=tm),
        grid=(N // tm,),
        in_specs=[
            pl.BlockSpec((tm, D), lambda i: (i, 0)),
            pl.BlockSpec((N_EXPERTS, D), fixed),
            pl.BlockSpec((N_EXPERTS, 1), fixed),
        ],
        out_specs=[
            pl.BlockSpec((TOP_K, tm), col),
            pl.BlockSpec((TOP_K, tm), col),
            pl.BlockSpec((TOP_K, tm), col),
            pl.BlockSpec((N_EXPERTS, LANES), fixed),
        ],
        out_shape=[
            jax.ShapeDtypeStruct((TOP_K, N), jnp.int32),
            jax.ShapeDtypeStruct((TOP_K, N), F32),
            jax.ShapeDtypeStruct((TOP_K, N), jnp.int32),
            jax.ShapeDtypeStruct((N_EXPERTS, LANES), jnp.int32),
        ],
        scratch_shapes=[pltpu.VMEM((N_EXPERTS, 1), F32)],
        compiler_params=_cparams("arbitrary"),
        name="router",
    )(u, wr_t, rb)


def _experts_kernel(be_ref, live_ref, x_ref, wg_ref, wu_ref, wd_ref, o_ref, wgb_ref, wub_ref, wdb_ref):
    i = pl.program_id(0)
    e = be_ref[i]
    prev = be_ref[jnp.maximum(i - 1, 0)]

    @pl.when((i == 0) | (e != prev))
    def _():
        wgb_ref[...] = wg_ref[...].astype(BF16)
        wub_ref[...] = wu_ref[...].astype(BF16)
        wdb_ref[...] = wd_ref[...].astype(BF16)

    @pl.when(live_ref[i] > 0)
    def _():
        x = _from_row_tiles(x_ref[...]).astype(BF16)
        h = (_silu(_dot(x, wgb_ref[...])) * _dot(x, wub_ref[...])).astype(BF16)
        o_ref[...] = _to_row_tiles(_dot(h, wdb_ref[...]))

    @pl.when(live_ref[i] == 0)
    def _():
        o_ref[...] = jnp.zeros_like(o_ref)


def _experts(xs3, block_e, live, wg, wu, wd, layer, bm):
    R, S, _ = xs3.shape
    D = S * LANES
    return pl.pallas_call(
        _experts_kernel,
        grid_spec=pltpu.PrefetchScalarGridSpec(
            num_scalar_prefetch=2,
            grid=(R // bm,),
            in_specs=[
                pl.BlockSpec((bm, S, LANES), lambda i, be, lv: (i, 0, 0)),
                pl.BlockSpec((None, None, D, D_EXPERT), lambda i, be, lv: (layer, be[i], 0, 0)),
                pl.BlockSpec((None, None, D, D_EXPERT), lambda i, be, lv: (layer, be[i], 0, 0)),
                pl.BlockSpec((None, None, D_EXPERT, D), lambda i, be, lv: (layer, be[i], 0, 0)),
            ],
            out_specs=pl.BlockSpec((bm, S, LANES), lambda i, be, lv: (i, 0, 0)),
            scratch_shapes=[
                pltpu.VMEM((D, D_EXPERT), BF16),
                pltpu.VMEM((D, D_EXPERT), BF16),
                pltpu.VMEM((D_EXPERT, D), BF16),
            ],
        ),
        out_shape=jax.ShapeDtypeStruct((R, S, LANES), F32),
        compiler_params=_cparams("arbitrary"),
        name="experts",
    )(block_e, live, xs3, wg, wu, wd)


ISSUE_TOKENS = 2


def _dispatch_kernel(dest_ref, u_hbm, xs_in, xs_hbm, sem, *, tmd):
    del xs_in
    base = pl.program_id(0) * tmd

    def issue(t2, carry):
        first = t2 * (ISSUE_TOKENS * TOP_K)
        rows = [dest_ref[first + r] for r in range(ISSUE_TOKENS * TOP_K)]
        for r, row in enumerate(rows):
            pltpu.make_async_copy(u_hbm.at[base + t2 * ISSUE_TOKENS + r // TOP_K], xs_hbm.at[row], sem).start()
        return carry

    lax.fori_loop(0, tmd // ISSUE_TOKENS, issue, 0)
    pltpu.make_async_copy(u_hbm.at[pl.ds(0, tmd * TOP_K)], xs_hbm.at[pl.ds(0, tmd * TOP_K)], sem).wait()


def _dispatch(u3, dest, R, tmd):
    N, S, _ = u3.shape
    return pl.pallas_call(
        functools.partial(_dispatch_kernel, tmd=tmd),
        grid=(N // tmd,),
        in_specs=[
            pl.BlockSpec((tmd * TOP_K,), lambda i: (i,), memory_space=pltpu.SMEM),
            pl.BlockSpec(memory_space=pl.ANY),
            pl.BlockSpec(memory_space=pl.ANY),
        ],
        out_specs=pl.BlockSpec(memory_space=pl.ANY),
        out_shape=jax.ShapeDtypeStruct((R, S, LANES), F32),
        scratch_shapes=[pltpu.SemaphoreType.DMA(())],
        input_output_aliases={2: 0},
        compiler_params=_cparams("arbitrary"),
        name="moe_dispatch",
    )(dest, u3, jnp.zeros((R, S, LANES), F32))


def _combine_kernel(dest_ref, w_ref, y_hbm, o_ref, buf_ref, sem, *, tmc):
    def issue(t2, carry):
        first = t2 * (ISSUE_TOKENS * TOP_K)
        rows = [dest_ref[first + r] for r in range(ISSUE_TOKENS * TOP_K)]
        for r, row in enumerate(rows):
            pltpu.make_async_copy(y_hbm.at[row], buf_ref.at[first + r], sem).start()
        return carry

    lax.fori_loop(0, tmc // ISSUE_TOKENS, issue, 0)
    pltpu.make_async_copy(y_hbm.at[pl.ds(0, tmc * TOP_K)], buf_ref, sem).wait()

    def reduce(t, carry):
        acc = w_ref[t * TOP_K] * buf_ref[t * TOP_K]
        for k in range(1, TOP_K):
            acc = acc + w_ref[t * TOP_K + k] * buf_ref[t * TOP_K + k]
        o_ref[t] = acc
        return carry

    lax.fori_loop(0, tmc, reduce, 0)


def _combine(y3, dest, w, N, tmc):
    _, S, _ = y3.shape
    return pl.pallas_call(
        functools.partial(_combine_kernel, tmc=tmc),
        grid=(N // tmc,),
        in_specs=[
            pl.BlockSpec((tmc * TOP_K,), lambda i: (i,), memory_space=pltpu.SMEM),
            pl.BlockSpec((tmc * TOP_K,), lambda i: (i,), memory_space=pltpu.SMEM),
            pl.BlockSpec(memory_space=pl.ANY),
        ],
        out_specs=pl.BlockSpec((tmc, S, LANES), lambda i: (i, 0, 0)),
        out_shape=jax.ShapeDtypeStruct((N, S, LANES), F32),
        scratch_shapes=[pltpu.VMEM((tmc * TOP_K, S, LANES), F32), pltpu.SemaphoreType.DMA(())],
        compiler_params=_cparams("arbitrary"),
        name="moe_combine",
    )(dest, w, y3)


def _moe_final_kernel(u_ref, x_ref, r_ref, wgu_ref, wd_ref, gate_ref, lg_ref, lb_ref, xo_ref):
    gu = _dot(u_ref[...], wgu_ref[...])
    h = (_silu(gu[:, :D_SHARED]) * gu[:, D_SHARED:]).astype(BF16)
    y = _from_row_tiles(r_ref[...]) + _dot(h, wd_ref[...])
    hres = DEEPNORM_ALPHA * x_ref[...] + (1.0 + gate_ref[0]) * y
    xo_ref[...] = _layer_norm(hres, lg_ref[...], lb_ref[...])


def _moe_final(u, x, routed, wgu, wd, gate, lg, lb, T, tm):
    N, D = x.shape
    g3, g_spec = _row_mod(gate, T, tm)
    row = lambda i: (i, 0)
    fixed = lambda i: (0, 0)
    return pl.pallas_call(
        _moe_final_kernel,
        grid=(N // tm,),
        in_specs=[
            pl.BlockSpec((tm, D), row),
            pl.BlockSpec((tm, D), row),
            pl.BlockSpec((tm, D // LANES, LANES), lambda i: (i, 0, 0)),
            pl.BlockSpec(wgu.shape, fixed),
            pl.BlockSpec(wd.shape, fixed),
            g_spec,
            pl.BlockSpec((1, D), fixed),
            pl.BlockSpec((1, D), fixed),
        ],
        out_specs=pl.BlockSpec((tm, D), row),
        out_shape=jax.ShapeDtypeStruct((N, D), F32),
        compiler_params=_cparams("parallel"),
        name="moe_final",
    )(u, x, routed, wgu, wd, g3, lg.reshape(1, D), lb.reshape(1, D))


def _fox_kernel(q_ref, k_ref, v_ref, fq_ref, fk_ref, o_ref, m_ref, l_ref, acc_ref, *, bq):
    i = pl.program_id(1)
    j = pl.program_id(2)

    @pl.when(j == 0)
    def _():
        m_ref[...] = jnp.full_like(m_ref, MASK_NEG)
        l_ref[...] = jnp.zeros_like(l_ref)
        acc_ref[...] = jnp.zeros_like(acc_ref)

    @pl.when(j <= i)
    def _():
        r = i * bq + lax.broadcasted_iota(jnp.int32, (bq, bq), 0)
        c = j * bq + lax.broadcasted_iota(jnp.int32, (bq, bq), 1)
        visible = c <= r
        scale = FOX_HD ** -0.5
        for h in range(FOX_HEADS):
            hs = slice(h * FOX_HD, (h + 1) * FOX_HD)
            s = _dot_nt(q_ref[:, hs], k_ref[:, hs]) * scale + fq_ref[:, h:h + 1] - fk_ref[h:h + 1, :]
            s = jnp.where(visible, s, MASK_NEG)
            m_prev = m_ref[h]
            m_new = jnp.maximum(m_prev, jnp.max(s, axis=1, keepdims=True))
            alpha = jnp.exp(m_prev - m_new)
            p = jnp.exp(s - m_new)
            l_ref[h] = alpha * l_ref[h] + jnp.sum(p, axis=1, keepdims=True)
            acc_ref[:, hs] = alpha * acc_ref[:, hs] + _dot(p.astype(BF16), v_ref[:, hs])
            m_ref[h] = m_new

    @pl.when(j == pl.num_programs(2) - 1)
    def _():
        for h in range(FOX_HEADS):
            hs = slice(h * FOX_HD, (h + 1) * FOX_HD)
            o_ref[:, hs] = (acc_ref[:, hs] / l_ref[h]).astype(o_ref.dtype)


def _fox_attention(q, kb, vb, cum, cum_t, B, T, bq):
    nq = T // bq
    kv_map = lambda b, i, j: (b * nq + jnp.minimum(j, i), 0)
    return pl.pallas_call(
        functools.partial(_fox_kernel, bq=bq),
        grid=(B, nq, nq),
        in_specs=[
            pl.BlockSpec((bq, FOX_DIM), lambda b, i, j: (b * nq + i, 0)),
            pl.BlockSpec((bq, FOX_DIM), kv_map),
            pl.BlockSpec((bq, FOX_DIM), kv_map),
            pl.BlockSpec((bq, LANES), lambda b, i, j: (b * nq + i, 0)),
            pl.BlockSpec((FOX_HEADS, bq), lambda b, i, j: (b, jnp.minimum(j, i))),
        ],
        out_specs=pl.BlockSpec((bq, FOX_DIM), lambda b, i, j: (b * nq + i, 0)),
        out_shape=jax.ShapeDtypeStruct((B * T, FOX_DIM), BF16),
        scratch_shapes=[
            pltpu.VMEM((FOX_HEADS, bq, 1), F32),
            pltpu.VMEM((FOX_HEADS, bq, 1), F32),
            pltpu.VMEM((bq, FOX_DIM), F32),
        ],
        compiler_params=_cparams("parallel", "parallel", "arbitrary"),
        name="fox_attention",
    )(q, kb, vb, cum, cum_t)


def _decode_kernel(pt_ref, q_ref, kn_ref, vn_ref, fq_ref, fkn_ref, *rest, tq, npg):
    kc, vc, lfc = rest[:npg], rest[npg:2 * npg], rest[2 * npg:3 * npg]
    o_ref, m_ref, l_ref, acc_ref, carry_ref = rest[3 * npg:]
    j = pl.program_id(1)
    H = FOX_HEADS
    scale = FOX_HD ** -0.5
    q = [q_ref[:, h * FOX_HD:(h + 1) * FOX_HD].astype(BF16) for h in range(H)]
    fq = jnp.concatenate([fq_ref[:, h:h + 1] for h in range(H)], axis=0)

    def per_head_rows(x):
        return jnp.concatenate([jnp.broadcast_to(x[h:h + 1, :], (tq, x.shape[1])) for h in range(H)], axis=0)

    def attend(s, values, state):
        m_prev, l_prev, acc_prev = state
        m_new = jnp.maximum(m_prev, jnp.max(s, axis=1, keepdims=True))
        alpha = jnp.exp(m_prev - m_new)
        p = jnp.exp(s - m_new)
        l_new = alpha * l_prev + jnp.sum(p, axis=1, keepdims=True)
        pv = []
        for h in range(H):
            o = None
            for i, value in enumerate(values):
                ph = p[h * tq:(h + 1) * tq, i * PAGE_SIZE:(i + 1) * PAGE_SIZE].astype(BF16)
                t = _dot(ph, value(h))
                o = t if o is None else o + t
            pv.append(o)
        return m_new, l_new, alpha * acc_prev + jnp.concatenate(pv, axis=0)

    @pl.when(j == 0)
    def _():
        carry_ref[...] = jnp.zeros_like(carry_ref)
        r = lax.broadcasted_iota(jnp.int32, (H * tq, PAGE_SIZE), 0) % tq
        c = lax.broadcasted_iota(jnp.int32, (H * tq, PAGE_SIZE), 1)
        s = jnp.concatenate(
            [_dot_nt(q[h], kn_ref[0, :, h * FOX_HD:(h + 1) * FOX_HD]) for h in range(H)], axis=0)
        s = s * scale + fq - per_head_rows(fkn_ref[0])
        s = jnp.where(c <= r, s, MASK_NEG)
        state = (jnp.full((H * tq, 1), MASK_NEG, F32), jnp.zeros((H * tq, 1), F32),
                 jnp.zeros((H * tq, FOX_HD), F32))
        m_ref[...], l_ref[...], acc_ref[...] = attend(
            s, [lambda h: vn_ref[0, :, h * FOX_HD:(h + 1) * FOX_HD]], state)

    r = lax.broadcasted_iota(jnp.int32, (PAGE_SIZE, PAGE_SIZE), 0)
    c = lax.broadcasted_iota(jnp.int32, (PAGE_SIZE, PAGE_SIZE), 1)
    after = (r > c).astype(BF16)
    carry = carry_ref[...]
    blocks = []
    for i in range(npg):
        lf = lfc[i][...]
        suffix = _dot_exact_rhs(lf, after) + carry
        carry = carry + jnp.sum(lf, axis=1, keepdims=True)
        kt = jnp.swapaxes(kc[i][...], 0, 1)
        sk = jnp.concatenate([_dot_nt(q[h], kt[h].astype(BF16)) for h in range(H)], axis=0)
        blocks.append(sk * scale + per_head_rows(suffix))
    carry_ref[...] = carry
    s = jnp.concatenate(blocks, axis=1) + fq
    vts = [jnp.swapaxes(vc[i][...], 0, 1) for i in range(npg)]
    values = [lambda h, i=i: vts[i][h].astype(BF16) for i in range(npg)]
    m_new, l_new, acc_new = attend(s, values, (m_ref[...], l_ref[...], acc_ref[...]))
    m_ref[...] = m_new
    l_ref[...] = l_new
    acc_ref[...] = acc_new

    @pl.when(j == pl.num_programs(1) - 1)
    def _():
        out = acc_new / l_new
        for h in range(H):
            o_ref[:, h * FOX_HD:(h + 1) * FOX_HD] = out[h * tq:(h + 1) * tq, :]


DECODE_PAGES_PER_STEP = 8


def _decode_attention(page_table, q, k_new, v_new, cum, cum_new_t, cache_k, cache_v, cache_lf_t, B, tq):
    n_pages = page_table.shape[1]
    npg = DECODE_PAGES_PER_STEP
    assert n_pages % npg == 0

    def page(i, nd):
        return lambda b, j, pt: (pt[b * n_pages + (n_pages - 1 - (j * npg + i))],) + (0,) * nd

    kv_specs = [pl.BlockSpec((None, PAGE_SIZE, FOX_HEADS, FOX_HD), page(i, 3)) for i in range(npg)]
    lf_specs = [pl.BlockSpec((None, FOX_HEADS, PAGE_SIZE), page(i, 2)) for i in range(npg)]
    return pl.pallas_call(
        functools.partial(_decode_kernel, tq=tq, npg=npg),
        grid_spec=pltpu.PrefetchScalarGridSpec(
            num_scalar_prefetch=1,
            grid=(B, n_pages // npg),
            in_specs=[
                pl.BlockSpec((tq, FOX_DIM), lambda b, j, pt: (b, 0)),
                pl.BlockSpec((1, PAGE_SIZE, FOX_DIM), lambda b, j, pt: (b, 0, 0)),
                pl.BlockSpec((1, PAGE_SIZE, FOX_DIM), lambda b, j, pt: (b, 0, 0)),
                pl.BlockSpec((tq, LANES), lambda b, j, pt: (b, 0)),
                pl.BlockSpec((1, FOX_HEADS, PAGE_SIZE), lambda b, j, pt: (b, 0, 0)),
            ] + kv_specs + kv_specs + lf_specs,
            out_specs=pl.BlockSpec((tq, FOX_DIM), lambda b, j, pt: (b, 0)),
            scratch_shapes=[
                pltpu.VMEM((FOX_HEADS * tq, 1), F32),
                pltpu.VMEM((FOX_HEADS * tq, 1), F32),
                pltpu.VMEM((FOX_HEADS * tq, FOX_HD), F32),
                pltpu.VMEM((FOX_HEADS, 1), F32),
            ],
        ),
        out_shape=jax.ShapeDtypeStruct((B * tq, FOX_DIM), F32),
        compiler_params=_cparams("parallel", "arbitrary"),
        name="decode_attention",
    )(page_table.reshape(-1), q, k_new, v_new, cum, cum_new_t, *([cache_k] * npg), *([cache_v] * npg),
      *([cache_lf_t] * npg))


def _moe(u, u3, x, W, l, gate, T, tm, bm):
    N, D = u.shape
    idx_t, w_t, rank_t, cnt = _router(u, W["router_wt"][l], W["router_bias"][l].reshape(N_EXPERTS, 1), tm)
    counts = cnt[:, 0]
    padded = (counts + bm - 1) // bm * bm
    pad_end = jnp.cumsum(padded)
    pad_start = pad_end - padded
    dest = (pad_start[idx_t] + rank_t).T.reshape(-1)
    n_blocks = N * TOP_K // bm + N_EXPERTS
    starts = jnp.arange(n_blocks, dtype=jnp.int32) * bm
    block_e = jnp.minimum(jnp.sum(pad_end[None, :] <= starts[:, None], axis=1), N_EXPERTS - 1).astype(jnp.int32)
    live = (starts < pad_end[-1]).astype(jnp.int32)
    xs3 = _dispatch(u3, dest, n_blocks * bm, min(256, N))
    y3 = _experts(xs3, block_e, live, W["exp_w_gate"], W["exp_w_up"], W["exp_w_down"], l, bm)
    routed3 = _combine(y3, dest, w_t.T.reshape(-1), N, min(128, N))
    return _moe_final(u, x, routed3, W["sh_wgu"][l], W["sh_wd"][l], gate, W["ln_g"][l, 1], W["ln_b"][l, 1], T, tm)


def _trunk(x, mods, kv_mod, s0, past, W, B, T):
    N = B * T
    tm = min(512, N)
    bm = 256 if N * TOP_K >= 256 * N_EXPERTS else 128
    x = x.reshape(N, D_MODEL)
    new_gla = []
    for l in range(DEPTH):
        sh_a, sc_a, g_a, sh_m, sc_m, g_m = jnp.split(mods[l], 6, axis=-1)
        if l < N_A:
            p = _modproj(x, sc_a, sh_a, W["gla_w_in"][l], T, tm, F32)
            if T >= GLA_CHUNK:
                Tp, tc = T, min(512, T)
            else:
                Tp = tc = GLA_CHUNK
                p = jnp.pad(p.reshape(B, T, -1), ((0, 0), (0, Tp - T), (0, 0))).reshape(B * Tp, -1)
            o, s_T = _gla(p, W["gla_w_a2"][l], W["gla_b_a"][l], W["gla_norm_g"][l], s0[l], B, Tp, tc, T)
            if Tp != T:
                o = o.reshape(B, Tp, -1)[:, :T].reshape(N, -1)
            new_gla.append(s_T)
            w_o = W["gla_w_o"][l]
        else:
            q = _modproj(x, sc_a, sh_a, W["fox_w_q"][l - N_A], T, tm, BF16 if past is None else F32)
            if past is None:
                o = _fox_attention(q, kb, vb, cum, cum_t, B, T, min(256, T))
            else:
                o = _decode_attention(past["page_table"], q, kb_pad, vb_pad, cum, cum_t, past["k"], past["v"],
                                      past["lf_t"], B, T)
            w_o = W["fox_w_o"][l - N_A]
        x, u, u3 = _proj_res_ln(o, w_o, x, g_a, W["ln_g"][l, 0], W["ln_b"][l, 0], sc_m, sh_m, T, tm)
        x = _moe(u, u3, x, W, l, g_m, T, tm, bm)
        if l == N_A - 1:
            sh_kv, sc_kv = jnp.split(kv_mod, 2, axis=-1)
            k_new, v_new, kb, vb, lf, cum = _kv_proj(x, sc_kv, sh_kv, W["kv_w"], W["fox_b_f"], T, tm)
            cum_h = cum[:, :FOX_HEADS].reshape(B, T, FOX_HEADS)
            if past is None:
                cum_t = cum_h.transpose(0, 2, 1).reshape(B * FOX_HEADS, T)
            else:
                rows = ((0, 0), (0, PAGE_SIZE - T), (0, 0))
                cum_t = jnp.pad(cum_h, rows).transpose(0, 2, 1)
                kb_pad = jnp.pad(kb.reshape(B, T, FOX_DIM), rows)
                vb_pad = jnp.pad(vb.reshape(B, T, FOX_DIM), rows)
    return (x.reshape(B, T, D_MODEL), jnp.stack(new_gla),
            k_new.reshape(B, T, FOX_HEADS, FOX_HD), v_new.reshape(B, T, FOX_HEADS, FOX_HD),
            lf[:, :FOX_HEADS].reshape(B, T, FOX_HEADS))


def kernel(x_prompt, x_sample, c_prompt, c_sample, state_gla, cache_k, cache_v, cache_logf, page_table, gla_w_in, gla_w_a2, gla_b_a, gla_norm_g, gla_w_o, kv_w_mod, kv_b_mod, kv_w, fox_b_f, fox_w_q, fox_w_o, ada_w, ada_b, ln_g, ln_b, router_w, router_bias, exp_w_gate, exp_w_up, exp_w_down, sh_w_gate, sh_w_up, sh_w_down):
    Bp, Tp, D = x_prompt.shape
    Bs, Ts, _ = x_sample.shape
    lane_pad = lambda w, n: jnp.pad(w, [(0, 0)] * (w.ndim - 1) + [(0, n - w.shape[-1])])
    W = dict(
        gla_w_in=jnp.concatenate(
            [gla_w_in[..., :GLA_IN_MAIN], lane_pad(gla_w_in[..., GLA_IN_MAIN:], LANES)], axis=-1).astype(BF16),
        gla_w_a2=jnp.pad(gla_w_a2, ((0, 0), (0, LANES - GLA_GATE_RANK), (0, 0))).astype(BF16),
        gla_b_a=gla_b_a.reshape(N_A, 1, GLA_DK),
        gla_norm_g=gla_norm_g.reshape(N_A, 1, GLA_DVH),
        gla_w_o=gla_w_o.astype(BF16),
        kv_w=jnp.concatenate([kv_w[:, :2 * FOX_DIM], lane_pad(kv_w[:, 2 * FOX_DIM:], LANES)], axis=-1).astype(BF16),
        fox_b_f=lane_pad(fox_b_f.reshape(1, FOX_HEADS), LANES),
        fox_w_q=fox_w_q.astype(BF16),
        fox_w_o=fox_w_o.astype(BF16),
        ln_g=ln_g, ln_b=ln_b,
        router_wt=jnp.swapaxes(router_w, 1, 2).astype(BF16),
        router_bias=router_bias,
        exp_w_gate=exp_w_gate, exp_w_up=exp_w_up, exp_w_down=exp_w_down,
        sh_wgu=jnp.concatenate([sh_w_gate, sh_w_up], axis=-1).astype(BF16),
        sh_wd=sh_w_down.astype(BF16),
    )
    c_all = jnp.concatenate([c_prompt, c_sample], axis=0)
    mods = _cond_linear(c_all, ada_w, ada_b, 1536)
    kv_mod = _cond_linear(c_all, kv_w_mod[None], kv_b_mod[None], 1024)[0]

    s0_prompt = jnp.zeros((N_A, Bp, GLA_HEADS, GLA_DKH, GLA_DVH), F32)
    out_p = _trunk(x_prompt, mods[:, :Bp], kv_mod[:Bp], s0_prompt, None, W, Bp, Tp)

    past = dict(page_table=page_table, k=cache_k, v=cache_v, lf_t=jnp.swapaxes(cache_logf, 1, 2))
    out_s = _trunk(x_sample, mods[:, Bp:], kv_mod[Bp:], state_gla, past, W, Bs, Ts)

    y_p, gla_p, k_p, v_p, lf_p = out_p
    y_s, gla_s, k_s, v_s, lf_s = out_s
    return (y_p, y_s, gla_p, gla_s, k_p, v_p, lf_p, k_s, v_s, lf_s)
```

```python
import functools

import jax
import jax.numpy as jnp
from jax import lax
from jax.experimental import pallas as pl
from jax.experimental.pallas import tpu as pltpu

F32 = jnp.float32
BF16 = jnp.bfloat16

D_MODEL = 1024
DEPTH = 4
N_A = DEPTH // 2
PAGE_SIZE = 128
GLA_HEADS = 4
GLA_DK = D_MODEL // 2
GLA_DV = D_MODEL
GLA_DKH = GLA_DK // GLA_HEADS
GLA_DVH = GLA_DV // GLA_HEADS
GLA_GATE_RANK = 16
GLA_GATE_TAU = 16.0
GLA_CHUNK = 64
FOX_HEADS = 8
FOX_HD = D_MODEL // FOX_HEADS
FOX_DIM = FOX_HEADS * FOX_HD
N_EXPERTS = 64
TOP_K = 8
N_GROUPS = 8
GROUP_SIZE = N_EXPERTS // N_GROUPS
TOPK_GROUPS = 4
D_EXPERT = 256
D_SHARED = 256
ROUTED_SCALE = 2.5
DEEPNORM_ALPHA = (2 * DEPTH) ** 0.25
LN_EPS = 1e-5
RMS_EPS = 1e-6

LANES = 128
GLA_IN_MAIN = 2 * GLA_DK + 2 * GLA_DV
GLA_IN_PAD = GLA_IN_MAIN + LANES
KV_PAD = 2 * FOX_DIM + LANES
MASK_NEG = -1e30
VMEM_LIMIT = 52 * 1024 * 1024


def _cparams(*sem):
    return pltpu.CompilerParams(dimension_semantics=sem, vmem_limit_bytes=VMEM_LIMIT)


def _dot(a, b):
    return jnp.dot(a, b, preferred_element_type=F32)


def _dot_nt(a, b):
    return lax.dot_general(a, b, (((1,), (1,)), ((), ())), preferred_element_type=F32)


def _dot_tn(a, b):
    return lax.dot_general(a, b, (((0,), (0,)), ((), ())), preferred_element_type=F32)


def _sigmoid(x):
    return 1.0 / (1.0 + jnp.exp(-x))


def _silu(x):
    return x * _sigmoid(x)


def _log_sigmoid(x):
    return jnp.minimum(x, 0.0) - jnp.log1p(jnp.exp(-jnp.abs(x)))


def _split3(x):
    hi = x.astype(BF16)
    r = x - hi.astype(F32)
    mid = r.astype(BF16)
    lo = (r - mid.astype(F32)).astype(BF16)
    return hi, mid, lo


def _dot_exact_lhs(mat01, x):
    hi, mid, lo = _split3(x)
    return _dot(mat01, hi) + _dot(mat01, mid) + _dot(mat01, lo)


def _dot_exact_rhs(x, mat01):
    hi, mid, lo = _split3(x)
    return _dot(hi, mat01) + _dot(mid, mat01) + _dot(lo, mat01)


def _layer_norm(h, g, b):
    mu = jnp.mean(h, axis=-1, keepdims=True)
    xc = h - mu
    var = jnp.mean(xc * xc, axis=-1, keepdims=True)
    return xc * lax.rsqrt(var + LN_EPS) * g + b


def _row_mod(m, T, tm):
    B, D = m.shape
    if T % tm == 0:
        tiles_per_batch = T // tm
        return m.reshape(B, 1, D), pl.BlockSpec((1, 1, D), lambda i: (i // tiles_per_batch, 0, 0))
    assert tm % T == 0
    e = jnp.repeat(m, T, axis=0).reshape(B * T // tm, tm, D)
    return e, pl.BlockSpec((1, tm, D), lambda i: (i, 0, 0))


def _cond_kernel(c_ref, w_ref, b_ref, o_ref):
    s = _silu(c_ref[...]).astype(BF16)
    o_ref[0] = _dot(s, w_ref[0].astype(BF16)) + b_ref[0]


def _cond_linear(c, w, b, tn):
    M, D = c.shape
    L, _, N = w.shape
    return pl.pallas_call(
        _cond_kernel,
        grid=(L, N // tn),
        in_specs=[
            pl.BlockSpec((M, D), lambda l, j: (0, 0)),
            pl.BlockSpec((1, D, tn), lambda l, j: (l, 0, j)),
            pl.BlockSpec((1, 1, tn), lambda l, j: (l, 0, j)),
        ],
        out_specs=pl.BlockSpec((1, M, tn), lambda l, j: (l, 0, j)),
        out_shape=jax.ShapeDtypeStruct((L, M, N), F32),
        compiler_params=_cparams("parallel", "parallel"),
        name="cond_linear",
    )(c, w, b.reshape(L, 1, N))


def _modproj_kernel(x_ref, sc_ref, sh_ref, w_ref, o_ref):
    u = (x_ref[...] * (1.0 + sc_ref[0]) + sh_ref[0]).astype(BF16)
    o_ref[...] = _dot(u, w_ref[...]).astype(o_ref.dtype)


def _modproj(x, sc, sh, w, T, tm, out_dtype):
    N, D = x.shape
    Dout = w.shape[1]
    sc3, sc_spec = _row_mod(sc, T, tm)
    sh3, sh_spec = _row_mod(sh, T, tm)
    return pl.pallas_call(
        _modproj_kernel,
        grid=(N // tm,),
        in_specs=[
            pl.BlockSpec((tm, D), lambda i: (i, 0)),
            sc_spec,
            sh_spec,
            pl.BlockSpec((D, Dout), lambda i: (0, 0)),
        ],
        out_specs=pl.BlockSpec((tm, Dout), lambda i: (i, 0)),
        out_shape=jax.ShapeDtypeStruct((N, Dout), out_dtype),
        compiler_params=_cparams("parallel"),
        name="modproj",
    )(x, sc3, sh3, w)


def _kv_kernel(x_ref, sc_ref, sh_ref, w_ref, bf_ref, k_ref, v_ref, kb_ref, vb_ref, lf_ref, cum_ref,
               carry_ref, *, T, tm):
    u = (x_ref[...] * (1.0 + sc_ref[0]) + sh_ref[0]).astype(BF16)
    p = _dot(u, w_ref[...])
    k = p[:, :FOX_DIM]
    v = p[:, FOX_DIM:2 * FOX_DIM]
    k_ref[...] = k
    v_ref[...] = v
    kb_ref[...] = k.astype(BF16)
    vb_ref[...] = v.astype(BF16)
    lf = _log_sigmoid(p[:, 2 * FOX_DIM:] + bf_ref[...])
    lf_ref[...] = lf
    if T % tm == 0:
        @pl.when(pl.program_id(0) % (T // tm) == 0)
        def _():
            carry_ref[...] = jnp.zeros_like(carry_ref)

        sb = LANES
        r = lax.broadcasted_iota(jnp.int32, (sb, sb), 0)
        c = lax.broadcasted_iota(jnp.int32, (sb, sb), 1)
        tri = (c <= r).astype(BF16)
        carry = carry_ref[...]
        for s in range(tm // sb):
            cs = _dot_exact_lhs(tri, lf[s * sb:(s + 1) * sb]) + carry
            cum_ref[s * sb:(s + 1) * sb, :] = cs
            carry = cs[sb - 1:sb, :]
        carry_ref[...] = carry
    else:
        r = lax.broadcasted_iota(jnp.int32, (tm, tm), 0)
        c = lax.broadcasted_iota(jnp.int32, (tm, tm), 1)
        tri = ((c <= r) & ((r // T) == (c // T))).astype(BF16)
        cum_ref[...] = _dot_exact_lhs(tri, lf)


def _kv_proj(x, sc, sh, w, bf, T, tm):
    N, D = x.shape
    sc3, sc_spec = _row_mod(sc, T, tm)
    sh3, sh_spec = _row_mod(sh, T, tm)
    row = lambda i: (i, 0)
    return pl.pallas_call(
        functools.partial(_kv_kernel, T=T, tm=tm),
        grid=(N // tm,),
        in_specs=[
            pl.BlockSpec((tm, D), row),
            sc_spec,
            sh_spec,
            pl.BlockSpec((D, KV_PAD), lambda i: (0, 0)),
            pl.BlockSpec((1, LANES), lambda i: (0, 0)),
        ],
        out_specs=[
            pl.BlockSpec((tm, FOX_DIM), row),
            pl.BlockSpec((tm, FOX_DIM), row),
            pl.BlockSpec((tm, FOX_DIM), row),
            pl.BlockSpec((tm, FOX_DIM), row),
            pl.BlockSpec((tm, LANES), row),
            pl.BlockSpec((tm, LANES), row),
        ],
        out_shape=[
            jax.ShapeDtypeStruct((N, FOX_DIM), F32),
            jax.ShapeDtypeStruct((N, FOX_DIM), F32),
            jax.ShapeDtypeStruct((N, FOX_DIM), BF16),
            jax.ShapeDtypeStruct((N, FOX_DIM), BF16),
            jax.ShapeDtypeStruct((N, LANES), F32),
            jax.ShapeDtypeStruct((N, LANES), F32),
        ],
        scratch_shapes=[pltpu.VMEM((1, LANES), F32)],
        compiler_params=_cparams("arbitrary"),
        name="kv_proj",
    )(x, sc3, sh3, w, bf)


def _gla_kernel(q_ref, k_ref, v_ref, g_ref, a_ref, wa_ref, ba_ref, ng_ref, s0_ref, o_ref, st_ref,
                state_ref, *, c, nsub, t_valid):
    t = pl.program_id(2)

    @pl.when(t == 0)
    def _():
        state_ref[...] = s0_ref[0, 0].T

    r = lax.broadcasted_iota(jnp.int32, (c, c), 0)
    cc = lax.broadcasted_iota(jnp.int32, (c, c), 1)
    causal = cc <= r
    tri = causal.astype(BF16)
    scale = GLA_DKH ** -0.5
    for i in range(nsub):
        sl = slice(i * c, (i + 1) * c)
        z = _dot(a_ref[sl, :].astype(BF16), wa_ref[...]) + ba_ref[...]
        la = _log_sigmoid(z) / GLA_GATE_TAU
        if t_valid < c:
            la = jnp.where(lax.broadcasted_iota(jnp.int32, la.shape, 0) < t_valid, la, 0.0)
        bc = _dot_exact_lhs(tri, la)
        b_last = bc[c - 1:c, :]
        k = k_ref[sl, :]
        v = v_ref[sl, :].astype(BF16)
        q_dec = ((q_ref[sl, :] * scale) * jnp.exp(bc)).astype(BF16)
        k_dec = (k * jnp.exp(-bc)).astype(BF16)
        k_end = (k * jnp.exp(b_last - bc)).astype(BF16)
        st = state_ref[...]
        inter = _dot_nt(q_dec, st.astype(BF16))
        att = jnp.where(causal, _dot_nt(q_dec, k_dec), 0.0).astype(BF16)
        o = inter + _dot(att, v)
        state_ref[...] = st * jnp.exp(b_last) + _dot_tn(v, k_end)
        o = o * lax.rsqrt(jnp.mean(o * o, axis=-1, keepdims=True) + RMS_EPS) * ng_ref[...]
        o_ref[sl, :] = (o * _silu(g_ref[sl, :])).astype(o_ref.dtype)

    @pl.when(t == pl.num_programs(2) - 1)
    def _():
        st_ref[0, 0] = state_ref[...].T


def _gla(p, wa, ba, ng, s0, B, Tp, tc, t_valid):
    nt = Tp // tc
    c = GLA_CHUNK
    row = lambda b, h, t: b * nt + t
    kq = GLA_DK // GLA_DKH
    kv = 2 * GLA_DK // GLA_DVH
    kg = kv + GLA_HEADS
    ka = GLA_IN_MAIN // LANES
    return pl.pallas_call(
        functools.partial(_gla_kernel, c=c, nsub=tc // c, t_valid=t_valid),
        grid=(B, GLA_HEADS, nt),
        in_specs=[
            pl.BlockSpec((tc, GLA_DKH), lambda b, h, t: (row(b, h, t), h)),
            pl.BlockSpec((tc, GLA_DKH), lambda b, h, t: (row(b, h, t), kq + h)),
            pl.BlockSpec((tc, GLA_DVH), lambda b, h, t: (row(b, h, t), kv + h)),
            pl.BlockSpec((tc, GLA_DVH), lambda b, h, t: (row(b, h, t), kg + h)),
            pl.BlockSpec((tc, LANES), lambda b, h, t: (row(b, h, t), ka)),
            pl.BlockSpec((LANES, GLA_DKH), lambda b, h, t: (0, h)),
            pl.BlockSpec((1, GLA_DKH), lambda b, h, t: (0, h)),
            pl.BlockSpec((1, GLA_DVH), lambda b, h, t: (0, 0)),
            pl.BlockSpec((1, 1, GLA_DKH, GLA_DVH), lambda b, h, t: (b, h, 0, 0)),
        ],
        out_specs=[
            pl.BlockSpec((tc, GLA_DVH), lambda b, h, t: (row(b, h, t), h)),
            pl.BlockSpec((1, 1, GLA_DKH, GLA_DVH), lambda b, h, t: (b, h, 0, 0)),
        ],
        out_shape=[
            jax.ShapeDtypeStruct((B * Tp, GLA_DV), BF16),
            jax.ShapeDtypeStruct((B, GLA_HEADS, GLA_DKH, GLA_DVH), F32),
        ],
        scratch_shapes=[pltpu.VMEM((GLA_DVH, GLA_DKH), F32)],
        compiler_params=_cparams("parallel", "parallel", "arbitrary"),
        name="gla",
    )(p, p, p, p, p, wa, ba, ng, s0)


def _to_row_tiles(x):
    parts = [x[:, c * LANES:(c + 1) * LANES] for c in range(x.shape[1] // LANES)]
    return jnp.swapaxes(jnp.stack(parts, axis=0), 0, 1)


def _from_row_tiles(x3):
    xt = jnp.swapaxes(x3, 0, 1)
    return jnp.concatenate([xt[c] for c in range(xt.shape[0])], axis=1)


def _proj_res_ln_kernel(a_ref, w_ref, x_ref, gate_ref, lg_ref, lb_ref, sc_ref, sh_ref, xo_ref, uo_ref, u3_ref):
    y = _dot(a_ref[...].astype(BF16), w_ref[...])
    h = DEEPNORM_ALPHA * x_ref[...] + (1.0 + gate_ref[0]) * y
    xn = _layer_norm(h, lg_ref[...], lb_ref[...])
    xo_ref[...] = xn
    u = xn * (1.0 + sc_ref[0]) + sh_ref[0]
    uo_ref[...] = u.astype(uo_ref.dtype)
    u3_ref[...] = _to_row_tiles(u)


def _proj_res_ln(a, w, x, gate, lg, lb, sc, sh, T, tm):
    N, D = x.shape
    g3, g_spec = _row_mod(gate, T, tm)
    sc3, sc_spec = _row_mod(sc, T, tm)
    sh3, sh_spec = _row_mod(sh, T, tm)
    row = lambda i: (i, 0)
    fixed = lambda i: (0, 0)
    return pl.pallas_call(
        _proj_res_ln_kernel,
        grid=(N // tm,),
        in_specs=[
            pl.BlockSpec((tm, a.shape[1]), row),
            pl.BlockSpec(w.shape, fixed),
            pl.BlockSpec((tm, D), row),
            g_spec,
            pl.BlockSpec((1, D), fixed),
            pl.BlockSpec((1, D), fixed),
            sc_spec,
            sh_spec,
        ],
        out_specs=[pl.BlockSpec((tm, D), row), pl.BlockSpec((tm, D), row),
                   pl.BlockSpec((tm, D // LANES, LANES), lambda i: (i, 0, 0))],
        out_shape=[jax.ShapeDtypeStruct((N, D), F32), jax.ShapeDtypeStruct((N, D), BF16),
                   jax.ShapeDtypeStruct((N, D // LANES, LANES), F32)],
        compiler_params=_cparams("parallel"),
        name="proj_res_ln",
    )(a, w, x, g3, lg.reshape(1, D), lb.reshape(1, D), sc3, sh3)


def _router_kernel(u_ref, wr_ref, rb_ref, idx_ref, w_ref, rank_ref, cnt_ref, carry_ref, *, tm):
    @pl.when(pl.program_id(0) == 0)
    def _():
        carry_ref[...] = jnp.zeros_like(carry_ref)

    s = _sigmoid(_dot_nt(wr_ref[...], u_ref[...]))
    sb = s + rb_ref[...]
    neg = -jnp.inf
    sub = lax.broadcasted_iota(jnp.int32, (GROUP_SIZE, tm), 0)
    gs = GROUP_SIZE
    s_g = [s[g * gs:(g + 1) * gs] for g in range(N_GROUPS)]
    sb_g = [sb[g * gs:(g + 1) * gs] for g in range(N_GROUPS)]
    eid_g = [sub + g * gs for g in range(N_GROUPS)]

    score = []
    for g in range(N_GROUPS):
        m1 = jnp.max(sb_g[g], axis=0, keepdims=True)
        first = jnp.min(jnp.where(sb_g[g] == m1, sub, gs), axis=0, keepdims=True)
        m2 = jnp.max(jnp.where(sub == first, neg, sb_g[g]), axis=0, keepdims=True)
        score.append(m1 + m2)

    chosen = [jnp.zeros((1, tm), jnp.int32) for _ in range(N_GROUPS)]
    for _ in range(TOPK_GROUPS):
        m = score[0]
        for g in range(1, N_GROUPS):
            m = jnp.maximum(m, score[g])
        gi = jnp.full((1, tm), N_GROUPS, jnp.int32)
        for g in range(N_GROUPS - 1, -1, -1):
            gi = jnp.where(score[g] == m, g, gi)
        for g in range(N_GROUPS):
            hit = gi == g
            chosen[g] = jnp.where(hit, 1, chosen[g])
            score[g] = jnp.where(hit, neg, score[g])
    cand = [jnp.where(chosen[g] > 0, sb_g[g], neg) for g in range(N_GROUPS)]

    onehot = [jnp.zeros((gs, tm), F32) for _ in range(N_GROUPS)]
    idxs, ws = [], []
    for _ in range(TOP_K):
        mx = cand[0]
        for g in range(1, N_GROUPS):
            mx = jnp.maximum(mx, cand[g])
        m = jnp.max(mx, axis=0, keepdims=True)
        lo = jnp.where(cand[0] == m, eid_g[0], N_EXPERTS)
        for g in range(1, N_GROUPS):
            lo = jnp.minimum(lo, jnp.where(cand[g] == m, eid_g[g], N_EXPERTS))
        idx = jnp.min(lo, axis=0, keepdims=True)
        wsum = jnp.zeros((gs, tm), F32)
        for g in range(N_GROUPS):
            hit = eid_g[g] == idx
            wsum = wsum + jnp.where(hit, s_g[g], 0.0)
            cand[g] = jnp.where(hit, neg, cand[g])
            onehot[g] = jnp.where(hit, 1.0, onehot[g])
        idxs.append(idx)
        ws.append(jnp.sum(wsum, axis=0, keepdims=True))
    total = ws[0]
    for k in range(1, TOP_K):
        total = total + ws[k]
    for k in range(TOP_K):
        idx_ref[k:k + 1, :] = idxs[k]
        w_ref[k:k + 1, :] = ws[k] / total * ROUTED_SCALE

    oh = jnp.concatenate(onehot, axis=0)
    r = lax.broadcasted_iota(jnp.int32, (tm, tm), 0)
    c = lax.broadcasted_iota(jnp.int32, (tm, tm), 1)
    before = (r < c).astype(BF16)
    pre = _dot(oh.astype(BF16), before) + carry_ref[...]
    for k in range(TOP_K):
        acc = jnp.zeros((gs, tm), F32)
        for g in range(N_GROUPS):
            acc = acc + jnp.where(eid_g[g] == idxs[k], pre[g * gs:(g + 1) * gs], 0.0)
        rank_ref[k:k + 1, :] = jnp.sum(acc, axis=0, keepdims=True).astype(jnp.int32)
    carry_ref[...] = carry_ref[...] + jnp.sum(oh, axis=1, keepdims=True)
    cnt_ref[...] = jnp.broadcast_to(carry_ref[...], cnt_ref.shape).astype(jnp.int32)


def _router(u, wr_t, rb, tm):
    N, D = u.shape
    col = lambda i: (0, i)
    fixed = lambda i: (0, 0)
    return pl.pallas_call(
        functools.partial(_router_kernel, tm=tm),
        grid=(N // tm,),
        in_specs=[
            pl.BlockSpec((tm, D), lambda i: (i, 0)),
            pl.BlockSpec((N_EXPERTS, D), fixed),
            pl.BlockSpec((N_EXPERTS, 1), fixed),
        ],
        out_specs=[
            pl.BlockSpec((TOP_K, tm), col),
            pl.BlockSpec((TOP_K, tm), col),
            pl.BlockSpec((TOP_K, tm), col),
            pl.BlockSpec((N_EXPERTS, LANES), fixed),
        ],
        out_shape=[
            jax.ShapeDtypeStruct((TOP_K, N), jnp.int32),
            jax.ShapeDtypeStruct((TOP_K, N), F32),
            jax.ShapeDtypeStruct((TOP_K, N), jnp.int32),
            jax.ShapeDtypeStruct((N_EXPERTS, LANES), jnp.int32),
        ],
        scratch_shapes=[pltpu.VMEM((N_EXPERTS, 1), F32)],
        compiler_params=_cparams("arbitrary"),
        name="router",
    )(u, wr_t, rb)


def _experts_kernel(be_ref, live_ref, x_ref, wg_ref, wu_ref, wd_ref, o_ref, wgb_ref, wub_ref, wdb_ref):
    i = pl.program_id(0)
    e = be_ref[i]
    prev = be_ref[jnp.maximum(i - 1, 0)]

    @pl.when((i == 0) | (e != prev))
    def _():
        wgb_ref[...] = wg_ref[...].astype(BF16)
        wub_ref[...] = wu_ref[...].astype(BF16)
        wdb_ref[...] = wd_ref[...].astype(BF16)

    @pl.when(live_ref[i] > 0)
    def _():
        x = _from_row_tiles(x_ref[...]).astype(BF16)
        h = (_silu(_dot(x, wgb_ref[...])) * _dot(x, wub_ref[...])).astype(BF16)
        o_ref[...] = _to_row_tiles(_dot(h, wdb_ref[...]))

    @pl.when(live_ref[i] == 0)
    def _():
        o_ref[...] = jnp.zeros_like(o_ref)


def _experts(xs3, block_e, live, wg, wu, wd, layer, bm):
    R, S, _ = xs3.shape
    D = S * LANES
    return pl.pallas_call(
        _experts_kernel,
        grid_spec=pltpu.PrefetchScalarGridSpec(
            num_scalar_prefetch=2,
            grid=(R // bm,),
            in_specs=[
                pl.BlockSpec((bm, S, LANES), lambda i, be, lv: (i, 0, 0)),
                pl.BlockSpec((None, None, D, D_EXPERT), lambda i, be, lv: (layer, be[i], 0, 0)),
                pl.BlockSpec((None, None, D, D_EXPERT), lambda i, be, lv: (layer, be[i], 0, 0)),
                pl.BlockSpec((None, None, D_EXPERT, D), lambda i, be, lv: (layer, be[i], 0, 0)),
            ],
            out_specs=pl.BlockSpec((bm, S, LANES), lambda i, be, lv: (i, 0, 0)),
            scratch_shapes=[
                pltpu.VMEM((D, D_EXPERT), BF16),
                pltpu.VMEM((D, D_EXPERT), BF16),
                pltpu.VMEM((D_EXPERT, D), BF16),
            ],
        ),
        out_shape=jax.ShapeDtypeStruct((R, S, LANES), F32),
        compiler_params=_cparams("arbitrary"),
        name="experts",
    )(block_e, live, xs3, wg, wu, wd)


ISSUE_TOKENS = 2


def _dispatch_kernel(dest_ref, u_ref, xs_in, xs_hbm, sem, *, tmd):
    del xs_in

    def issue(t2, carry):
        first = t2 * (ISSUE_TOKENS * TOP_K)
        rows = [dest_ref[first + r] for r in range(ISSUE_TOKENS * TOP_K)]
        for r, row in enumerate(rows):
            pltpu.make_async_copy(u_ref.at[t2 * ISSUE_TOKENS + r // TOP_K], xs_hbm.at[row], sem).start()
        return carry

    lax.fori_loop(0, tmd // ISSUE_TOKENS, issue, 0)
    for k in range(TOP_K):
        pltpu.make_async_copy(u_ref, xs_hbm.at[pl.ds(0, tmd)], sem).wait()


def _dispatch(u3, dest, R, tmd):
    N, S, _ = u3.shape
    return pl.pallas_call(
        functools.partial(_dispatch_kernel, tmd=tmd),
        grid=(N // tmd,),
        in_specs=[
            pl.BlockSpec((tmd * TOP_K,), lambda i: (i,), memory_space=pltpu.SMEM),
            pl.BlockSpec((tmd, S, LANES), lambda i: (i, 0, 0)),
            pl.BlockSpec(memory_space=pl.ANY),
        ],
        out_specs=pl.BlockSpec(memory_space=pl.ANY),
        out_shape=jax.ShapeDtypeStruct((R, S, LANES), F32),
        scratch_shapes=[pltpu.SemaphoreType.DMA(())],
        input_output_aliases={2: 0},
        compiler_params=_cparams("arbitrary"),
        name="moe_dispatch",
    )(dest, u3, jnp.zeros((R, S, LANES), F32))


def _combine_kernel(dest_ref, w_ref, y_hbm, o_ref, buf_ref, sem, *, tmc):
    def issue(t2, carry):
        first = t2 * (ISSUE_TOKENS * TOP_K)
        rows = [dest_ref[first + r] for r in range(ISSUE_TOKENS * TOP_K)]
        for r, row in enumerate(rows):
            pltpu.make_async_copy(y_hbm.at[row], buf_ref.at[first + r], sem).start(priority=r % 2)
        return carry

    lax.fori_loop(0, tmc // ISSUE_TOKENS, issue, 0)
    pltpu.make_async_copy(y_hbm.at[pl.ds(0, tmc * TOP_K)], buf_ref, sem).wait()

    def reduce(t, carry):
        acc = w_ref[t * TOP_K] * buf_ref[t * TOP_K]
        for k in range(1, TOP_K):
            acc = acc + w_ref[t * TOP_K + k] * buf_ref[t * TOP_K + k]
        o_ref[t] = acc
        return carry

    lax.fori_loop(0, tmc, reduce, 0)


def _combine(y3, dest, w, N, tmc):
    _, S, _ = y3.shape
    return pl.pallas_call(
        functools.partial(_combine_kernel, tmc=tmc),
        grid=(N // tmc,),
        in_specs=[
            pl.BlockSpec((tmc * TOP_K,), lambda i: (i,), memory_space=pltpu.SMEM),
            pl.BlockSpec((tmc * TOP_K,), lambda i: (i,), memory_space=pltpu.SMEM),
            pl.BlockSpec(memory_space=pl.ANY),
        ],
        out_specs=pl.BlockSpec((tmc, S, LANES), lambda i: (i, 0, 0)),
        out_shape=jax.ShapeDtypeStruct((N, S, LANES), F32),
        scratch_shapes=[pltpu.VMEM((tmc * TOP_K, S, LANES), F32), pltpu.SemaphoreType.DMA(())],
        compiler_params=_cparams("arbitrary"),
        name="moe_combine",
    )(dest, w, y3)


def _moe_final_kernel(u_ref, x_ref, r_ref, wgu_ref, wd_ref, gate_ref, lg_ref, lb_ref, xo_ref):
    gu = _dot(u_ref[...], wgu_ref[...])
    h = (_silu(gu[:, :D_SHARED]) * gu[:, D_SHARED:]).astype(BF16)
    y = _from_row_tiles(r_ref[...]) + _dot(h, wd_ref[...])
    hres = DEEPNORM_ALPHA * x_ref[...] + (1.0 + gate_ref[0]) * y
    xo_ref[...] = _layer_norm(hres, lg_ref[...], lb_ref[...])


def _moe_final(u, x, routed, wgu, wd, gate, lg, lb, T, tm):
    N, D = x.shape
    g3, g_spec = _row_mod(gate, T, tm)
    row = lambda i: (i, 0)
    fixed = lambda i: (0, 0)
    return pl.pallas_call(
        _moe_final_kernel,
        grid=(N // tm,),
        in_specs=[
            pl.BlockSpec((tm, D), row),
            pl.BlockSpec((tm, D), row),
            pl.BlockSpec((tm, D // LANES, LANES), lambda i: (i, 0, 0)),
            pl.BlockSpec(wgu.shape, fixed),
            pl.BlockSpec(wd.shape, fixed),
            g_spec,
            pl.BlockSpec((1, D), fixed),
            pl.BlockSpec((1, D), fixed),
        ],
        out_specs=pl.BlockSpec((tm, D), row),
        out_shape=jax.ShapeDtypeStruct((N, D), F32),
        compiler_params=_cparams("parallel"),
        name="moe_final",
    )(u, x, routed, wgu, wd, g3, lg.reshape(1, D), lb.reshape(1, D))


def _fox_kernel(q_ref, k_ref, v_ref, fq_ref, fk_ref, o_ref, m_ref, l_ref, acc_ref, *, bq):
    i = pl.program_id(1)
    j = pl.program_id(2)

    @pl.when(j == 0)
    def _():
        m_ref[...] = jnp.full_like(m_ref, MASK_NEG)
        l_ref[...] = jnp.zeros_like(l_ref)
        acc_ref[...] = jnp.zeros_like(acc_ref)

    @pl.when(j <= i)
    def _():
        kpos = j * bq + lax.broadcasted_iota(jnp.int32, (bq, bq), 0)
        qpos = i * bq + lax.broadcasted_iota(jnp.int32, (bq, bq), 1)
        visible = kpos <= qpos
        scale = FOX_HD ** -0.5
        for h in range(FOX_HEADS):
            hs = slice(h * FOX_HD, (h + 1) * FOX_HD)
            s = _dot_nt(k_ref[:, hs], q_ref[:, hs]) * scale + fq_ref[h:h + 1, :] - fk_ref[:, h:h + 1]
            s = jnp.where(visible, s, MASK_NEG)
            m_prev = m_ref[h:h + 1, :]
            m_new = jnp.maximum(m_prev, jnp.max(s, axis=0, keepdims=True))
            alpha = jnp.exp(m_prev - m_new)
            p = jnp.exp(s - m_new)
            l_ref[h:h + 1, :] = alpha * l_ref[h:h + 1, :] + jnp.sum(p, axis=0, keepdims=True)
            acc_ref[hs, :] = alpha * acc_ref[hs, :] + _dot_tn(v_ref[:, hs], p.astype(BF16))
            m_ref[h:h + 1, :] = m_new

    @pl.when(j == pl.num_programs(2) - 1)
    def _():
        for h in range(FOX_HEADS):
            hs = slice(h * FOX_HD, (h + 1) * FOX_HD)
            o_ref[:, hs] = (acc_ref[hs, :] / l_ref[h:h + 1, :]).T.astype(o_ref.dtype)


def _fox_attention(q, kb, vb, cum, cum_t, B, T, bq):
    nq = T // bq
    kv_map = lambda b, i, j: (b * nq + jnp.minimum(j, i), 0)
    return pl.pallas_call(
        functools.partial(_fox_kernel, bq=bq),
        grid=(B, nq, nq),
        in_specs=[
            pl.BlockSpec((bq, FOX_DIM), lambda b, i, j: (b * nq + i, 0)),
            pl.BlockSpec((bq, FOX_DIM), kv_map),
            pl.BlockSpec((bq, FOX_DIM), kv_map),
            pl.BlockSpec((FOX_HEADS, bq), lambda b, i, j: (b, i)),
            pl.BlockSpec((bq, LANES), kv_map),
        ],
        out_specs=pl.BlockSpec((bq, FOX_DIM), lambda b, i, j: (b * nq + i, 0)),
        out_shape=jax.ShapeDtypeStruct((B * T, FOX_DIM), BF16),
        scratch_shapes=[
            pltpu.VMEM((FOX_HEADS, bq), F32),
            pltpu.VMEM((FOX_HEADS, bq), F32),
            pltpu.VMEM((FOX_DIM, bq), F32),
        ],
        compiler_params=_cparams("parallel", "parallel", "arbitrary"),
        name="fox_attention",
    )(q, kb, vb, cum_t, cum)


def _decode_kernel(pt_ref, q_ref, kn_ref, vn_ref, fq_ref, fkn_ref, *rest, tq, npg):
    kc, vc, lfc = rest[:npg], rest[npg:2 * npg], rest[2 * npg:3 * npg]
    o_ref, m_ref, l_ref, acc_ref, carry_ref = rest[3 * npg:]
    j = pl.program_id(1)
    H = FOX_HEADS
    scale = FOX_HD ** -0.5
    q = [q_ref[:, h * FOX_HD:(h + 1) * FOX_HD].astype(BF16) for h in range(H)]
    fq = jnp.concatenate([fq_ref[:, h:h + 1] for h in range(H)], axis=0)

    def per_head_rows(x):
        return jnp.concatenate([jnp.broadcast_to(x[h:h + 1, :], (tq, x.shape[1])) for h in range(H)], axis=0)

    def attend(s, values, state):
        m_prev, l_prev, acc_prev = state
        m_new = jnp.maximum(m_prev, jnp.max(s, axis=1, keepdims=True))
        alpha = jnp.exp(m_prev - m_new)
        p = jnp.exp(s - m_new)
        l_new = alpha * l_prev + jnp.sum(p, axis=1, keepdims=True)
        pv = []
        for h in range(H):
            o = None
            for i, value in enumerate(values):
                ph = p[h * tq:(h + 1) * tq, i * PAGE_SIZE:(i + 1) * PAGE_SIZE].astype(BF16)
                t = _dot(ph, value(h))
                o = t if o is None else o + t
            pv.append(o)
        return m_new, l_new, alpha * acc_prev + jnp.concatenate(pv, axis=0)

    @pl.when(j == 0)
    def _():
        carry_ref[...] = jnp.zeros_like(carry_ref)
        r = lax.broadcasted_iota(jnp.int32, (H * tq, PAGE_SIZE), 0) % tq
        c = lax.broadcasted_iota(jnp.int32, (H * tq, PAGE_SIZE), 1)
        s = jnp.concatenate(
            [_dot_nt(q[h], kn_ref[0, :, h * FOX_HD:(h + 1) * FOX_HD]) for h in range(H)], axis=0)
        s = s * scale + fq - per_head_rows(fkn_ref[0])
        s = jnp.where(c <= r, s, MASK_NEG)
        state = (jnp.full((H * tq, 1), MASK_NEG, F32), jnp.zeros((H * tq, 1), F32),
                 jnp.zeros((H * tq, FOX_HD), F32))
        m_ref[...], l_ref[...], acc_ref[...] = attend(
            s, [lambda h: vn_ref[0, :, h * FOX_HD:(h + 1) * FOX_HD]], state)

    r = lax.broadcasted_iota(jnp.int32, (PAGE_SIZE, PAGE_SIZE), 0)
    c = lax.broadcasted_iota(jnp.int32, (PAGE_SIZE, PAGE_SIZE), 1)
    after = (r > c).astype(BF16)
    carry = carry_ref[...]
    blocks = []
    for i in range(npg):
        lf = lfc[i][...]
        suffix = _dot_exact_rhs(lf, after) + carry
        carry = carry + jnp.sum(lf, axis=1, keepdims=True)
        kt = jnp.swapaxes(kc[i][...], 0, 1)
        sk = jnp.concatenate([_dot_nt(q[h], kt[h].astype(BF16)) for h in range(H)], axis=0)
        blocks.append(sk * scale + per_head_rows(suffix))
    carry_ref[...] = carry
    s = jnp.concatenate(blocks, axis=1) + fq
    vts = [jnp.swapaxes(vc[i][...], 0, 1) for i in range(npg)]
    values = [lambda h, i=i: vts[i][h].astype(BF16) for i in range(npg)]
    m_new, l_new, acc_new = attend(s, values, (m_ref[...], l_ref[...], acc_ref[...]))
    m_ref[...] = m_new
    l_ref[...] = l_new
    acc_ref[...] = acc_new

    @pl.when(j == pl.num_programs(1) - 1)
    def _():
        out = acc_new / l_new
        for h in range(H):
            o_ref[:, h * FOX_HD:(h + 1) * FOX_HD] = out[h * tq:(h + 1) * tq, :]


DECODE_PAGES_PER_STEP = 8


def _decode_attention(page_table, q, k_new, v_new, cum, cum_new_t, cache_k, cache_v, cache_lf_t, B, tq):
    n_pages = page_table.shape[1]
    npg = DECODE_PAGES_PER_STEP
    assert n_pages % npg == 0

    def page(i, nd):
        return lambda b, j, pt: (pt[b * n_pages + (n_pages - 1 - (j * npg + i))],) + (0,) * nd

    kv_specs = [pl.BlockSpec((None, PAGE_SIZE, FOX_HEADS, FOX_HD), page(i, 3)) for i in range(npg)]
    lf_specs = [pl.BlockSpec((None, FOX_HEADS, PAGE_SIZE), page(i, 2)) for i in range(npg)]
    return pl.pallas_call(
        functools.partial(_decode_kernel, tq=tq, npg=npg),
        grid_spec=pltpu.PrefetchScalarGridSpec(
            num_scalar_prefetch=1,
            grid=(B, n_pages // npg),
            in_specs=[
                pl.BlockSpec((tq, FOX_DIM), lambda b, j, pt: (b, 0)),
                pl.BlockSpec((1, PAGE_SIZE, FOX_DIM), lambda b, j, pt: (b, 0, 0)),
                pl.BlockSpec((1, PAGE_SIZE, FOX_DIM), lambda b, j, pt: (b, 0, 0)),
                pl.BlockSpec((tq, LANES), lambda b, j, pt: (b, 0)),
                pl.BlockSpec((1, FOX_HEADS, PAGE_SIZE), lambda b, j, pt: (b, 0, 0)),
            ] + kv_specs + kv_specs + lf_specs,
            out_specs=pl.BlockSpec((tq, FOX_DIM), lambda b, j, pt: (b, 0)),
            scratch_shapes=[
                pltpu.VMEM((FOX_HEADS * tq, 1), F32),
                pltpu.VMEM((FOX_HEADS * tq, 1), F32),
                pltpu.VMEM((FOX_HEADS * tq, FOX_HD), F32),
                pltpu.VMEM((FOX_HEADS, 1), F32),
            ],
        ),
        out_shape=jax.ShapeDtypeStruct((B * tq, FOX_DIM), F32),
        compiler_params=_cparams("parallel", "arbitrary"),
        name="decode_attention",
    )(page_table.reshape(-1), q, k_new, v_new, cum, cum_new_t, *([cache_k] * npg), *([cache_v] * npg),
      *([cache_lf_t] * npg))


def _moe(u, u3, x, W, l, gate, T, tm, bm):
    N, D = u.shape
    idx_t, w_t, rank_t, cnt = _router(u, W["router_wt"][l], W["router_bias"][l].reshape(N_EXPERTS, 1), tm)
    counts = cnt[:, 0]
    padded = (counts + bm - 1) // bm * bm
    pad_end = jnp.cumsum(padded)
    pad_start = pad_end - padded
    first = jnp.sum(jnp.where(idx_t[:, :, None] == jnp.arange(N_EXPERTS), pad_start, 0), axis=-1)
    dest = (first + rank_t).T.reshape(-1)
    n_blocks = N * TOP_K // bm + N_EXPERTS
    starts = jnp.arange(n_blocks, dtype=jnp.int32) * bm
    block_e = jnp.minimum(jnp.sum(pad_end[None, :] <= starts[:, None], axis=1), N_EXPERTS - 1).astype(jnp.int32)
    live = (starts < pad_end[-1]).astype(jnp.int32)
    xs3 = _dispatch(u3, dest, n_blocks * bm, min(256, N))
    y3 = _experts(xs3, block_e, live, W["exp_w_gate"], W["exp_w_up"], W["exp_w_down"], l, bm)
    routed3 = _combine(y3, dest, w_t.T.reshape(-1), N, min(128, N))
    return _moe_final(u, x, routed3, W["sh_wgu"][l], W["sh_wd"][l], gate, W["ln_g"][l, 1], W["ln_b"][l, 1], T, tm)


def _trunk(x, mods, kv_mod, s0, past, W, B, T):
    N = B * T
    tm = min(512, N)
    bm = 256 if N * TOP_K >= 256 * N_EXPERTS else 128
    x = x.reshape(N, D_MODEL)
    new_gla = []
    for l in range(DEPTH):
        sh_a, sc_a, g_a, sh_m, sc_m, g_m = jnp.split(mods[l], 6, axis=-1)
        if l < N_A:
            p = _modproj(x, sc_a, sh_a, W["gla_w_in"][l], T, tm, F32)
            if T >= GLA_CHUNK:
                Tp, tc = T, min(512, T)
            else:
                Tp = tc = GLA_CHUNK
                p = jnp.pad(p.reshape(B, T, -1), ((0, 0), (0, Tp - T), (0, 0))).reshape(B * Tp, -1)
            o, s_T = _gla(p, W["gla_w_a2"][l], W["gla_b_a"][l], W["gla_norm_g"][l], s0[l], B, Tp, tc, T)
            if Tp != T:
                o = o.reshape(B, Tp, -1)[:, :T].reshape(N, -1)
            new_gla.append(s_T)
            w_o = W["gla_w_o"][l]
        else:
            q = _modproj(x, sc_a, sh_a, W["fox_w_q"][l - N_A], T, tm, BF16 if past is None else F32)
            if past is None:
                o = _fox_attention(q, kb, vb, cum, cum_t, B, T, min(256, T))
            else:
                o = _decode_attention(past["page_table"], q, kb_pad, vb_pad, cum, cum_t, past["k"], past["v"],
                                      past["lf_t"], B, T)
            w_o = W["fox_w_o"][l - N_A]
        x, u, u3 = _proj_res_ln(o, w_o, x, g_a, W["ln_g"][l, 0], W["ln_b"][l, 0], sc_m, sh_m, T, tm)
        x = _moe(u, u3, x, W, l, g_m, T, tm, bm)
        if l == N_A - 1:
            sh_kv, sc_kv = jnp.split(kv_mod, 2, axis=-1)
            k_new, v_new, kb, vb, lf, cum = _kv_proj(x, sc_kv, sh_kv, W["kv_w"], W["fox_b_f"], T, tm)
            cum_h = cum[:, :FOX_HEADS].reshape(B, T, FOX_HEADS)
            if past is None:
                cum_t = cum_h.transpose(0, 2, 1).reshape(B * FOX_HEADS, T)
            else:
                rows = ((0, 0), (0, PAGE_SIZE - T), (0, 0))
                cum_t = jnp.pad(cum_h, rows).transpose(0, 2, 1)
                kb_pad = jnp.pad(kb.reshape(B, T, FOX_DIM), rows)
                vb_pad = jnp.pad(vb.reshape(B, T, FOX_DIM), rows)
    return (x.reshape(B, T, D_MODEL), jnp.stack(new_gla),
            k_new.reshape(B, T, FOX_HEADS, FOX_HD), v_new.reshape(B, T, FOX_HEADS, FOX_HD),
            lf[:, :FOX_HEADS].reshape(B, T, FOX_HEADS))


def kernel(x_prompt, x_sample, c_prompt, c_sample, state_gla, cache_k, cache_v, cache_logf, page_table, gla_w_in, gla_w_a2, gla_b_a, gla_norm_g, gla_w_o, kv_w_mod, kv_b_mod, kv_w, fox_b_f, fox_w_q, fox_w_o, ada_w, ada_b, ln_g, ln_b, router_w, router_bias, exp_w_gate, exp_w_up, exp_w_down, sh_w_gate, sh_w_up, sh_w_down):
    Bp, Tp, D = x_prompt.shape
    Bs, Ts, _ = x_sample.shape
    lane_pad = lambda w, n: jnp.pad(w, [(0, 0)] * (w.ndim - 1) + [(0, n - w.shape[-1])])
    W = dict(
        gla_w_in=jnp.concatenate(
            [gla_w_in[..., :GLA_IN_MAIN], lane_pad(gla_w_in[..., GLA_IN_MAIN:], LANES)], axis=-1).astype(BF16),
        gla_w_a2=jnp.pad(gla_w_a2, ((0, 0), (0, LANES - GLA_GATE_RANK), (0, 0))).astype(BF16),
        gla_b_a=gla_b_a.reshape(N_A, 1, GLA_DK),
        gla_norm_g=gla_norm_g.reshape(N_A, 1, GLA_DVH),
        gla_w_o=gla_w_o.astype(BF16),
        kv_w=jnp.concatenate([kv_w[:, :2 * FOX_DIM], lane_pad(kv_w[:, 2 * FOX_DIM:], LANES)], axis=-1).astype(BF16),
        fox_b_f=lane_pad(fox_b_f.reshape(1, FOX_HEADS), LANES),
        fox_w_q=fox_w_q.astype(BF16),
        fox_w_o=fox_w_o.astype(BF16),
        ln_g=ln_g, ln_b=ln_b,
        router_wt=jnp.swapaxes(router_w, 1, 2).astype(BF16),
        router_bias=router_bias,
        exp_w_gate=exp_w_gate, exp_w_up=exp_w_up, exp_w_down=exp_w_down,
        sh_wgu=jnp.concatenate([sh_w_gate, sh_w_up], axis=-1).astype(BF16),
        sh_wd=sh_w_down.astype(BF16),
    )
    c_all = jnp.concatenate([c_prompt, c_sample], axis=0)
    mods = _cond_linear(c_all, ada_w, ada_b, 1536)
    kv_mod = _cond_linear(c_all, kv_w_mod[None], kv_b_mod[None], 1024)[0]

    s0_prompt = jnp.zeros((N_A, Bp, GLA_HEADS, GLA_DKH, GLA_DVH), F32)
    out_p = _trunk(x_prompt, mods[:, :Bp], kv_mod[:Bp], s0_prompt, None, W, Bp, Tp)

    past = dict(page_table=page_table, k=cache_k, v=cache_v, lf_t=jnp.swapaxes(cache_logf, 1, 2))
    out_s = _trunk(x_sample, mods[:, Bp:], kv_mod[Bp:], state_gla, past, W, Bs, Ts)

    y_p, gla_p, k_p, v_p, lf_p = out_p
    y_s, gla_s, k_s, v_s, lf_s = out_s
    return (y_p, y_s, gla_p, gla_s, k_p, v_p, lf_p, k_s, v_s, lf_s)
```

```python
import functools

import jax
import jax.numpy as jnp
from jax import lax
from jax.experimental import pallas as pl
from jax.experimental.pallas import tpu as pltpu

F32 = jnp.float32
BF16 = jnp.bfloat16

D_MODEL = 1024
DEPTH = 4
N_A = DEPTH // 2
PAGE_SIZE = 128
GLA_HEADS = 4
GLA_DK = D_MODEL // 2
GLA_DV = D_MODEL
GLA_DKH = GLA_DK // GLA_HEADS
GLA_DVH = GLA_DV // GLA_HEADS
GLA_GATE_RANK = 16
GLA_GATE_TAU = 16.0
GLA_CHUNK = 64
FOX_HEADS = 8
FOX_HD = D_MODEL // FOX_HEADS
FOX_DIM = FOX_HEADS * FOX_HD
N_EXPERTS = 64
TOP_K = 8
N_GROUPS = 8
GROUP_SIZE = N_EXPERTS // N_GROUPS
TOPK_GROUPS = 4
D_EXPERT = 256
D_SHARED = 256
ROUTED_SCALE = 2.5
DEEPNORM_ALPHA = (2 * DEPTH) ** 0.25
LN_EPS = 1e-5
RMS_EPS = 1e-6

LANES = 128
GLA_IN_MAIN = 2 * GLA_DK + 2 * GLA_DV
GLA_IN_PAD = GLA_IN_MAIN + LANES
KV_PAD = 2 * FOX_DIM + LANES
MASK_NEG = -1e30
VMEM_LIMIT = 52 * 1024 * 1024


def _cparams(*sem):
    return pltpu.CompilerParams(dimension_semantics=sem, vmem_limit_bytes=VMEM_LIMIT)


def _dot(a, b):
    return jnp.dot(a, b, preferred_element_type=F32)


def _dot_nt(a, b):
    return lax.dot_general(a, b, (((1,), (1,)), ((), ())), preferred_element_type=F32)


def _dot_tn(a, b):
    return lax.dot_general(a, b, (((0,), (0,)), ((), ())), preferred_element_type=F32)


def _sigmoid(x):
    return 1.0 / (1.0 + jnp.exp(-x))


def _silu(x):
    return x * _sigmoid(x)


def _log_sigmoid(x):
    return jnp.minimum(x, 0.0) - jnp.log1p(jnp.exp(-jnp.abs(x)))


def _split3(x):
    hi = x.astype(BF16)
    r = x - hi.astype(F32)
    mid = r.astype(BF16)
    lo = (r - mid.astype(F32)).astype(BF16)
    return hi, mid, lo


def _dot_exact_lhs(mat01, x):
    hi, mid, lo = _split3(x)
    return _dot(mat01, hi) + _dot(mat01, mid) + _dot(mat01, lo)


def _dot_exact_rhs(x, mat01):
    hi, mid, lo = _split3(x)
    return _dot(hi, mat01) + _dot(mid, mat01) + _dot(lo, mat01)


def _layer_norm(h, g, b):
    mu = jnp.mean(h, axis=-1, keepdims=True)
    xc = h - mu
    var = jnp.mean(xc * xc, axis=-1, keepdims=True)
    return xc * lax.rsqrt(var + LN_EPS) * g + b


def _row_mod(m, T, tm):
    B, D = m.shape
    if T % tm == 0:
        tiles_per_batch = T // tm
        return m.reshape(B, 1, D), pl.BlockSpec((1, 1, D), lambda i: (i // tiles_per_batch, 0, 0))
    assert tm % T == 0
    e = jnp.repeat(m, T, axis=0).reshape(B * T // tm, tm, D)
    return e, pl.BlockSpec((1, tm, D), lambda i: (i, 0, 0))


def _cond_kernel(c_ref, w_ref, b_ref, o_ref):
    s = _silu(c_ref[...]).astype(BF16)
    o_ref[0] = _dot(s, w_ref[0].astype(BF16)) + b_ref[0]


def _cond_linear(c, w, b, tn):
    M, D = c.shape
    L, _, N = w.shape
    return pl.pallas_call(
        _cond_kernel,
        grid=(L, N // tn),
        in_specs=[
            pl.BlockSpec((M, D), lambda l, j: (0, 0)),
            pl.BlockSpec((1, D, tn), lambda l, j: (l, 0, j)),
            pl.BlockSpec((1, 1, tn), lambda l, j: (l, 0, j)),
        ],
        out_specs=pl.BlockSpec((1, M, tn), lambda l, j: (l, 0, j)),
        out_shape=jax.ShapeDtypeStruct((L, M, N), F32),
        compiler_params=_cparams("parallel", "parallel"),
        name="cond_linear",
    )(c, w, b.reshape(L, 1, N))


def _modproj_kernel(x_ref, sc_ref, sh_ref, w_ref, o_ref):
    u = (x_ref[...] * (1.0 + sc_ref[0]) + sh_ref[0]).astype(BF16)
    o_ref[...] = _dot(u, w_ref[...]).astype(o_ref.dtype)


def _modproj(x, sc, sh, w, T, tm, out_dtype):
    N, D = x.shape
    Dout = w.shape[1]
    sc3, sc_spec = _row_mod(sc, T, tm)
    sh3, sh_spec = _row_mod(sh, T, tm)
    return pl.pallas_call(
        _modproj_kernel,
        grid=(N // tm,),
        in_specs=[
            pl.BlockSpec((tm, D), lambda i: (i, 0)),
            sc_spec,
            sh_spec,
            pl.BlockSpec((D, Dout), lambda i: (0, 0)),
        ],
        out_specs=pl.BlockSpec((tm, Dout), lambda i: (i, 0)),
        out_shape=jax.ShapeDtypeStruct((N, Dout), out_dtype),
        compiler_params=_cparams("parallel"),
        name="modproj",
    )(x, sc3, sh3, w)


def _kv_kernel(x_ref, sc_ref, sh_ref, w_ref, bf_ref, k_ref, v_ref, kb_ref, vb_ref, lf_ref, cum_ref,
               carry_ref, *, T, tm):
    u = (x_ref[...] * (1.0 + sc_ref[0]) + sh_ref[0]).astype(BF16)
    p = _dot(u, w_ref[...])
    k = p[:, :FOX_DIM]
    v = p[:, FOX_DIM:2 * FOX_DIM]
    k_ref[...] = k
    v_ref[...] = v
    kb_ref[...] = k.astype(BF16)
    vb_ref[...] = v.astype(BF16)
    lf = _log_sigmoid(p[:, 2 * FOX_DIM:] + bf_ref[...])
    lf_ref[...] = lf
    if T % tm == 0:
        @pl.when(pl.program_id(0) % (T // tm) == 0)
        def _():
            carry_ref[...] = jnp.zeros_like(carry_ref)

        sb = LANES
        r = lax.broadcasted_iota(jnp.int32, (sb, sb), 0)
        c = lax.broadcasted_iota(jnp.int32, (sb, sb), 1)
        tri = (c <= r).astype(BF16)
        carry = carry_ref[...]
        for s in range(tm // sb):
            cs = _dot_exact_lhs(tri, lf[s * sb:(s + 1) * sb]) + carry
            cum_ref[s * sb:(s + 1) * sb, :] = cs
            carry = cs[sb - 1:sb, :]
        carry_ref[...] = carry
    else:
        r = lax.broadcasted_iota(jnp.int32, (tm, tm), 0)
        c = lax.broadcasted_iota(jnp.int32, (tm, tm), 1)
        tri = ((c <= r) & ((r // T) == (c // T))).astype(BF16)
        cum_ref[...] = _dot_exact_lhs(tri, lf)


def _kv_proj(x, sc, sh, w, bf, T, tm):
    N, D = x.shape
    sc3, sc_spec = _row_mod(sc, T, tm)
    sh3, sh_spec = _row_mod(sh, T, tm)
    row = lambda i: (i, 0)
    return pl.pallas_call(
        functools.partial(_kv_kernel, T=T, tm=tm),
        grid=(N // tm,),
        in_specs=[
            pl.BlockSpec((tm, D), row),
            sc_spec,
            sh_spec,
            pl.BlockSpec((D, KV_PAD), lambda i: (0, 0)),
            pl.BlockSpec((1, LANES), lambda i: (0, 0)),
        ],
        out_specs=[
            pl.BlockSpec((tm, FOX_DIM), row),
            pl.BlockSpec((tm, FOX_DIM), row),
            pl.BlockSpec((tm, FOX_DIM), row),
            pl.BlockSpec((tm, FOX_DIM), row),
            pl.BlockSpec((tm, LANES), row),
            pl.BlockSpec((tm, LANES), row),
        ],
        out_shape=[
            jax.ShapeDtypeStruct((N, FOX_DIM), F32),
            jax.ShapeDtypeStruct((N, FOX_DIM), F32),
            jax.ShapeDtypeStruct((N, FOX_DIM), BF16),
            jax.ShapeDtypeStruct((N, FOX_DIM), BF16),
            jax.ShapeDtypeStruct((N, LANES), F32),
            jax.ShapeDtypeStruct((N, LANES), F32),
        ],
        scratch_shapes=[pltpu.VMEM((1, LANES), F32)],
        compiler_params=_cparams("arbitrary"),
        name="kv_proj",
    )(x, sc3, sh3, w, bf)


def _gla_kernel(q_ref, k_ref, v_ref, g_ref, a_ref, wa_ref, ba_ref, ng_ref, s0_ref, o_ref, st_ref,
                state_ref, *, c, nsub, t_valid):
    t = pl.program_id(2)

    @pl.when(t == 0)
    def _():
        state_ref[...] = s0_ref[0, 0].T

    r = lax.broadcasted_iota(jnp.int32, (c, c), 0)
    cc = lax.broadcasted_iota(jnp.int32, (c, c), 1)
    causal = cc <= r
    tri = causal.astype(BF16)
    scale = GLA_DKH ** -0.5
    for i in range(nsub):
        sl = slice(i * c, (i + 1) * c)
        z = _dot(a_ref[sl, :].astype(BF16), wa_ref[...]) + ba_ref[...]
        la = _log_sigmoid(z) / GLA_GATE_TAU
        if t_valid < c:
            la = jnp.where(lax.broadcasted_iota(jnp.int32, la.shape, 0) < t_valid, la, 0.0)
        bc = _dot_exact_lhs(tri, la)
        b_last = bc[c - 1:c, :]
        k = k_ref[sl, :]
        v = v_ref[sl, :].astype(BF16)
        q_dec = ((q_ref[sl, :] * scale) * jnp.exp(bc)).astype(BF16)
        k_dec = (k * jnp.exp(-bc)).astype(BF16)
        k_end = (k * jnp.exp(b_last - bc)).astype(BF16)
        st = state_ref[...]
        inter = _dot_nt(q_dec, st.astype(BF16))
        att = jnp.where(causal, _dot_nt(q_dec, k_dec), 0.0).astype(BF16)
        o = inter + _dot(att, v)
        state_ref[...] = st * jnp.exp(b_last) + _dot_tn(v, k_end)
        o = o * lax.rsqrt(jnp.mean(o * o, axis=-1, keepdims=True) + RMS_EPS) * ng_ref[...]
        o_ref[sl, :] = (o * _silu(g_ref[sl, :])).astype(o_ref.dtype)

    @pl.when(t == pl.num_programs(2) - 1)
    def _():
        st_ref[0, 0] = state_ref[...].T


def _gla(p, wa, ba, ng, s0, B, Tp, tc, t_valid):
    nt = Tp // tc
    c = GLA_CHUNK
    row = lambda b, h, t: b * nt + t
    kq = GLA_DK // GLA_DKH
    kv = 2 * GLA_DK // GLA_DVH
    kg = kv + GLA_HEADS
    ka = GLA_IN_MAIN // LANES
    return pl.pallas_call(
        functools.partial(_gla_kernel, c=c, nsub=tc // c, t_valid=t_valid),
        grid=(B, GLA_HEADS, nt),
        in_specs=[
            pl.BlockSpec((tc, GLA_DKH), lambda b, h, t: (row(b, h, t), h)),
            pl.BlockSpec((tc, GLA_DKH), lambda b, h, t: (row(b, h, t), kq + h)),
            pl.BlockSpec((tc, GLA_DVH), lambda b, h, t: (row(b, h, t), kv + h)),
            pl.BlockSpec((tc, GLA_DVH), lambda b, h, t: (row(b, h, t), kg + h)),
            pl.BlockSpec((tc, LANES), lambda b, h, t: (row(b, h, t), ka)),
            pl.BlockSpec((LANES, GLA_DKH), lambda b, h, t: (0, h)),
            pl.BlockSpec((1, GLA_DKH), lambda b, h, t: (0, h)),
            pl.BlockSpec((1, GLA_DVH), lambda b, h, t: (0, 0)),
            pl.BlockSpec((1, 1, GLA_DKH, GLA_DVH), lambda b, h, t: (b, h, 0, 0)),
        ],
        out_specs=[
            pl.BlockSpec((tc, GLA_DVH), lambda b, h, t: (row(b, h, t), h)),
            pl.BlockSpec((1, 1, GLA_DKH, GLA_DVH), lambda b, h, t: (b, h, 0, 0)),
        ],
        out_shape=[
            jax.ShapeDtypeStruct((B * Tp, GLA_DV), BF16),
            jax.ShapeDtypeStruct((B, GLA_HEADS, GLA_DKH, GLA_DVH), F32),
        ],
        scratch_shapes=[pltpu.VMEM((GLA_DVH, GLA_DKH), F32)],
        compiler_params=_cparams("parallel", "parallel", "arbitrary"),
        name="gla",
    )(p, p, p, p, p, wa, ba, ng, s0)


def _to_row_tiles(x):
    parts = [x[:, c * LANES:(c + 1) * LANES] for c in range(x.shape[1] // LANES)]
    return jnp.swapaxes(jnp.stack(parts, axis=0), 0, 1)


def _from_row_tiles(x3):
    xt = jnp.swapaxes(x3, 0, 1)
    return jnp.concatenate([xt[c] for c in range(xt.shape[0])], axis=1)


def _proj_res_ln_kernel(a_ref, w_ref, x_ref, gate_ref, lg_ref, lb_ref, sc_ref, sh_ref, xo_ref, uo_ref, u3_ref):
    y = _dot(a_ref[...].astype(BF16), w_ref[...])
    h = DEEPNORM_ALPHA * x_ref[...] + (1.0 + gate_ref[0]) * y
    xn = _layer_norm(h, lg_ref[...], lb_ref[...])
    xo_ref[...] = xn
    u = xn * (1.0 + sc_ref[0]) + sh_ref[0]
    uo_ref[...] = u.astype(uo_ref.dtype)
    u3_ref[...] = _to_row_tiles(u)


def _proj_res_ln(a, w, x, gate, lg, lb, sc, sh, T, tm):
    N, D = x.shape
    g3, g_spec = _row_mod(gate, T, tm)
    sc3, sc_spec = _row_mod(sc, T, tm)
    sh3, sh_spec = _row_mod(sh, T, tm)
    row = lambda i: (i, 0)
    fixed = lambda i: (0, 0)
    return pl.pallas_call(
        _proj_res_ln_kernel,
        grid=(N // tm,),
        in_specs=[
            pl.BlockSpec((tm, a.shape[1]), row),
            pl.BlockSpec(w.shape, fixed),
            pl.BlockSpec((tm, D), row),
            g_spec,
            pl.BlockSpec((1, D), fixed),
            pl.BlockSpec((1, D), fixed),
            sc_spec,
            sh_spec,
        ],
        out_specs=[pl.BlockSpec((tm, D), row), pl.BlockSpec((tm, D), row),
                   pl.BlockSpec((tm, D // LANES, LANES), lambda i: (i, 0, 0))],
        out_shape=[jax.ShapeDtypeStruct((N, D), F32), jax.ShapeDtypeStruct((N, D), BF16),
                   jax.ShapeDtypeStruct((N, D // LANES, LANES), F32)],
        compiler_params=_cparams("parallel"),
        name="proj_res_ln",
    )(a, w, x, g3, lg.reshape(1, D), lb.reshape(1, D), sc3, sh3)


def _router_kernel(u_ref, wr_ref, rb_ref, idx_ref, w_ref, rank_ref, cnt_ref, carry_ref, *, tm):
    @pl.when(pl.program_id(0) == 0)
    def _():
        carry_ref[...] = jnp.zeros_like(carry_ref)

    s = _sigmoid(_dot_nt(wr_ref[...], u_ref[...]))
    sb = s + rb_ref[...]
    neg = -jnp.inf
    sub = lax.broadcasted_iota(jnp.int32, (GROUP_SIZE, tm), 0)
    gs = GROUP_SIZE
    s_g = [s[g * gs:(g + 1) * gs] for g in range(N_GROUPS)]
    sb_g = [sb[g * gs:(g + 1) * gs] for g in range(N_GROUPS)]
    eid_g = [sub + g * gs for g in range(N_GROUPS)]

    score = []
    for g in range(N_GROUPS):
        m1 = jnp.max(sb_g[g], axis=0, keepdims=True)
        first = jnp.min(jnp.where(sb_g[g] == m1, sub, gs), axis=0, keepdims=True)
        m2 = jnp.max(jnp.where(sub == first, neg, sb_g[g]), axis=0, keepdims=True)
        score.append(m1 + m2)

    chosen = [jnp.zeros((1, tm), jnp.int32) for _ in range(N_GROUPS)]
    for _ in range(TOPK_GROUPS):
        m = score[0]
        for g in range(1, N_GROUPS):
            m = jnp.maximum(m, score[g])
        gi = jnp.full((1, tm), N_GROUPS, jnp.int32)
        for g in range(N_GROUPS - 1, -1, -1):
            gi = jnp.where(score[g] == m, g, gi)
        for g in range(N_GROUPS):
            hit = gi == g
            chosen[g] = jnp.where(hit, 1, chosen[g])
            score[g] = jnp.where(hit, neg, score[g])
    cand = [jnp.where(chosen[g] > 0, sb_g[g], neg) for g in range(N_GROUPS)]

    onehot = [jnp.zeros((gs, tm), F32) for _ in range(N_GROUPS)]
    idxs, ws = [], []
    for _ in range(TOP_K):
        mx = cand[0]
        for g in range(1, N_GROUPS):
            mx = jnp.maximum(mx, cand[g])
        m = jnp.max(mx, axis=0, keepdims=True)
        lo = jnp.where(cand[0] == m, eid_g[0], N_EXPERTS)
        for g in range(1, N_GROUPS):
            lo = jnp.minimum(lo, jnp.where(cand[g] == m, eid_g[g], N_EXPERTS))
        idx = jnp.min(lo, axis=0, keepdims=True)
        wsum = jnp.zeros((gs, tm), F32)
        for g in range(N_GROUPS):
            hit = eid_g[g] == idx
            wsum = wsum + jnp.where(hit, s_g[g], 0.0)
            cand[g] = jnp.where(hit, neg, cand[g])
            onehot[g] = jnp.where(hit, 1.0, onehot[g])
        idxs.append(idx)
        ws.append(jnp.sum(wsum, axis=0, keepdims=True))
    total = ws[0]
    for k in range(1, TOP_K):
        total = total + ws[k]
    for k in range(TOP_K):
        idx_ref[k:k + 1, :] = idxs[k]
        w_ref[k:k + 1, :] = ws[k] / total * ROUTED_SCALE

    oh = jnp.concatenate(onehot, axis=0)
    r = lax.broadcasted_iota(jnp.int32, (tm, tm), 0)
    c = lax.broadcasted_iota(jnp.int32, (tm, tm), 1)
    before = (r < c).astype(BF16)
    pre = _dot(oh.astype(BF16), before) + carry_ref[...]
    for k in range(TOP_K):
        acc = jnp.zeros((gs, tm), F32)
        for g in range(N_GROUPS):
            acc = acc + jnp.where(eid_g[g] == idxs[k], pre[g * gs:(g + 1) * gs], 0.0)
        rank_ref[k:k + 1, :] = jnp.sum(acc, axis=0, keepdims=True).astype(jnp.int32)
    carry_ref[...] = carry_ref[...] + jnp.sum(oh, axis=1, keepdims=True)
    cnt_ref[...] = jnp.broadcast_to(carry_ref[...], cnt_ref.shape).astype(jnp.int32)


def _router(u, wr_t, rb, tm):
    N, D = u.shape
    col = lambda i: (0, i)
    fixed = lambda i: (0, 0)
    return pl.pallas_call(
        functools.partial(_router_kernel, tm=tm),
        grid=(N // tm,),
        in_specs=[
            pl.BlockSpec((tm, D), lambda i: (i, 0)),
            pl.BlockSpec((N_EXPERTS, D), fixed),
            pl.BlockSpec((N_EXPERTS, 1), fixed),
        ],
        out_specs=[
            pl.BlockSpec((TOP_K, tm), col),
            pl.BlockSpec((TOP_K, tm), col),
            pl.BlockSpec((TOP_K, tm), col),
            pl.BlockSpec((N_EXPERTS, LANES), fixed),
        ],
        out_shape=[
            jax.ShapeDtypeStruct((TOP_K, N), jnp.int32),
            jax.ShapeDtypeStruct((TOP_K, N), F32),
            jax.ShapeDtypeStruct((TOP_K, N), jnp.int32),
            jax.ShapeDtypeStruct((N_EXPERTS, LANES), jnp.int32),
        ],
        scratch_shapes=[pltpu.VMEM((N_EXPERTS, 1), F32)],
        compiler_params=_cparams("arbitrary"),
        name="router",
    )(u, wr_t, rb)


def _experts_kernel(be_ref, live_ref, x_ref, wg_ref, wu_ref, wd_ref, o_ref, wgb_ref, wub_ref, wdb_ref):
    i = pl.program_id(0)
    e = be_ref[i]
    prev = be_ref[jnp.maximum(i - 1, 0)]

    @pl.when((i == 0) | (e != prev))
    def _():
        wgb_ref[...] = wg_ref[...].astype(BF16)
        wub_ref[...] = wu_ref[...].astype(BF16)
        wdb_ref[...] = wd_ref[...].astype(BF16)

    @pl.when(live_ref[i] > 0)
    def _():
        x = _from_row_tiles(x_ref[...]).astype(BF16)
        h = (_silu(_dot(x, wgb_ref[...])) * _dot(x, wub_ref[...])).astype(BF16)
        o_ref[...] = _to_row_tiles(_dot(h, wdb_ref[...]))

    @pl.when(live_ref[i] == 0)
    def _():
        o_ref[...] = jnp.zeros_like(o_ref)


EXPERT_ROWS = 512


def _experts(xs3, block_e, live, wg, wu, wd, layer, bm):
    R, S, _ = xs3.shape
    D = S * LANES
    return pl.pallas_call(
        _experts_kernel,
        grid_spec=pltpu.PrefetchScalarGridSpec(
            num_scalar_prefetch=2,
            grid=(R // bm,),
            in_specs=[
                pl.BlockSpec((bm, S, LANES), lambda i, be, lv: (i * lv[i], 0, 0)),
                pl.BlockSpec((None, None, D, D_EXPERT), lambda i, be, lv: (layer, be[i], 0, 0)),
                pl.BlockSpec((None, None, D, D_EXPERT), lambda i, be, lv: (layer, be[i], 0, 0)),
                pl.BlockSpec((None, None, D_EXPERT, D), lambda i, be, lv: (layer, be[i], 0, 0)),
            ],
            out_specs=pl.BlockSpec((bm, S, LANES), lambda i, be, lv: (i, 0, 0)),
            scratch_shapes=[
                pltpu.VMEM((D, D_EXPERT), BF16),
                pltpu.VMEM((D, D_EXPERT), BF16),
                pltpu.VMEM((D_EXPERT, D), BF16),
            ],
        ),
        out_shape=jax.ShapeDtypeStruct((R, S, LANES), F32),
        compiler_params=_cparams("arbitrary"),
        name="experts",
    )(block_e, live, xs3, wg, wu, wd)


ISSUE_TOKENS = 2


ZERO_ROWS = 128


def _dispatch_kernel(pad_end_ref, padded_ref, dest_ref, u_ref, *rest, tmd, bm, first):
    if first:
        xs_hbm, sem, zero_ref, zsem = rest

        @pl.when(pl.program_id(0) == 0)
        def _():
            zero_ref[...] = jnp.zeros_like(zero_ref)

            def clear(e, carry):
                @pl.when(padded_ref[e] > 0)
                def _():
                    for s in range(bm // ZERO_ROWS):
                        rows = pl.ds(pad_end_ref[e] - bm + s * ZERO_ROWS, ZERO_ROWS)
                        pltpu.make_async_copy(zero_ref, xs_hbm.at[rows], zsem).start()
                return carry

            def cleared(e, carry):
                @pl.when(padded_ref[e] > 0)
                def _():
                    for s in range(bm // ZERO_ROWS):
                        pltpu.make_async_copy(zero_ref, xs_hbm.at[pl.ds(0, ZERO_ROWS)], zsem).wait()
                return carry

            lax.fori_loop(0, N_EXPERTS, clear, 0)
            lax.fori_loop(0, N_EXPERTS, cleared, 0)
    else:
        _, xs_hbm, sem = rest

    def issue(t2, carry):
        base = t2 * (ISSUE_TOKENS * TOP_K)
        rows = [dest_ref[base + r] for r in range(ISSUE_TOKENS * TOP_K)]
        for r, row in enumerate(rows):
            pltpu.make_async_copy(u_ref.at[t2 * ISSUE_TOKENS + r // TOP_K], xs_hbm.at[row], sem).start(priority=r % 2)
        return carry

    lax.fori_loop(0, tmd // ISSUE_TOKENS, issue, 0)
    for k in range(TOP_K):
        pltpu.make_async_copy(u_ref, xs_hbm.at[pl.ds(0, tmd)], sem).wait()


def _dispatch(u3, dest, xs_prev, pad_end, padded, R, tmd, bm):
    N, S, _ = u3.shape
    first = xs_prev is None
    in_specs = [
        pl.BlockSpec((tmd * TOP_K,), lambda i, pe, pd: (i,), memory_space=pltpu.SMEM),
        pl.BlockSpec((tmd, S, LANES), lambda i, pe, pd: (i, 0, 0)),
    ]
    scratch = [pltpu.SemaphoreType.DMA(())]
    if first:
        scratch += [pltpu.VMEM((ZERO_ROWS, S, LANES), F32), pltpu.SemaphoreType.DMA(())]
    else:
        in_specs.append(pl.BlockSpec(memory_space=pl.ANY))
    return pl.pallas_call(
        functools.partial(_dispatch_kernel, tmd=tmd, bm=bm, first=first),
        grid_spec=pltpu.PrefetchScalarGridSpec(
            num_scalar_prefetch=2,
            grid=(N // tmd,),
            in_specs=in_specs,
            out_specs=pl.BlockSpec(memory_space=pl.ANY),
            scratch_shapes=scratch,
        ),
        out_shape=jax.ShapeDtypeStruct((R, S, LANES), F32),
        input_output_aliases={} if first else {4: 0},
        compiler_params=_cparams("arbitrary"),
        name="moe_dispatch",
    )(pad_end, padded, dest, u3, *(() if first else (xs_prev,)))


def _combine_kernel(dest_ref, w_ref, y_hbm, o_ref, buf_ref, sem, *, tmc):
    def issue(t2, carry):
        first = t2 * (ISSUE_TOKENS * TOP_K)
        rows = [dest_ref[first + r] for r in range(ISSUE_TOKENS * TOP_K)]
        for r, row in enumerate(rows):
            pltpu.make_async_copy(y_hbm.at[row], buf_ref.at[first + r], sem).start(priority=r % 2)
        return carry

    lax.fori_loop(0, tmc // ISSUE_TOKENS, issue, 0)
    pltpu.make_async_copy(y_hbm.at[pl.ds(0, tmc * TOP_K)], buf_ref, sem).wait()

    def reduce(t, carry):
        acc = w_ref[t * TOP_K] * buf_ref[t * TOP_K]
        for k in range(1, TOP_K):
            acc = acc + w_ref[t * TOP_K + k] * buf_ref[t * TOP_K + k]
        o_ref[t] = acc
        return carry

    lax.fori_loop(0, tmc, reduce, 0)


def _combine(y3, dest, w, N, tmc):
    _, S, _ = y3.shape
    return pl.pallas_call(
        functools.partial(_combine_kernel, tmc=tmc),
        grid=(N // tmc,),
        in_specs=[
            pl.BlockSpec((tmc * TOP_K,), lambda i: (i,), memory_space=pltpu.SMEM),
            pl.BlockSpec((tmc * TOP_K,), lambda i: (i,), memory_space=pltpu.SMEM),
            pl.BlockSpec(memory_space=pl.ANY),
        ],
        out_specs=pl.BlockSpec((tmc, S, LANES), lambda i: (i, 0, 0)),
        out_shape=jax.ShapeDtypeStruct((N, S, LANES), F32),
        scratch_shapes=[pltpu.VMEM((tmc * TOP_K, S, LANES), F32), pltpu.SemaphoreType.DMA(())],
        compiler_params=_cparams("arbitrary"),
        name="moe_combine",
    )(dest, w, y3)


def _moe_final_kernel(u_ref, x_ref, r_ref, wgu_ref, wd_ref, gate_ref, lg_ref, lb_ref, xo_ref):
    gu = _dot(u_ref[...], wgu_ref[...])
    h = (_silu(gu[:, :D_SHARED]) * gu[:, D_SHARED:]).astype(BF16)
    y = _from_row_tiles(r_ref[...]) + _dot(h, wd_ref[...])
    hres = DEEPNORM_ALPHA * x_ref[...] + (1.0 + gate_ref[0]) * y
    xo_ref[...] = _layer_norm(hres, lg_ref[...], lb_ref[...])


def _moe_final(u, x, routed, wgu, wd, gate, lg, lb, T, tm):
    N, D = x.shape
    g3, g_spec = _row_mod(gate, T, tm)
    row = lambda i: (i, 0)
    fixed = lambda i: (0, 0)
    return pl.pallas_call(
        _moe_final_kernel,
        grid=(N // tm,),
        in_specs=[
            pl.BlockSpec((tm, D), row),
            pl.BlockSpec((tm, D), row),
            pl.BlockSpec((tm, D // LANES, LANES), lambda i: (i, 0, 0)),
            pl.BlockSpec(wgu.shape, fixed),
            pl.BlockSpec(wd.shape, fixed),
            g_spec,
            pl.BlockSpec((1, D), fixed),
            pl.BlockSpec((1, D), fixed),
        ],
        out_specs=pl.BlockSpec((tm, D), row),
        out_shape=jax.ShapeDtypeStruct((N, D), F32),
        compiler_params=_cparams("parallel"),
        name="moe_final",
    )(u, x, routed, wgu, wd, g3, lg.reshape(1, D), lb.reshape(1, D))


def _fox_kernel(q_ref, k_ref, v_ref, fq_ref, fk_ref, o_ref, m_ref, l_ref, acc_ref, *, bq):
    i = pl.program_id(1)
    j = pl.program_id(2)

    @pl.when(j == 0)
    def _():
        m_ref[...] = jnp.full_like(m_ref, MASK_NEG)
        l_ref[...] = jnp.zeros_like(l_ref)
        acc_ref[...] = jnp.zeros_like(acc_ref)

    @pl.when(j <= i)
    def _():
        kpos = j * bq + lax.broadcasted_iota(jnp.int32, (bq, bq), 0)
        qpos = i * bq + lax.broadcasted_iota(jnp.int32, (bq, bq), 1)
        visible = kpos <= qpos
        scale = FOX_HD ** -0.5
        for h in range(FOX_HEADS):
            hs = slice(h * FOX_HD, (h + 1) * FOX_HD)
            s = _dot_nt(k_ref[:, hs], q_ref[:, hs]) * scale + fq_ref[h:h + 1, :] - fk_ref[:, h:h + 1]
            s = jnp.where(visible, s, MASK_NEG)
            m_prev = m_ref[h:h + 1, :]
            m_new = jnp.maximum(m_prev, jnp.max(s, axis=0, keepdims=True))
            alpha = jnp.exp(m_prev - m_new)
            p = jnp.exp(s - m_new)
            l_ref[h:h + 1, :] = alpha * l_ref[h:h + 1, :] + jnp.sum(p, axis=0, keepdims=True)
            acc_ref[hs, :] = alpha * acc_ref[hs, :] + _dot_tn(v_ref[:, hs], p.astype(BF16))
            m_ref[h:h + 1, :] = m_new

    @pl.when(j == pl.num_programs(2) - 1)
    def _():
        for h in range(FOX_HEADS):
            hs = slice(h * FOX_HD, (h + 1) * FOX_HD)
            o_ref[:, hs] = (acc_ref[hs, :] / l_ref[h:h + 1, :]).T.astype(o_ref.dtype)


def _fox_attention(q, kb, vb, cum, cum_t, B, T, bq):
    nq = T // bq
    kv_map = lambda b, i, j: (b * nq + jnp.minimum(j, i), 0)
    return pl.pallas_call(
        functools.partial(_fox_kernel, bq=bq),
        grid=(B, nq, nq),
        in_specs=[
            pl.BlockSpec((bq, FOX_DIM), lambda b, i, j: (b * nq + i, 0)),
            pl.BlockSpec((bq, FOX_DIM), kv_map),
            pl.BlockSpec((bq, FOX_DIM), kv_map),
            pl.BlockSpec((FOX_HEADS, bq), lambda b, i, j: (b, i)),
            pl.BlockSpec((bq, LANES), kv_map),
        ],
        out_specs=pl.BlockSpec((bq, FOX_DIM), lambda b, i, j: (b * nq + i, 0)),
        out_shape=jax.ShapeDtypeStruct((B * T, FOX_DIM), BF16),
        scratch_shapes=[
            pltpu.VMEM((FOX_HEADS, bq), F32),
            pltpu.VMEM((FOX_HEADS, bq), F32),
            pltpu.VMEM((FOX_DIM, bq), F32),
        ],
        compiler_params=_cparams("parallel", "parallel", "arbitrary"),
        name="fox_attention",
    )(q, kb, vb, cum_t, cum)


def _decode_kernel(pt_ref, q_ref, kn_ref, vn_ref, fq_ref, fkn_ref, *rest, tq, npg):
    kc, vc, lfc = rest[:npg], rest[npg:2 * npg], rest[2 * npg:3 * npg]
    o_ref, m_ref, l_ref, acc_ref, carry_ref = rest[3 * npg:]
    j = pl.program_id(1)
    H = FOX_HEADS
    scale = FOX_HD ** -0.5
    q = [q_ref[:, h * FOX_HD:(h + 1) * FOX_HD].astype(BF16) for h in range(H)]
    fq = jnp.concatenate([fq_ref[:, h:h + 1] for h in range(H)], axis=0)

    def per_head_rows(x):
        return jnp.concatenate([jnp.broadcast_to(x[h:h + 1, :], (tq, x.shape[1])) for h in range(H)], axis=0)

    def attend(s, values, state):
        m_prev, l_prev, acc_prev = state
        m_new = jnp.maximum(m_prev, jnp.max(s, axis=1, keepdims=True))
        alpha = jnp.exp(m_prev - m_new)
        p = jnp.exp(s - m_new)
        l_new = alpha * l_prev + jnp.sum(p, axis=1, keepdims=True)
        pv = []
        for h in range(H):
            o = None
            for i, value in enumerate(values):
                ph = p[h * tq:(h + 1) * tq, i * PAGE_SIZE:(i + 1) * PAGE_SIZE].astype(BF16)
                t = _dot(ph, value(h))
                o = t if o is None else o + t
            pv.append(o)
        return m_new, l_new, alpha * acc_prev + jnp.concatenate(pv, axis=0)

    @pl.when(j == 0)
    def _():
        carry_ref[...] = jnp.zeros_like(carry_ref)
        r = lax.broadcasted_iota(jnp.int32, (H * tq, PAGE_SIZE), 0) % tq
        c = lax.broadcasted_iota(jnp.int32, (H * tq, PAGE_SIZE), 1)
        s = jnp.concatenate(
            [_dot_nt(q[h], kn_ref[0, :, h * FOX_HD:(h + 1) * FOX_HD]) for h in range(H)], axis=0)
        s = s * scale + fq - per_head_rows(fkn_ref[0])
        s = jnp.where(c <= r, s, MASK_NEG)
        state = (jnp.full((H * tq, 1), MASK_NEG, F32), jnp.zeros((H * tq, 1), F32),
                 jnp.zeros((H * tq, FOX_HD), F32))
        m_ref[...], l_ref[...], acc_ref[...] = attend(
            s, [lambda h: vn_ref[0, :, h * FOX_HD:(h + 1) * FOX_HD]], state)

    r = lax.broadcasted_iota(jnp.int32, (PAGE_SIZE, PAGE_SIZE), 0)
    c = lax.broadcasted_iota(jnp.int32, (PAGE_SIZE, PAGE_SIZE), 1)
    after = (r > c).astype(BF16)
    carry = carry_ref[...]
    blocks = []
    for i in range(npg):
        lf = lfc[i][...]
        suffix = _dot_exact_rhs(lf, after) + carry
        carry = carry + jnp.sum(lf, axis=1, keepdims=True)
        kt = jnp.swapaxes(kc[i][...], 0, 1)
        sk = jnp.concatenate([_dot_nt(q[h], kt[h].astype(BF16)) for h in range(H)], axis=0)
        blocks.append(sk * scale + per_head_rows(suffix))
    carry_ref[...] = carry
    s = jnp.concatenate(blocks, axis=1) + fq
    vts = [jnp.swapaxes(vc[i][...], 0, 1) for i in range(npg)]
    values = [lambda h, i=i: vts[i][h].astype(BF16) for i in range(npg)]
    m_new, l_new, acc_new = attend(s, values, (m_ref[...], l_ref[...], acc_ref[...]))
    m_ref[...] = m_new
    l_ref[...] = l_new
    acc_ref[...] = acc_new

    @pl.when(j == pl.num_programs(1) - 1)
    def _():
        out = acc_new / l_new
        for h in range(H):
            o_ref[:, h * FOX_HD:(h + 1) * FOX_HD] = out[h * tq:(h + 1) * tq, :]


DECODE_PAGES_PER_STEP = 8


def _decode_attention(page_table, q, k_new, v_new, cum, cum_new_t, cache_k, cache_v, cache_lf_t, B, tq):
    n_pages = page_table.shape[1]
    npg = DECODE_PAGES_PER_STEP
    assert n_pages % npg == 0

    def page(i, nd):
        return lambda b, j, pt: (pt[b * n_pages + (n_pages - 1 - (j * npg + i))],) + (0,) * nd

    kv_specs = [pl.BlockSpec((None, PAGE_SIZE, FOX_HEADS, FOX_HD), page(i, 3)) for i in range(npg)]
    lf_specs = [pl.BlockSpec((None, FOX_HEADS, PAGE_SIZE), page(i, 2)) for i in range(npg)]
    return pl.pallas_call(
        functools.partial(_decode_kernel, tq=tq, npg=npg),
        grid_spec=pltpu.PrefetchScalarGridSpec(
            num_scalar_prefetch=1,
            grid=(B, n_pages // npg),
            in_specs=[
                pl.BlockSpec((tq, FOX_DIM), lambda b, j, pt: (b, 0)),
                pl.BlockSpec((1, PAGE_SIZE, FOX_DIM), lambda b, j, pt: (b, 0, 0)),
                pl.BlockSpec((1, PAGE_SIZE, FOX_DIM), lambda b, j, pt: (b, 0, 0)),
                pl.BlockSpec((tq, LANES), lambda b, j, pt: (b, 0)),
                pl.BlockSpec((1, FOX_HEADS, PAGE_SIZE), lambda b, j, pt: (b, 0, 0)),
            ] + kv_specs + kv_specs + lf_specs,
            out_specs=pl.BlockSpec((tq, FOX_DIM), lambda b, j, pt: (b, 0)),
            scratch_shapes=[
                pltpu.VMEM((FOX_HEADS * tq, 1), F32),
                pltpu.VMEM((FOX_HEADS * tq, 1), F32),
                pltpu.VMEM((FOX_HEADS * tq, FOX_HD), F32),
                pltpu.VMEM((FOX_HEADS, 1), F32),
            ],
        ),
        out_shape=jax.ShapeDtypeStruct((B * tq, FOX_DIM), F32),
        compiler_params=_cparams("parallel", "arbitrary"),
        name="decode_attention",
    )(page_table.reshape(-1), q, k_new, v_new, cum, cum_new_t, *([cache_k] * npg), *([cache_v] * npg),
      *([cache_lf_t] * npg))


def _moe(groups, W, l):
    bm = EXPERT_ROWS
    routes = [_router(g["u"], W["router_wt"][l], W["router_bias"][l].reshape(N_EXPERTS, 1), g["tm"])
              for g in groups]
    counts = [r[3][:, 0] for r in routes]
    total = sum(counts)
    padded = (total + bm - 1) // bm * bm
    pad_end = jnp.cumsum(padded)
    n_blocks = -(-sum(g["N"] for g in groups) * TOP_K // bm) + N_EXPERTS
    starts = jnp.arange(n_blocks, dtype=jnp.int32) * bm
    block_e = jnp.minimum(jnp.sum(pad_end[None, :] <= starts[:, None], axis=1), N_EXPERTS - 1).astype(jnp.int32)
    live = (starts < pad_end[-1]).astype(jnp.int32)
    offset = pad_end - padded
    xs3, dests = None, []
    for g, (idx_t, _, rank_t, _), cnt in zip(groups, routes, counts):
        first = jnp.sum(jnp.where(idx_t[:, :, None] == jnp.arange(N_EXPERTS), offset, 0), axis=-1)
        dests.append((first + rank_t).T.reshape(-1))
        xs3 = _dispatch(g["u3"], dests[-1], xs3, pad_end, padded, n_blocks * bm, min(256, g["N"]), bm)
        offset = offset + cnt
    y3 = _experts(xs3, block_e, live, W["exp_w_gate"], W["exp_w_up"], W["exp_w_down"], l, bm)
    for g, (_, w_t, _, _), dest in zip(groups, routes, dests):
        routed3 = _combine(y3, dest, w_t.T.reshape(-1), g["N"], min(128, g["N"]))
        g["x"] = _moe_final(g["u"], g["x"], routed3, W["sh_wgu"][l], W["sh_wd"][l], g["g_m"], W["ln_g"][l, 1],
                            W["ln_b"][l, 1], g["T"], g["tm"])


def _mixer(g, W, l):
    B, T, N, tm, past = g["B"], g["T"], g["N"], g["tm"], g["past"]
    sh_a, sc_a, g_a, sh_m, sc_m, g_m = jnp.split(g["mods"][l], 6, axis=-1)
    if l < N_A:
        p = _modproj(g["x"], sc_a, sh_a, W["gla_w_in"][l], T, tm, F32)
        if T >= GLA_CHUNK:
            Tp, tc = T, min(512, T)
        else:
            Tp = tc = GLA_CHUNK
            p = jnp.pad(p.reshape(B, T, -1), ((0, 0), (0, Tp - T), (0, 0))).reshape(B * Tp, -1)
        o, s_T = _gla(p, W["gla_w_a2"][l], W["gla_b_a"][l], W["gla_norm_g"][l], g["s0"][l], B, Tp, tc, T)
        if Tp != T:
            o = o.reshape(B, Tp, -1)[:, :T].reshape(N, -1)
        g["gla"].append(s_T)
        w_o = W["gla_w_o"][l]
    else:
        q = _modproj(g["x"], sc_a, sh_a, W["fox_w_q"][l - N_A], T, tm, BF16 if past is None else F32)
        if past is None:
            o = _fox_attention(q, g["kb"], g["vb"], g["cum"], g["cum_t"], B, T, min(256, T))
        else:
            o = _decode_attention(past["page_table"], q, g["kb"], g["vb"], g["cum"], g["cum_t"], past["k"],
                                  past["v"], past["lf_t"], B, T)
        w_o = W["fox_w_o"][l - N_A]
    g["x"], g["u"], g["u3"] = _proj_res_ln(o, w_o, g["x"], g_a, W["ln_g"][l, 0], W["ln_b"][l, 0], sc_m, sh_m, T, tm)
    g["g_m"] = g_m


def _shared_kv(g, W):
    B, T = g["B"], g["T"]
    sh_kv, sc_kv = jnp.split(g["kv_mod"], 2, axis=-1)
    g["k"], g["v"], kb, vb, g["lf"], g["cum"] = _kv_proj(g["x"], sc_kv, sh_kv, W["kv_w"], W["fox_b_f"], T, g["tm"])
    cum_h = g["cum"][:, :FOX_HEADS].reshape(B, T, FOX_HEADS)
    if g["past"] is None:
        g["cum_t"] = cum_h.transpose(0, 2, 1).reshape(B * FOX_HEADS, T)
        g["kb"], g["vb"] = kb, vb
    else:
        rows = ((0, 0), (0, PAGE_SIZE - T), (0, 0))
        g["cum_t"] = jnp.pad(cum_h, rows).transpose(0, 2, 1)
        g["kb"] = jnp.pad(kb.reshape(B, T, FOX_DIM), rows)
        g["vb"] = jnp.pad(vb.reshape(B, T, FOX_DIM), rows)


def _group(x, mods, kv_mod, s0, past):
    B, T, D = x.shape
    return dict(x=x.reshape(B * T, D), B=B, T=T, N=B * T, tm=min(512, B * T), mods=mods, kv_mod=kv_mod, s0=s0,
                past=past, gla=[])


def _group_outputs(g):
    B, T = g["B"], g["T"]
    return (g["x"].reshape(B, T, D_MODEL), jnp.stack(g["gla"]),
            g["k"].reshape(B, T, FOX_HEADS, FOX_HD), g["v"].reshape(B, T, FOX_HEADS, FOX_HD),
            g["lf"][:, :FOX_HEADS].reshape(B, T, FOX_HEADS))


def kernel(x_prompt, x_sample, c_prompt, c_sample, state_gla, cache_k, cache_v, cache_logf, page_table, gla_w_in, gla_w_a2, gla_b_a, gla_norm_g, gla_w_o, kv_w_mod, kv_b_mod, kv_w, fox_b_f, fox_w_q, fox_w_o, ada_w, ada_b, ln_g, ln_b, router_w, router_bias, exp_w_gate, exp_w_up, exp_w_down, sh_w_gate, sh_w_up, sh_w_down):
    Bp, Tp, D = x_prompt.shape
    Bs, Ts, _ = x_sample.shape
    lane_pad = lambda w, n: jnp.pad(w, [(0, 0)] * (w.ndim - 1) + [(0, n - w.shape[-1])])
    W = dict(
        gla_w_in=jnp.concatenate(
            [gla_w_in[..., :GLA_IN_MAIN], lane_pad(gla_w_in[..., GLA_IN_MAIN:], LANES)], axis=-1).astype(BF16),
        gla_w_a2=jnp.pad(gla_w_a2, ((0, 0), (0, LANES - GLA_GATE_RANK), (0, 0))).astype(BF16),
        gla_b_a=gla_b_a.reshape(N_A, 1, GLA_DK),
        gla_norm_g=gla_norm_g.reshape(N_A, 1, GLA_DVH),
        gla_w_o=gla_w_o.astype(BF16),
        kv_w=jnp.concatenate([kv_w[:, :2 * FOX_DIM], lane_pad(kv_w[:, 2 * FOX_DIM:], LANES)], axis=-1).astype(BF16),
        fox_b_f=lane_pad(fox_b_f.reshape(1, FOX_HEADS), LANES),
        fox_w_q=fox_w_q.astype(BF16),
        fox_w_o=fox_w_o.astype(BF16),
        ln_g=ln_g, ln_b=ln_b,
        router_wt=jnp.swapaxes(router_w, 1, 2).astype(BF16),
        router_bias=router_bias,
        exp_w_gate=exp_w_gate, exp_w_up=exp_w_up, exp_w_down=exp_w_down,
        sh_wgu=jnp.concatenate([sh_w_gate, sh_w_up], axis=-1).astype(BF16),
        sh_wd=sh_w_down.astype(BF16),
    )
    c_all = jnp.concatenate([c_prompt, c_sample], axis=0)
    mods = _cond_linear(c_all, ada_w, ada_b, 1536)
    kv_mod = _cond_linear(c_all, kv_w_mod[None], kv_b_mod[None], 1024)[0]

    s0_prompt = jnp.zeros((N_A, Bp, GLA_HEADS, GLA_DKH, GLA_DVH), F32)
    past = dict(page_table=page_table, k=cache_k, v=cache_v, lf_t=jnp.swapaxes(cache_logf, 1, 2))
    groups = [_group(x_prompt, mods[:, :Bp], kv_mod[:Bp], s0_prompt, None),
              _group(x_sample, mods[:, Bp:], kv_mod[Bp:], state_gla, past)]
    for l in range(DEPTH):
        for g in groups:
            _mixer(g, W, l)
        _moe(groups, W, l)
        if l == N_A - 1:
            for g in groups:
                _shared_kv(g, W)
    y_p, gla_p, k_p, v_p, lf_p = _group_outputs(groups[0])
    y_s, gla_s, k_s, v_s, lf_s = _group_outputs(groups[1])
    return (y_p, y_s, gla_p, gla_s, k_p, v_p, lf_p, k_s, v_s, lf_s)
```

```python
import functools

import jax
import jax.numpy as jnp
from jax import lax
from jax.experimental import pallas as pl
from jax.experimental.pallas import tpu as pltpu

F32 = jnp.float32
BF16 = jnp.bfloat16

D_MODEL = 1024
DEPTH = 4
N_A = DEPTH // 2
PAGE_SIZE = 128
GLA_HEADS = 4
GLA_DK = D_MODEL // 2
GLA_DV = D_MODEL
GLA_DKH = GLA_DK // GLA_HEADS
GLA_DVH = GLA_DV // GLA_HEADS
GLA_GATE_RANK = 16
GLA_GATE_TAU = 16.0
GLA_CHUNK = 64
FOX_HEADS = 8
FOX_HD = D_MODEL // FOX_HEADS
FOX_DIM = FOX_HEADS * FOX_HD
N_EXPERTS = 64
TOP_K = 8
N_GROUPS = 8
GROUP_SIZE = N_EXPERTS // N_GROUPS
TOPK_GROUPS = 4
D_EXPERT = 256
D_SHARED = 256
ROUTED_SCALE = 2.5
DEEPNORM_ALPHA = (2 * DEPTH) ** 0.25
LN_EPS = 1e-5
RMS_EPS = 1e-6

LANES = 128
GLA_IN_MAIN = 2 * GLA_DK + 2 * GLA_DV
GLA_IN_PAD = GLA_IN_MAIN + LANES
KV_PAD = 2 * FOX_DIM + LANES
MASK_NEG = -1e30
VMEM_LIMIT = 52 * 1024 * 1024


def _cparams(*sem):
    return pltpu.CompilerParams(dimension_semantics=sem, vmem_limit_bytes=VMEM_LIMIT)


def _dot(a, b):
    return jnp.dot(a, b, preferred_element_type=F32)


def _dot_nt(a, b):
    return lax.dot_general(a, b, (((1,), (1,)), ((), ())), preferred_element_type=F32)


def _dot_tn(a, b):
    return lax.dot_general(a, b, (((0,), (0,)), ((), ())), preferred_element_type=F32)


def _sigmoid(x):
    return 1.0 / (1.0 + jnp.exp(-x))


def _silu(x):
    return x * _sigmoid(x)


def _log_sigmoid(x):
    return jnp.minimum(x, 0.0) - jnp.log1p(jnp.exp(-jnp.abs(x)))


def _split3(x):
    hi = x.astype(BF16)
    r = x - hi.astype(F32)
    mid = r.astype(BF16)
    lo = (r - mid.astype(F32)).astype(BF16)
    return hi, mid, lo


def _dot_exact_lhs(mat01, x):
    hi, mid, lo = _split3(x)
    return _dot(mat01, hi) + _dot(mat01, mid) + _dot(mat01, lo)


def _dot_exact_rhs(x, mat01):
    hi, mid, lo = _split3(x)
    return _dot(hi, mat01) + _dot(mid, mat01) + _dot(lo, mat01)


def _layer_norm(h, g, b):
    mu = jnp.mean(h, axis=-1, keepdims=True)
    xc = h - mu
    var = jnp.mean(xc * xc, axis=-1, keepdims=True)
    return xc * lax.rsqrt(var + LN_EPS) * g + b


def _row_mod(m, T, tm):
    B, D = m.shape
    if T % tm == 0:
        tiles_per_batch = T // tm
        return m.reshape(B, 1, D), pl.BlockSpec((1, 1, D), lambda i: (i // tiles_per_batch, 0, 0))
    assert tm % T == 0
    e = jnp.repeat(m, T, axis=0).reshape(B * T // tm, tm, D)
    return e, pl.BlockSpec((1, tm, D), lambda i: (i, 0, 0))


def _cond_kernel(c_ref, w_ref, b_ref, o_ref):
    s = _silu(c_ref[...]).astype(BF16)
    o_ref[0] = _dot(s, w_ref[0].astype(BF16)) + b_ref[0]


def _cond_linear(c, w, b, tn):
    M, D = c.shape
    L, _, N = w.shape
    return pl.pallas_call(
        _cond_kernel,
        grid=(L, N // tn),
        in_specs=[
            pl.BlockSpec((M, D), lambda l, j: (0, 0)),
            pl.BlockSpec((1, D, tn), lambda l, j: (l, 0, j)),
            pl.BlockSpec((1, 1, tn), lambda l, j: (l, 0, j)),
        ],
        out_specs=pl.BlockSpec((1, M, tn), lambda l, j: (l, 0, j)),
        out_shape=jax.ShapeDtypeStruct((L, M, N), F32),
        compiler_params=_cparams("parallel", "parallel"),
        name="cond_linear",
    )(c, w, b.reshape(L, 1, N))


def _modproj_kernel(x_ref, sc_ref, sh_ref, w_ref, o_ref):
    u = (x_ref[...] * (1.0 + sc_ref[0]) + sh_ref[0]).astype(BF16)
    o_ref[...] = _dot(u, w_ref[...]).astype(o_ref.dtype)


def _modproj(x, sc, sh, w, T, tm, out_dtype):
    N, D = x.shape
    Dout = w.shape[1]
    sc3, sc_spec = _row_mod(sc, T, tm)
    sh3, sh_spec = _row_mod(sh, T, tm)
    return pl.pallas_call(
        _modproj_kernel,
        grid=(N // tm,),
        in_specs=[
            pl.BlockSpec((tm, D), lambda i: (i, 0)),
            sc_spec,
            sh_spec,
            pl.BlockSpec((D, Dout), lambda i: (0, 0)),
        ],
        out_specs=pl.BlockSpec((tm, Dout), lambda i: (i, 0)),
        out_shape=jax.ShapeDtypeStruct((N, Dout), out_dtype),
        compiler_params=_cparams("parallel"),
        name="modproj",
    )(x, sc3, sh3, w)


def _kv_kernel(x_ref, sc_ref, sh_ref, w_ref, bf_ref, k_ref, v_ref, kb_ref, vb_ref, lf_ref, cum_ref,
               carry_ref, *, T, tm):
    u = (x_ref[...] * (1.0 + sc_ref[0]) + sh_ref[0]).astype(BF16)
    p = _dot(u, w_ref[...])
    k = p[:, :FOX_DIM]
    v = p[:, FOX_DIM:2 * FOX_DIM]
    k_ref[...] = k
    v_ref[...] = v
    kb_ref[...] = k.astype(BF16)
    vb_ref[...] = v.astype(BF16)
    lf = _log_sigmoid(p[:, 2 * FOX_DIM:] + bf_ref[...])
    lf_ref[...] = lf
    if T % tm == 0:
        @pl.when(pl.program_id(0) % (T // tm) == 0)
        def _():
            carry_ref[...] = jnp.zeros_like(carry_ref)

        sb = LANES
        r = lax.broadcasted_iota(jnp.int32, (sb, sb), 0)
        c = lax.broadcasted_iota(jnp.int32, (sb, sb), 1)
        tri = (c <= r).astype(BF16)
        carry = carry_ref[...]
        for s in range(tm // sb):
            cs = _dot_exact_lhs(tri, lf[s * sb:(s + 1) * sb]) + carry
            cum_ref[s * sb:(s + 1) * sb, :] = cs
            carry = cs[sb - 1:sb, :]
        carry_ref[...] = carry
    else:
        r = lax.broadcasted_iota(jnp.int32, (tm, tm), 0)
        c = lax.broadcasted_iota(jnp.int32, (tm, tm), 1)
        tri = ((c <= r) & ((r // T) == (c // T))).astype(BF16)
        cum_ref[...] = _dot_exact_lhs(tri, lf)


def _kv_proj(x, sc, sh, w, bf, T, tm):
    N, D = x.shape
    sc3, sc_spec = _row_mod(sc, T, tm)
    sh3, sh_spec = _row_mod(sh, T, tm)
    row = lambda i: (i, 0)
    return pl.pallas_call(
        functools.partial(_kv_kernel, T=T, tm=tm),
        grid=(N // tm,),
        in_specs=[
            pl.BlockSpec((tm, D), row),
            sc_spec,
            sh_spec,
            pl.BlockSpec((D, KV_PAD), lambda i: (0, 0)),
            pl.BlockSpec((1, LANES), lambda i: (0, 0)),
        ],
        out_specs=[
            pl.BlockSpec((tm, FOX_DIM), row),
            pl.BlockSpec((tm, FOX_DIM), row),
            pl.BlockSpec((tm, FOX_DIM), row),
            pl.BlockSpec((tm, FOX_DIM), row),
            pl.BlockSpec((tm, LANES), row),
            pl.BlockSpec((tm, LANES), row),
        ],
        out_shape=[
            jax.ShapeDtypeStruct((N, FOX_DIM), F32),
            jax.ShapeDtypeStruct((N, FOX_DIM), F32),
            jax.ShapeDtypeStruct((N, FOX_DIM), BF16),
            jax.ShapeDtypeStruct((N, FOX_DIM), BF16),
            jax.ShapeDtypeStruct((N, LANES), F32),
            jax.ShapeDtypeStruct((N, LANES), F32),
        ],
        scratch_shapes=[pltpu.VMEM((1, LANES), F32)],
        compiler_params=_cparams("arbitrary"),
        name="kv_proj",
    )(x, sc3, sh3, w, bf)


def _gla_kernel(q_ref, k_ref, v_ref, g_ref, a_ref, wa_ref, ba_ref, ng_ref, s0_ref, o_ref, st_ref,
                state_ref, *, c, nsub, t_valid):
    t = pl.program_id(2)

    @pl.when(t == 0)
    def _():
        state_ref[...] = s0_ref[0, 0].T

    tc = c * nsub
    r = lax.broadcasted_iota(jnp.int32, (tc, tc), 0)
    cc = lax.broadcasted_iota(jnp.int32, (tc, tc), 1)
    causal = (cc <= r) & ((r // c) == (cc // c))
    scale = GLA_DKH ** -0.5
    z = _dot(a_ref[...].astype(BF16), wa_ref[...]) + ba_ref[...]
    la = _log_sigmoid(z) / GLA_GATE_TAU
    if t_valid < c:
        la = jnp.where(lax.broadcasted_iota(jnp.int32, la.shape, 0) % c < t_valid, la, 0.0)
    bc = _dot_exact_lhs(causal.astype(BF16), la)
    b_last = [bc[(i + 1) * c - 1:(i + 1) * c, :] for i in range(nsub)]
    b_end = jnp.concatenate([jnp.broadcast_to(b, (c, b.shape[1])) for b in b_last], axis=0)
    k = k_ref[...]
    v = v_ref[...].astype(BF16)
    q_dec = ((q_ref[...] * scale) * jnp.exp(bc)).astype(BF16)
    k_dec = (k * jnp.exp(-bc)).astype(BF16)
    k_end = (k * jnp.exp(b_end - bc)).astype(BF16)
    att = jnp.where(causal, _dot_nt(q_dec, k_dec), 0.0).astype(BF16)
    intra = _dot(att, v)
    st = state_ref[...]
    inter = []
    for i in range(nsub):
        sl = slice(i * c, (i + 1) * c)
        inter.append(_dot_nt(q_dec[sl], st.astype(BF16)))
        st = st * jnp.exp(b_last[i]) + _dot_tn(v[sl], k_end[sl])
    state_ref[...] = st
    o = jnp.concatenate(inter, axis=0) + intra
    o = o * lax.rsqrt(jnp.mean(o * o, axis=-1, keepdims=True) + RMS_EPS) * ng_ref[...]
    o_ref[...] = (o * _silu(g_ref[...])).astype(o_ref.dtype)

    @pl.when(t == pl.num_programs(2) - 1)
    def _():
        st_ref[0, 0] = st.T


def _gla(p, wa, ba, ng, s0, B, Tp, tc, t_valid):
    nt = Tp // tc
    c = GLA_CHUNK
    row = lambda b, h, t: b * nt + t
    kq = GLA_DK // GLA_DKH
    kv = 2 * GLA_DK // GLA_DVH
    kg = kv + GLA_HEADS
    ka = GLA_IN_MAIN // LANES
    return pl.pallas_call(
        functools.partial(_gla_kernel, c=c, nsub=tc // c, t_valid=t_valid),
        grid=(B, GLA_HEADS, nt),
        in_specs=[
            pl.BlockSpec((tc, GLA_DKH), lambda b, h, t: (row(b, h, t), h)),
            pl.BlockSpec((tc, GLA_DKH), lambda b, h, t: (row(b, h, t), kq + h)),
            pl.BlockSpec((tc, GLA_DVH), lambda b, h, t: (row(b, h, t), kv + h)),
            pl.BlockSpec((tc, GLA_DVH), lambda b, h, t: (row(b, h, t), kg + h)),
            pl.BlockSpec((tc, LANES), lambda b, h, t: (row(b, h, t), ka)),
            pl.BlockSpec((LANES, GLA_DKH), lambda b, h, t: (0, h)),
            pl.BlockSpec((1, GLA_DKH), lambda b, h, t: (0, h)),
            pl.BlockSpec((1, GLA_DVH), lambda b, h, t: (0, 0)),
            pl.BlockSpec((1, 1, GLA_DKH, GLA_DVH), lambda b, h, t: (b, h, 0, 0)),
        ],
        out_specs=[
            pl.BlockSpec((tc, GLA_DVH), lambda b, h, t: (row(b, h, t), h)),
            pl.BlockSpec((1, 1, GLA_DKH, GLA_DVH), lambda b, h, t: (b, h, 0, 0)),
        ],
        out_shape=[
            jax.ShapeDtypeStruct((B * Tp, GLA_DV), BF16),
            jax.ShapeDtypeStruct((B, GLA_HEADS, GLA_DKH, GLA_DVH), F32),
        ],
        scratch_shapes=[pltpu.VMEM((GLA_DVH, GLA_DKH), F32)],
        compiler_params=_cparams("parallel", "parallel", "arbitrary"),
        name="gla",
    )(p, p, p, p, p, wa, ba, ng, s0)


def _to_row_tiles(x):
    parts = [x[:, c * LANES:(c + 1) * LANES] for c in range(x.shape[1] // LANES)]
    return jnp.swapaxes(jnp.stack(parts, axis=0), 0, 1)


def _from_row_tiles(x3):
    xt = jnp.swapaxes(x3, 0, 1)
    return jnp.concatenate([xt[c] for c in range(xt.shape[0])], axis=1)


def _proj_res_ln_kernel(a_ref, w_ref, x_ref, gate_ref, lg_ref, lb_ref, sc_ref, sh_ref, xo_ref, uo_ref, u3_ref):
    y = _dot(a_ref[...].astype(BF16), w_ref[...])
    h = DEEPNORM_ALPHA * x_ref[...] + (1.0 + gate_ref[0]) * y
    xn = _layer_norm(h, lg_ref[...], lb_ref[...])
    xo_ref[...] = xn
    u = xn * (1.0 + sc_ref[0]) + sh_ref[0]
    uo_ref[...] = u.astype(uo_ref.dtype)
    u3_ref[...] = _to_row_tiles(u)


def _proj_res_ln(a, w, x, gate, lg, lb, sc, sh, T, tm):
    N, D = x.shape
    g3, g_spec = _row_mod(gate, T, tm)
    sc3, sc_spec = _row_mod(sc, T, tm)
    sh3, sh_spec = _row_mod(sh, T, tm)
    row = lambda i: (i, 0)
    fixed = lambda i: (0, 0)
    return pl.pallas_call(
        _proj_res_ln_kernel,
        grid=(N // tm,),
        in_specs=[
            pl.BlockSpec((tm, a.shape[1]), row),
            pl.BlockSpec(w.shape, fixed),
            pl.BlockSpec((tm, D), row),
            g_spec,
            pl.BlockSpec((1, D), fixed),
            pl.BlockSpec((1, D), fixed),
            sc_spec,
            sh_spec,
        ],
        out_specs=[pl.BlockSpec((tm, D), row), pl.BlockSpec((tm, D), row),
                   pl.BlockSpec((tm, D // LANES, LANES), lambda i: (i, 0, 0))],
        out_shape=[jax.ShapeDtypeStruct((N, D), F32), jax.ShapeDtypeStruct((N, D), BF16),
                   jax.ShapeDtypeStruct((N, D // LANES, LANES), F32)],
        compiler_params=_cparams("parallel"),
        name="proj_res_ln",
    )(a, w, x, g3, lg.reshape(1, D), lb.reshape(1, D), sc3, sh3)


def _router_kernel(u_ref, wr_ref, rb_ref, idx_ref, w_ref, rank_ref, cnt_ref, carry_ref, *, tm):
    @pl.when(pl.program_id(0) == 0)
    def _():
        carry_ref[...] = jnp.zeros_like(carry_ref)

    s = _sigmoid(_dot_nt(wr_ref[...], u_ref[...]))
    sb = s + rb_ref[...]
    neg = -jnp.inf
    sub = lax.broadcasted_iota(jnp.int32, (GROUP_SIZE, tm), 0)
    gs = GROUP_SIZE
    s_g = [s[g * gs:(g + 1) * gs] for g in range(N_GROUPS)]
    sb_g = [sb[g * gs:(g + 1) * gs] for g in range(N_GROUPS)]
    eid_g = [sub + g * gs for g in range(N_GROUPS)]

    score = []
    for g in range(N_GROUPS):
        m1 = jnp.max(sb_g[g], axis=0, keepdims=True)
        first = jnp.min(jnp.where(sb_g[g] == m1, sub, gs), axis=0, keepdims=True)
        m2 = jnp.max(jnp.where(sub == first, neg, sb_g[g]), axis=0, keepdims=True)
        score.append(m1 + m2)

    chosen = [jnp.zeros((1, tm), jnp.int32) for _ in range(N_GROUPS)]
    for _ in range(TOPK_GROUPS):
        m = score[0]
        for g in range(1, N_GROUPS):
            m = jnp.maximum(m, score[g])
        gi = jnp.full((1, tm), N_GROUPS, jnp.int32)
        for g in range(N_GROUPS - 1, -1, -1):
            gi = jnp.where(score[g] == m, g, gi)
        for g in range(N_GROUPS):
            hit = gi == g
            chosen[g] = jnp.where(hit, 1, chosen[g])
            score[g] = jnp.where(hit, neg, score[g])
    cand = [jnp.where(chosen[g] > 0, sb_g[g], neg) for g in range(N_GROUPS)]

    onehot = [jnp.zeros((gs, tm), F32) for _ in range(N_GROUPS)]
    idxs, ws = [], []
    for _ in range(TOP_K):
        mx = cand[0]
        for g in range(1, N_GROUPS):
            mx = jnp.maximum(mx, cand[g])
        m = jnp.max(mx, axis=0, keepdims=True)
        lo = jnp.where(cand[0] == m, eid_g[0], N_EXPERTS)
        for g in range(1, N_GROUPS):
            lo = jnp.minimum(lo, jnp.where(cand[g] == m, eid_g[g], N_EXPERTS))
        idx = jnp.min(lo, axis=0, keepdims=True)
        wsum = jnp.zeros((gs, tm), F32)
        for g in range(N_GROUPS):
            hit = eid_g[g] == idx
            wsum = wsum + jnp.where(hit, s_g[g], 0.0)
            cand[g] = jnp.where(hit, neg, cand[g])
            onehot[g] = jnp.where(hit, 1.0, onehot[g])
        idxs.append(idx)
        ws.append(jnp.sum(wsum, axis=0, keepdims=True))
    total = ws[0]
    for k in range(1, TOP_K):
        total = total + ws[k]
    for k in range(TOP_K):
        idx_ref[k:k + 1, :] = idxs[k]
        w_ref[k:k + 1, :] = ws[k] / total * ROUTED_SCALE

    oh = jnp.concatenate(onehot, axis=0)
    r = lax.broadcasted_iota(jnp.int32, (tm, tm), 0)
    c = lax.broadcasted_iota(jnp.int32, (tm, tm), 1)
    before = (r < c).astype(BF16)
    pre = _dot(oh.astype(BF16), before) + carry_ref[...]
    for k in range(TOP_K):
        acc = jnp.zeros((gs, tm), F32)
        for g in range(N_GROUPS):
            acc = acc + jnp.where(eid_g[g] == idxs[k], pre[g * gs:(g + 1) * gs], 0.0)
        rank_ref[k:k + 1, :] = jnp.sum(acc, axis=0, keepdims=True).astype(jnp.int32)
    carry_ref[...] = carry_ref[...] + jnp.sum(oh, axis=1, keepdims=True)
    cnt_ref[...] = jnp.broadcast_to(carry_ref[...], cnt_ref.shape).astype(jnp.int32)


def _router(u, wr_t, rb, tm):
    N, D = u.shape
    col = lambda i: (0, i)
    fixed = lambda i: (0, 0)
    return pl.pallas_call(
        functools.partial(_router_kernel, tm=tm),
        grid=(N // tm,),
        in_specs=[
            pl.BlockSpec((tm, D), lambda i: (i, 0)),
            pl.BlockSpec((N_EXPERTS, D), fixed),
            pl.BlockSpec((N_EXPERTS, 1), fixed),
        ],
        out_specs=[
            pl.BlockSpec((TOP_K, tm), col),
            pl.BlockSpec((TOP_K, tm), col),
            pl.BlockSpec((TOP_K, tm), col),
            pl.BlockSpec((N_EXPERTS, LANES), fixed),
        ],
        out_shape=[
            jax.ShapeDtypeStruct((TOP_K, N), jnp.int32),
            jax.ShapeDtypeStruct((TOP_K, N), F32),
            jax.ShapeDtypeStruct((TOP_K, N), jnp.int32),
            jax.ShapeDtypeStruct((N_EXPERTS, LANES), jnp.int32),
        ],
        scratch_shapes=[pltpu.VMEM((N_EXPERTS, 1), F32)],
        compiler_params=_cparams("arbitrary"),
        name="router",
    )(u, wr_t, rb)


def _experts_kernel(be_ref, live_ref, x_ref, wg_ref, wu_ref, wd_ref, o_ref, wgb_ref, wub_ref, wdb_ref):
    i = pl.program_id(0)
    e = be_ref[i]
    prev = be_ref[jnp.maximum(i - 1, 0)]

    @pl.when((i == 0) | (e != prev))
    def _():
        wgb_ref[...] = wg_ref[...].astype(BF16)
        wub_ref[...] = wu_ref[...].astype(BF16)
        wdb_ref[...] = wd_ref[...].astype(BF16)

    @pl.when(live_ref[i] > 0)
    def _():
        x = _from_row_tiles(x_ref[...]).astype(BF16)
        h = (_silu(_dot(x, wgb_ref[...])) * _dot(x, wub_ref[...])).astype(BF16)
        o_ref[...] = _to_row_tiles(_dot(h, wdb_ref[...]))

    @pl.when(live_ref[i] == 0)
    def _():
        o_ref[...] = jnp.zeros_like(o_ref)


EXPERT_ROWS = 512


def _experts(xs3, block_e, live, wg, wu, wd, layer, bm):
    R, S, _ = xs3.shape
    D = S * LANES
    return pl.pallas_call(
        _experts_kernel,
        grid_spec=pltpu.PrefetchScalarGridSpec(
            num_scalar_prefetch=2,
            grid=(R // bm,),
            in_specs=[
                pl.BlockSpec((bm, S, LANES), lambda i, be, lv: (i * lv[i], 0, 0)),
                pl.BlockSpec((None, None, D, D_EXPERT), lambda i, be, lv: (layer, be[i], 0, 0)),
                pl.BlockSpec((None, None, D, D_EXPERT), lambda i, be, lv: (layer, be[i], 0, 0)),
                pl.BlockSpec((None, None, D_EXPERT, D), lambda i, be, lv: (layer, be[i], 0, 0)),
            ],
            out_specs=pl.BlockSpec((bm, S, LANES), lambda i, be, lv: (i, 0, 0)),
            scratch_shapes=[
                pltpu.VMEM((D, D_EXPERT), BF16),
                pltpu.VMEM((D, D_EXPERT), BF16),
                pltpu.VMEM((D_EXPERT, D), BF16),
            ],
        ),
        out_shape=jax.ShapeDtypeStruct((R, S, LANES), F32),
        compiler_params=_cparams("arbitrary"),
        name="experts",
    )(block_e, live, xs3, wg, wu, wd)


ISSUE_TOKENS = 2


ZERO_ROWS = 128


def _dispatch_kernel(pad_end_ref, padded_ref, dest_ref, u_ref, *rest, tmd, bm, first):
    if first:
        xs_hbm, sem, zero_ref, zsem = rest

        @pl.when(pl.program_id(0) == 0)
        def _():
            zero_ref[...] = jnp.zeros_like(zero_ref)

            def clear_block(first_row):
                for s in range(bm // ZERO_ROWS):
                    rows = pl.ds(first_row + s * ZERO_ROWS, ZERO_ROWS)
                    pltpu.make_async_copy(zero_ref, xs_hbm.at[rows], zsem).start()

            def block_cleared():
                for s in range(bm // ZERO_ROWS):
                    pltpu.make_async_copy(zero_ref, xs_hbm.at[pl.ds(0, ZERO_ROWS)], zsem).wait()

            def clear(e, carry):
                pl.when(padded_ref[e] > 0)(lambda: clear_block(pad_end_ref[e] - bm))
                return carry

            def cleared(e, carry):
                pl.when(padded_ref[e] > 0)(block_cleared)
                return carry

            def clear_unused(b, carry):
                clear_block(b * bm)
                return carry

            def unused_cleared(b, carry):
                block_cleared()
                return carry

            first_unused = pad_end_ref[N_EXPERTS - 1] // bm
            n_blocks = xs_hbm.shape[0] // bm
            lax.fori_loop(0, N_EXPERTS, clear, 0)
            lax.fori_loop(first_unused, n_blocks, clear_unused, 0)
            lax.fori_loop(0, N_EXPERTS, cleared, 0)
            lax.fori_loop(first_unused, n_blocks, unused_cleared, 0)
    else:
        _, xs_hbm, sem = rest

    def issue(t2, carry):
        base = t2 * (ISSUE_TOKENS * TOP_K)
        rows = [dest_ref[base + r] for r in range(ISSUE_TOKENS * TOP_K)]
        for r, row in enumerate(rows):
            pltpu.make_async_copy(u_ref.at[t2 * ISSUE_TOKENS + r // TOP_K], xs_hbm.at[row], sem).start(priority=r % 2)
        return carry

    lax.fori_loop(0, tmd // ISSUE_TOKENS, issue, 0)
    for k in range(TOP_K):
        pltpu.make_async_copy(u_ref, xs_hbm.at[pl.ds(0, tmd)], sem).wait()


def _dispatch(u3, dest, xs_prev, pad_end, padded, R, tmd, bm):
    N, S, _ = u3.shape
    first = xs_prev is None
    in_specs = [
        pl.BlockSpec((tmd * TOP_K,), lambda i, pe, pd: (i,), memory_space=pltpu.SMEM),
        pl.BlockSpec((tmd, S, LANES), lambda i, pe, pd: (i, 0, 0)),
    ]
    scratch = [pltpu.SemaphoreType.DMA(())]
    if first:
        scratch += [pltpu.VMEM((ZERO_ROWS, S, LANES), F32), pltpu.SemaphoreType.DMA(())]
    else:
        in_specs.append(pl.BlockSpec(memory_space=pl.ANY))
    return pl.pallas_call(
        functools.partial(_dispatch_kernel, tmd=tmd, bm=bm, first=first),
        grid_spec=pltpu.PrefetchScalarGridSpec(
            num_scalar_prefetch=2,
            grid=(N // tmd,),
            in_specs=in_specs,
            out_specs=pl.BlockSpec(memory_space=pl.ANY),
            scratch_shapes=scratch,
        ),
        out_shape=jax.ShapeDtypeStruct((R, S, LANES), F32),
        input_output_aliases={} if first else {4: 0},
        compiler_params=_cparams("arbitrary"),
        name="moe_dispatch",
    )(pad_end, padded, dest, u3, *(() if first else (xs_prev,)))


def _combine_kernel(dest_ref, w_ref, y_hbm, o_ref, buf_ref, sem, *, tmc):
    def issue(t2, carry):
        first = t2 * (ISSUE_TOKENS * TOP_K)
        rows = [dest_ref[first + r] for r in range(ISSUE_TOKENS * TOP_K)]
        for r, row in enumerate(rows):
            pltpu.make_async_copy(y_hbm.at[row], buf_ref.at[first + r], sem).start(priority=r % 2)
        return carry

    lax.fori_loop(0, tmc // ISSUE_TOKENS, issue, 0)
    pltpu.make_async_copy(y_hbm.at[pl.ds(0, tmc * TOP_K)], buf_ref, sem).wait()

    def reduce(t, carry):
        acc = w_ref[t * TOP_K] * buf_ref[t * TOP_K]
        for k in range(1, TOP_K):
            acc = acc + w_ref[t * TOP_K + k] * buf_ref[t * TOP_K + k]
        o_ref[t] = acc
        return carry

    lax.fori_loop(0, tmc, reduce, 0)


def _combine(y3, dest, w, N, tmc):
    _, S, _ = y3.shape
    return pl.pallas_call(
        functools.partial(_combine_kernel, tmc=tmc),
        grid=(N // tmc,),
        in_specs=[
            pl.BlockSpec((tmc * TOP_K,), lambda i: (i,), memory_space=pltpu.SMEM),
            pl.BlockSpec((tmc * TOP_K,), lambda i: (i,), memory_space=pltpu.SMEM),
            pl.BlockSpec(memory_space=pl.ANY),
        ],
        out_specs=pl.BlockSpec((tmc, S, LANES), lambda i: (i, 0, 0)),
        out_shape=jax.ShapeDtypeStruct((N, S, LANES), F32),
        scratch_shapes=[pltpu.VMEM((tmc * TOP_K, S, LANES), F32), pltpu.SemaphoreType.DMA(())],
        compiler_params=_cparams("arbitrary"),
        name="moe_combine",
    )(dest, w, y3)


def _moe_final_kernel(u_ref, x_ref, r_ref, wgu_ref, wd_ref, gate_ref, lg_ref, lb_ref, xo_ref):
    gu = _dot(u_ref[...], wgu_ref[...])
    h = (_silu(gu[:, :D_SHARED]) * gu[:, D_SHARED:]).astype(BF16)
    y = _from_row_tiles(r_ref[...]) + _dot(h, wd_ref[...])
    hres = DEEPNORM_ALPHA * x_ref[...] + (1.0 + gate_ref[0]) * y
    xo_ref[...] = _layer_norm(hres, lg_ref[...], lb_ref[...])


def _moe_final(u, x, routed, wgu, wd, gate, lg, lb, T, tm):
    N, D = x.shape
    g3, g_spec = _row_mod(gate, T, tm)
    row = lambda i: (i, 0)
    fixed = lambda i: (0, 0)
    return pl.pallas_call(
        _moe_final_kernel,
        grid=(N // tm,),
        in_specs=[
            pl.BlockSpec((tm, D), row),
            pl.BlockSpec((tm, D), row),
            pl.BlockSpec((tm, D // LANES, LANES), lambda i: (i, 0, 0)),
            pl.BlockSpec(wgu.shape, fixed),
            pl.BlockSpec(wd.shape, fixed),
            g_spec,
            pl.BlockSpec((1, D), fixed),
            pl.BlockSpec((1, D), fixed),
        ],
        out_specs=pl.BlockSpec((tm, D), row),
        out_shape=jax.ShapeDtypeStruct((N, D), F32),
        compiler_params=_cparams("parallel"),
        name="moe_final",
    )(u, x, routed, wgu, wd, g3, lg.reshape(1, D), lb.reshape(1, D))


def _fox_kernel(q_ref, k_ref, v_ref, fq_ref, fk_ref, o_ref, m_ref, l_ref, *acc_refs, bq):
    i = pl.program_id(1)
    j = pl.program_id(2)

    @pl.when(j == 0)
    def _():
        m_ref[...] = jnp.full_like(m_ref, MASK_NEG)
        l_ref[...] = jnp.zeros_like(l_ref)
        for acc_ref in acc_refs:
            acc_ref[...] = jnp.zeros_like(acc_ref)

    @pl.when(j <= i)
    def _():
        kpos = j * bq + lax.broadcasted_iota(jnp.int32, (bq, bq), 0)
        qpos = i * bq + lax.broadcasted_iota(jnp.int32, (bq, bq), 1)
        visible = kpos <= qpos
        scale = FOX_HD ** -0.5
        m_all = m_ref[...]
        l_all = l_ref[...]
        fq_all = fq_ref[...]
        m_rows, l_rows = [], []
        for h, acc_ref in enumerate(acc_refs):
            hs = slice(h * FOX_HD, (h + 1) * FOX_HD)
            s = _dot_nt(k_ref[:, hs], q_ref[:, hs]) * scale + fq_all[h:h + 1, :] - fk_ref[:, h:h + 1]
            s = jnp.where(visible, s, MASK_NEG)
            m_prev = m_all[h:h + 1, :]
            m_new = jnp.maximum(m_prev, jnp.max(s, axis=0, keepdims=True))
            alpha = jnp.exp(m_prev - m_new)
            p = jnp.exp(s - m_new)
            l_rows.append(alpha * l_all[h:h + 1, :] + jnp.sum(p, axis=0, keepdims=True))
            m_rows.append(m_new)
            acc_ref[...] = alpha * acc_ref[...] + _dot_tn(v_ref[:, hs], p.astype(BF16))
        m_ref[...] = jnp.concatenate(m_rows, axis=0)
        l_ref[...] = jnp.concatenate(l_rows, axis=0)

    @pl.when(j == pl.num_programs(2) - 1)
    def _():
        l_all = l_ref[...]
        for h, acc_ref in enumerate(acc_refs):
            hs = slice(h * FOX_HD, (h + 1) * FOX_HD)
            o_ref[:, hs] = (acc_ref[...] / l_all[h:h + 1, :]).T.astype(o_ref.dtype)


def _fox_attention(q, kb, vb, cum, cum_t, B, T, bq):
    nq = T // bq
    kv_map = lambda b, i, j: (b * nq + jnp.minimum(j, i), 0)
    return pl.pallas_call(
        functools.partial(_fox_kernel, bq=bq),
        grid=(B, nq, nq),
        in_specs=[
            pl.BlockSpec((bq, FOX_DIM), lambda b, i, j: (b * nq + i, 0)),
            pl.BlockSpec((bq, FOX_DIM), kv_map),
            pl.BlockSpec((bq, FOX_DIM), kv_map),
            pl.BlockSpec((FOX_HEADS, bq), lambda b, i, j: (b, i)),
            pl.BlockSpec((bq, LANES), kv_map),
        ],
        out_specs=pl.BlockSpec((bq, FOX_DIM), lambda b, i, j: (b * nq + i, 0)),
        out_shape=jax.ShapeDtypeStruct((B * T, FOX_DIM), BF16),
        scratch_shapes=[
            pltpu.VMEM((FOX_HEADS, bq), F32),
            pltpu.VMEM((FOX_HEADS, bq), F32),
        ] + [pltpu.VMEM((FOX_HD, bq), F32) for _ in range(FOX_HEADS)],
        compiler_params=_cparams("parallel", "parallel", "arbitrary"),
        name="fox_attention",
    )(q, kb, vb, cum_t, cum)


def _decode_kernel(pt_ref, q_ref, kn_ref, vn_ref, fq_ref, fkn_ref, *rest, tq, npg):
    kc, vc, lfc = rest[:npg], rest[npg:2 * npg], rest[2 * npg:3 * npg]
    o_ref, m_ref, l_ref, acc_ref, carry_ref = rest[3 * npg:]
    j = pl.program_id(1)
    H = FOX_HEADS
    scale = FOX_HD ** -0.5
    q = [q_ref[:, h * FOX_HD:(h + 1) * FOX_HD].astype(BF16) for h in range(H)]
    fq = jnp.concatenate([fq_ref[:, h:h + 1] for h in range(H)], axis=0)

    def per_head_rows(x):
        return jnp.concatenate([jnp.broadcast_to(x[h:h + 1, :], (tq, x.shape[1])) for h in range(H)], axis=0)

    def attend(s, values, state):
        m_prev, l_prev, acc_prev = state
        m_new = jnp.maximum(m_prev, jnp.max(s, axis=1, keepdims=True))
        alpha = jnp.exp(m_prev - m_new)
        p = jnp.exp(s - m_new)
        l_new = alpha * l_prev + jnp.sum(p, axis=1, keepdims=True)
        pv = []
        for h in range(H):
            o = None
            for i, value in enumerate(values):
                ph = p[h * tq:(h + 1) * tq, i * PAGE_SIZE:(i + 1) * PAGE_SIZE].astype(BF16)
                t = _dot(ph, value(h))
                o = t if o is None else o + t
            pv.append(o)
        return m_new, l_new, alpha * acc_prev + jnp.concatenate(pv, axis=0)

    @pl.when(j == 0)
    def _():
        carry_ref[...] = jnp.zeros_like(carry_ref)
        r = lax.broadcasted_iota(jnp.int32, (H * tq, PAGE_SIZE), 0) % tq
        c = lax.broadcasted_iota(jnp.int32, (H * tq, PAGE_SIZE), 1)
        s = jnp.concatenate(
            [_dot_nt(q[h], kn_ref[0, :, h * FOX_HD:(h + 1) * FOX_HD]) for h in range(H)], axis=0)
        s = s * scale + fq - per_head_rows(fkn_ref[0])
        s = jnp.where(c <= r, s, MASK_NEG)
        state = (jnp.full((H * tq, 1), MASK_NEG, F32), jnp.zeros((H * tq, 1), F32),
                 jnp.zeros((H * tq, FOX_HD), F32))
        m_ref[...], l_ref[...], acc_ref[...] = attend(
            s, [lambda h: vn_ref[0, :, h * FOX_HD:(h + 1) * FOX_HD]], state)

    r = lax.broadcasted_iota(jnp.int32, (PAGE_SIZE, PAGE_SIZE), 0)
    c = lax.broadcasted_iota(jnp.int32, (PAGE_SIZE, PAGE_SIZE), 1)
    after = (r > c).astype(BF16)
    carry = carry_ref[...]
    blocks = []
    for i in range(npg):
        lf = lfc[i][...]
        suffix = _dot_exact_rhs(lf, after) + carry
        carry = carry + jnp.sum(lf, axis=1, keepdims=True)
        kt = jnp.swapaxes(kc[i][...], 0, 1)
        sk = jnp.concatenate([_dot_nt(q[h], kt[h].astype(BF16)) for h in range(H)], axis=0)
        blocks.append(sk * scale + per_head_rows(suffix))
    carry_ref[...] = carry
    s = jnp.concatenate(blocks, axis=1) + fq
    vts = [jnp.swapaxes(vc[i][...], 0, 1) for i in range(npg)]
    values = [lambda h, i=i: vts[i][h].astype(BF16) for i in range(npg)]
    m_new, l_new, acc_new = attend(s, values, (m_ref[...], l_ref[...], acc_ref[...]))
    m_ref[...] = m_new
    l_ref[...] = l_new
    acc_ref[...] = acc_new

    @pl.when(j == pl.num_programs(1) - 1)
    def _():
        out = acc_new / l_new
        for h in range(H):
            o_ref[:, h * FOX_HD:(h + 1) * FOX_HD] = out[h * tq:(h + 1) * tq, :]


DECODE_PAGES_PER_STEP = 8


def _decode_attention(page_table, q, k_new, v_new, cum, cum_new_t, cache_k, cache_v, cache_lf_t, B, tq):
    n_pages = page_table.shape[1]
    npg = DECODE_PAGES_PER_STEP
    assert n_pages % npg == 0

    def page(i, nd):
        return lambda b, j, pt: (pt[b * n_pages + (n_pages - 1 - (j * npg + i))],) + (0,) * nd

    kv_specs = [pl.BlockSpec((None, PAGE_SIZE, FOX_HEADS, FOX_HD), page(i, 3)) for i in range(npg)]
    lf_specs = [pl.BlockSpec((None, FOX_HEADS, PAGE_SIZE), page(i, 2)) for i in range(npg)]
    return pl.pallas_call(
        functools.partial(_decode_kernel, tq=tq, npg=npg),
        grid_spec=pltpu.PrefetchScalarGridSpec(
            num_scalar_prefetch=1,
            grid=(B, n_pages // npg),
            in_specs=[
                pl.BlockSpec((tq, FOX_DIM), lambda b, j, pt: (b, 0)),
                pl.BlockSpec((1, PAGE_SIZE, FOX_DIM), lambda b, j, pt: (b, 0, 0)),
                pl.BlockSpec((1, PAGE_SIZE, FOX_DIM), lambda b, j, pt: (b, 0, 0)),
                pl.BlockSpec((tq, LANES), lambda b, j, pt: (b, 0)),
                pl.BlockSpec((1, FOX_HEADS, PAGE_SIZE), lambda b, j, pt: (b, 0, 0)),
            ] + kv_specs + kv_specs + lf_specs,
            out_specs=pl.BlockSpec((tq, FOX_DIM), lambda b, j, pt: (b, 0)),
            scratch_shapes=[
                pltpu.VMEM((FOX_HEADS * tq, 1), F32),
                pltpu.VMEM((FOX_HEADS * tq, 1), F32),
                pltpu.VMEM((FOX_HEADS * tq, FOX_HD), F32),
                pltpu.VMEM((FOX_HEADS, 1), F32),
            ],
        ),
        out_shape=jax.ShapeDtypeStruct((B * tq, FOX_DIM), F32),
        compiler_params=_cparams("parallel", "arbitrary"),
        name="decode_attention",
    )(page_table.reshape(-1), q, k_new, v_new, cum, cum_new_t, *([cache_k] * npg), *([cache_v] * npg),
      *([cache_lf_t] * npg))


def _moe(groups, W, l):
    bm = EXPERT_ROWS
    routes = [_router(g["u"], W["router_wt"][l], W["router_bias"][l].reshape(N_EXPERTS, 1), g["tm"])
              for g in groups]
    counts = [r[3][:, 0] for r in routes]
    total = sum(counts)
    padded = (total + bm - 1) // bm * bm
    pad_end = jnp.cumsum(padded)
    n_blocks = -(-sum(g["N"] for g in groups) * TOP_K // bm) + N_EXPERTS
    starts = jnp.arange(n_blocks, dtype=jnp.int32) * bm
    block_e = jnp.minimum(jnp.sum(pad_end[None, :] <= starts[:, None], axis=1), N_EXPERTS - 1).astype(jnp.int32)
    live = (starts < pad_end[-1]).astype(jnp.int32)
    offset = pad_end - padded
    xs3, dests = None, []
    for g, (idx_t, _, rank_t, _), cnt in zip(groups, routes, counts):
        first = jnp.sum(jnp.where(idx_t[:, :, None] == jnp.arange(N_EXPERTS), offset, 0), axis=-1)
        dests.append((first + rank_t).T.reshape(-1))
        xs3 = _dispatch(g["u3"], dests[-1], xs3, pad_end, padded, n_blocks * bm, min(256, g["N"]), bm)
        offset = offset + cnt
    y3 = _experts(xs3, block_e, live, W["exp_w_gate"], W["exp_w_up"], W["exp_w_down"], l, bm)
    for g, (_, w_t, _, _), dest in zip(groups, routes, dests):
        routed3 = _combine(y3, dest, w_t.T.reshape(-1), g["N"], min(128, g["N"]))
        g["x"] = _moe_final(g["u"], g["x"], routed3, W["sh_wgu"][l], W["sh_wd"][l], g["g_m"], W["ln_g"][l, 1],
                            W["ln_b"][l, 1], g["T"], g["tm"])


def _mixer(g, W, l):
    B, T, N, tm, past = g["B"], g["T"], g["N"], g["tm"], g["past"]
    sh_a, sc_a, g_a, sh_m, sc_m, g_m = jnp.split(g["mods"][l], 6, axis=-1)
    if l < N_A:
        p = _modproj(g["x"], sc_a, sh_a, W["gla_w_in"][l], T, tm, F32)
        if T >= GLA_CHUNK:
            Tp, tc = T, min(512, T)
        else:
            Tp = tc = GLA_CHUNK
            p = jnp.pad(p.reshape(B, T, -1), ((0, 0), (0, Tp - T), (0, 0))).reshape(B * Tp, -1)
        o, s_T = _gla(p, W["gla_w_a2"][l], W["gla_b_a"][l], W["gla_norm_g"][l], g["s0"][l], B, Tp, tc, T)
        if Tp != T:
            o = o.reshape(B, Tp, -1)[:, :T].reshape(N, -1)
        g["gla"].append(s_T)
        w_o = W["gla_w_o"][l]
    else:
        q = _modproj(g["x"], sc_a, sh_a, W["fox_w_q"][l - N_A], T, tm, BF16 if past is None else F32)
        if past is None:
            o = _fox_attention(q, g["kb"], g["vb"], g["cum"], g["cum_t"], B, T, min(256, T))
        else:
            o = _decode_attention(past["page_table"], q, g["kb"], g["vb"], g["cum"], g["cum_t"], past["k"],
                                  past["v"], past["lf_t"], B, T)
        w_o = W["fox_w_o"][l - N_A]
    g["x"], g["u"], g["u3"] = _proj_res_ln(o, w_o, g["x"], g_a, W["ln_g"][l, 0], W["ln_b"][l, 0], sc_m, sh_m, T, tm)
    g["g_m"] = g_m


def _shared_kv(g, W):
    B, T = g["B"], g["T"]
    sh_kv, sc_kv = jnp.split(g["kv_mod"], 2, axis=-1)
    g["k"], g["v"], kb, vb, g["lf"], g["cum"] = _kv_proj(g["x"], sc_kv, sh_kv, W["kv_w"], W["fox_b_f"], T, g["tm"])
    cum_h = g["cum"][:, :FOX_HEADS].reshape(B, T, FOX_HEADS)
    if g["past"] is None:
        g["cum_t"] = cum_h.transpose(0, 2, 1).reshape(B * FOX_HEADS, T)
        g["kb"], g["vb"] = kb, vb
    else:
        rows = ((0, 0), (0, PAGE_SIZE - T), (0, 0))
        g["cum_t"] = jnp.pad(cum_h, rows).transpose(0, 2, 1)
        g["kb"] = jnp.pad(kb.reshape(B, T, FOX_DIM), rows)
        g["vb"] = jnp.pad(vb.reshape(B, T, FOX_DIM), rows)


def _group(x, mods, kv_mod, s0, past):
    B, T, D = x.shape
    return dict(x=x.reshape(B * T, D), B=B, T=T, N=B * T, tm=min(512, B * T), mods=mods, kv_mod=kv_mod, s0=s0,
                past=past, gla=[])


def _group_outputs(g):
    B, T = g["B"], g["T"]
    return (g["x"].reshape(B, T, D_MODEL), jnp.stack(g["gla"]),
            g["k"].reshape(B, T, FOX_HEADS, FOX_HD), g["v"].reshape(B, T, FOX_HEADS, FOX_HD),
            g["lf"][:, :FOX_HEADS].reshape(B, T, FOX_HEADS))


def kernel(x_prompt, x_sample, c_prompt, c_sample, state_gla, cache_k, cache_v, cache_logf, page_table, gla_w_in, gla_w_a2, gla_b_a, gla_norm_g, gla_w_o, kv_w_mod, kv_b_mod, kv_w, fox_b_f, fox_w_q, fox_w_o, ada_w, ada_b, ln_g, ln_b, router_w, router_bias, exp_w_gate, exp_w_up, exp_w_down, sh_w_gate, sh_w_up, sh_w_down):
    Bp, Tp, D = x_prompt.shape
    Bs, Ts, _ = x_sample.shape
    lane_pad = lambda w, n: jnp.pad(w, [(0, 0)] * (w.ndim - 1) + [(0, n - w.shape[-1])])
    W = dict(
        gla_w_in=jnp.concatenate(
            [gla_w_in[..., :GLA_IN_MAIN], lane_pad(gla_w_in[..., GLA_IN_MAIN:], LANES)], axis=-1).astype(BF16),
        gla_w_a2=jnp.pad(gla_w_a2, ((0, 0), (0, LANES - GLA_GATE_RANK), (0, 0))).astype(BF16),
        gla_b_a=gla_b_a.reshape(N_A, 1, GLA_DK),
        gla_norm_g=gla_norm_g.reshape(N_A, 1, GLA_DVH),
        gla_w_o=gla_w_o.astype(BF16),
        kv_w=jnp.concatenate([kv_w[:, :2 * FOX_DIM], lane_pad(kv_w[:, 2 * FOX_DIM:], LANES)], axis=-1).astype(BF16),
        fox_b_f=lane_pad(fox_b_f.reshape(1, FOX_HEADS), LANES),
        fox_w_q=fox_w_q.astype(BF16),
        fox_w_o=fox_w_o.astype(BF16),
        ln_g=ln_g, ln_b=ln_b,
        router_wt=jnp.swapaxes(router_w, 1, 2).astype(BF16),
        router_bias=router_bias,
        exp_w_gate=exp_w_gate, exp_w_up=exp_w_up, exp_w_down=exp_w_down,
        sh_wgu=jnp.concatenate([sh_w_gate, sh_w_up], axis=-1).astype(BF16),
        sh_wd=sh_w_down.astype(BF16),
    )
    c_all = jnp.concatenate([c_prompt, c_sample], axis=0)
    mods = _cond_linear(c_all, ada_w, ada_b, 1536)
    kv_mod = _cond_linear(c_all, kv_w_mod[None], kv_b_mod[None], 1024)[0]

    s0_prompt = jnp.zeros((N_A, Bp, GLA_HEADS, GLA_DKH, GLA_DVH), F32)
    past = dict(page_table=page_table, k=cache_k, v=cache_v, lf_t=jnp.swapaxes(cache_logf, 1, 2))
    groups = [_group(x_prompt, mods[:, :Bp], kv_mod[:Bp], s0_prompt, None),
              _group(x_sample, mods[:, Bp:], kv_mod[Bp:], state_gla, past)]
    for l in range(DEPTH):
        for g in groups:
            _mixer(g, W, l)
        _moe(groups, W, l)
        if l == N_A - 1:
            for g in groups:
                _shared_kv(g, W)
    y_p, gla_p, k_p, v_p, lf_p = _group_outputs(groups[0])
    y_s, gla_s, k_s, v_s, lf_s = _group_outputs(groups[1])
    return (y_p, y_s, gla_p, gla_s, k_p, v_p, lf_p, k_s, v_s, lf_s)
```

```python
import functools

import jax
import jax.numpy as jnp
from jax import lax
from jax.experimental import pallas as pl
from jax.experimental.pallas import tpu as pltpu

F32 = jnp.float32
BF16 = jnp.bfloat16

D_MODEL = 1024
DEPTH = 4
N_A = DEPTH // 2
PAGE_SIZE = 128
GLA_HEADS = 4
GLA_DK = D_MODEL // 2
GLA_DV = D_MODEL
GLA_DKH = GLA_DK // GLA_HEADS
GLA_DVH = GLA_DV // GLA_HEADS
GLA_GATE_RANK = 16
GLA_GATE_TAU = 16.0
GLA_CHUNK = 64
FOX_HEADS = 8
FOX_HD = D_MODEL // FOX_HEADS
FOX_DIM = FOX_HEADS * FOX_HD
N_EXPERTS = 64
TOP_K = 8
N_GROUPS = 8
GROUP_SIZE = N_EXPERTS // N_GROUPS
TOPK_GROUPS = 4
D_EXPERT = 256
D_SHARED = 256
ROUTED_SCALE = 2.5
DEEPNORM_ALPHA = (2 * DEPTH) ** 0.25
LN_EPS = 1e-5
RMS_EPS = 1e-6

LANES = 128
GLA_IN_MAIN = 2 * GLA_DK + 2 * GLA_DV
GLA_IN_PAD = GLA_IN_MAIN + LANES
KV_PAD = 2 * FOX_DIM + LANES
MASK_NEG = -1e30
VMEM_LIMIT = 52 * 1024 * 1024


def _cparams(*sem):
    return pltpu.CompilerParams(dimension_semantics=sem, vmem_limit_bytes=VMEM_LIMIT)


def _dot(a, b):
    return jnp.dot(a, b, preferred_element_type=F32)


def _dot_nt(a, b):
    return lax.dot_general(a, b, (((1,), (1,)), ((), ())), preferred_element_type=F32)


def _dot_tn(a, b):
    return lax.dot_general(a, b, (((0,), (0,)), ((), ())), preferred_element_type=F32)


def _sigmoid(x):
    return 1.0 / (1.0 + jnp.exp(-x))


def _silu(x):
    return x * _sigmoid(x)


def _log_sigmoid(x):
    return jnp.minimum(x, 0.0) - jnp.log1p(jnp.exp(-jnp.abs(x)))


def _split3(x):
    hi = x.astype(BF16)
    r = x - hi.astype(F32)
    mid = r.astype(BF16)
    lo = (r - mid.astype(F32)).astype(BF16)
    return hi, mid, lo


def _dot_exact_lhs(mat01, x):
    hi, mid, lo = _split3(x)
    return _dot(mat01, hi) + _dot(mat01, mid) + _dot(mat01, lo)


def _dot_exact_rhs(x, mat01):
    hi, mid, lo = _split3(x)
    return _dot(hi, mat01) + _dot(mid, mat01) + _dot(lo, mat01)


def _layer_norm(h, g, b):
    mu = jnp.mean(h, axis=-1, keepdims=True)
    xc = h - mu
    var = jnp.mean(xc * xc, axis=-1, keepdims=True)
    return xc * lax.rsqrt(var + LN_EPS) * g + b


def _row_mod(m, T, tm):
    B, D = m.shape
    if T % tm == 0:
        tiles_per_batch = T // tm
        return m.reshape(B, 1, D), pl.BlockSpec((1, 1, D), lambda i: (i // tiles_per_batch, 0, 0))
    assert tm % T == 0
    e = jnp.repeat(m, T, axis=0).reshape(B * T // tm, tm, D)
    return e, pl.BlockSpec((1, tm, D), lambda i: (i, 0, 0))


def _cond_kernel(c_ref, w_ref, b_ref, o_ref):
    s = _silu(c_ref[...]).astype(BF16)
    o_ref[0] = _dot(s, w_ref[0].astype(BF16)) + b_ref[0]


def _cond_linear(c, w, b, tn):
    M, D = c.shape
    L, _, N = w.shape
    return pl.pallas_call(
        _cond_kernel,
        grid=(L, N // tn),
        in_specs=[
            pl.BlockSpec((M, D), lambda l, j: (0, 0)),
            pl.BlockSpec((1, D, tn), lambda l, j: (l, 0, j)),
            pl.BlockSpec((1, 1, tn), lambda l, j: (l, 0, j)),
        ],
        out_specs=pl.BlockSpec((1, M, tn), lambda l, j: (l, 0, j)),
        out_shape=jax.ShapeDtypeStruct((L, M, N), F32),
        compiler_params=_cparams("parallel", "parallel"),
        name="cond_linear",
    )(c, w, b.reshape(L, 1, N))


def _modproj_kernel(x_ref, sc_ref, sh_ref, w_ref, o_ref):
    u = (x_ref[...] * (1.0 + sc_ref[0]) + sh_ref[0]).astype(BF16)
    o_ref[...] = _dot(u, w_ref[...]).astype(o_ref.dtype)


def _modproj(x, sc, sh, w, T, tm, out_dtype):
    N, D = x.shape
    Dout = w.shape[1]
    sc3, sc_spec = _row_mod(sc, T, tm)
    sh3, sh_spec = _row_mod(sh, T, tm)
    return pl.pallas_call(
        _modproj_kernel,
        grid=(N // tm,),
        in_specs=[
            pl.BlockSpec((tm, D), lambda i: (i, 0)),
            sc_spec,
            sh_spec,
            pl.BlockSpec((D, Dout), lambda i: (0, 0)),
        ],
        out_specs=pl.BlockSpec((tm, Dout), lambda i: (i, 0)),
        out_shape=jax.ShapeDtypeStruct((N, Dout), out_dtype),
        compiler_params=_cparams("parallel"),
        name="modproj",
    )(x, sc3, sh3, w)


def _kv_kernel(x_ref, sc_ref, sh_ref, w_ref, bf_ref, k_ref, v_ref, kb_ref, vb_ref, lf_ref, cum_ref,
               carry_ref, *, T, tm):
    u = (x_ref[...] * (1.0 + sc_ref[0]) + sh_ref[0]).astype(BF16)
    p = _dot(u, w_ref[...])
    k = p[:, :FOX_DIM]
    v = p[:, FOX_DIM:2 * FOX_DIM]
    k_ref[...] = k
    v_ref[...] = v
    kb_ref[...] = k.astype(BF16)
    vb_ref[...] = v.astype(BF16)
    lf = _log_sigmoid(p[:, 2 * FOX_DIM:] + bf_ref[...])
    lf_ref[...] = lf
    if T % tm == 0:
        @pl.when(pl.program_id(0) % (T // tm) == 0)
        def _():
            carry_ref[...] = jnp.zeros_like(carry_ref)

        sb = LANES
        r = lax.broadcasted_iota(jnp.int32, (sb, sb), 0)
        c = lax.broadcasted_iota(jnp.int32, (sb, sb), 1)
        tri = (c <= r).astype(BF16)
        carry = carry_ref[...]
        for s in range(tm // sb):
            cs = _dot_exact_lhs(tri, lf[s * sb:(s + 1) * sb]) + carry
            cum_ref[s * sb:(s + 1) * sb, :] = cs
            carry = cs[sb - 1:sb, :]
        carry_ref[...] = carry
    else:
        r = lax.broadcasted_iota(jnp.int32, (tm, tm), 0)
        c = lax.broadcasted_iota(jnp.int32, (tm, tm), 1)
        tri = ((c <= r) & ((r // T) == (c // T))).astype(BF16)
        cum_ref[...] = _dot_exact_lhs(tri, lf)


def _kv_proj(x, sc, sh, w, bf, T, tm):
    N, D = x.shape
    sc3, sc_spec = _row_mod(sc, T, tm)
    sh3, sh_spec = _row_mod(sh, T, tm)
    row = lambda i: (i, 0)
    return pl.pallas_call(
        functools.partial(_kv_kernel, T=T, tm=tm),
        grid=(N // tm,),
        in_specs=[
            pl.BlockSpec((tm, D), row),
            sc_spec,
            sh_spec,
            pl.BlockSpec((D, KV_PAD), lambda i: (0, 0)),
            pl.BlockSpec((1, LANES), lambda i: (0, 0)),
        ],
        out_specs=[
            pl.BlockSpec((tm, FOX_DIM), row),
            pl.BlockSpec((tm, FOX_DIM), row),
            pl.BlockSpec((tm, FOX_DIM), row),
            pl.BlockSpec((tm, FOX_DIM), row),
            pl.BlockSpec((tm, LANES), row),
            pl.BlockSpec((tm, LANES), row),
        ],
        out_shape=[
            jax.ShapeDtypeStruct((N, FOX_DIM), F32),
            jax.ShapeDtypeStruct((N, FOX_DIM), F32),
            jax.ShapeDtypeStruct((N, FOX_DIM), BF16),
            jax.ShapeDtypeStruct((N, FOX_DIM), BF16),
            jax.ShapeDtypeStruct((N, LANES), F32),
            jax.ShapeDtypeStruct((N, LANES), F32),
        ],
        scratch_shapes=[pltpu.VMEM((1, LANES), F32)],
        compiler_params=_cparams("arbitrary"),
        name="kv_proj",
    )(x, sc3, sh3, w, bf)


def _gla_kernel(q_ref, k_ref, v_ref, g_ref, a_ref, wa_ref, ba_ref, ng_ref, s0_ref, o_ref, st_ref,
                state_ref, *, c, nsub, t_valid):
    t = pl.program_id(2)

    @pl.when(t == 0)
    def _():
        state_ref[...] = s0_ref[0, 0].T

    tc = c * nsub
    r = lax.broadcasted_iota(jnp.int32, (tc, tc), 0)
    cc = lax.broadcasted_iota(jnp.int32, (tc, tc), 1)
    causal = (cc <= r) & ((r // c) == (cc // c))
    scale = GLA_DKH ** -0.5
    z = _dot(a_ref[...].astype(BF16), wa_ref[...]) + ba_ref[...]
    la = _log_sigmoid(z) / GLA_GATE_TAU
    if t_valid < c:
        la = jnp.where(lax.broadcasted_iota(jnp.int32, la.shape, 0) % c < t_valid, la, 0.0)
    bc = _dot_exact_lhs(causal.astype(BF16), la)
    b_last = [bc[(i + 1) * c - 1:(i + 1) * c, :] for i in range(nsub)]
    b_end = jnp.concatenate([jnp.broadcast_to(b, (c, b.shape[1])) for b in b_last], axis=0)
    k = k_ref[...]
    v = v_ref[...].astype(BF16)
    q_dec = ((q_ref[...] * scale) * jnp.exp(bc)).astype(BF16)
    k_dec = (k * jnp.exp(-bc)).astype(BF16)
    k_end = (k * jnp.exp(b_end - bc)).astype(BF16)
    att = jnp.where(causal, _dot_nt(q_dec, k_dec), 0.0).astype(BF16)
    intra = _dot(att, v)
    st = state_ref[...]
    inter = []
    for i in range(nsub):
        sl = slice(i * c, (i + 1) * c)
        inter.append(_dot_nt(q_dec[sl], st.astype(BF16)))
        st = st * jnp.exp(b_last[i]) + _dot_tn(v[sl], k_end[sl])
    state_ref[...] = st
    o = jnp.concatenate(inter, axis=0) + intra
    o = o * lax.rsqrt(jnp.mean(o * o, axis=-1, keepdims=True) + RMS_EPS) * ng_ref[...]
    o_ref[...] = (o * _silu(g_ref[...])).astype(o_ref.dtype)

    @pl.when(t == pl.num_programs(2) - 1)
    def _():
        st_ref[0, 0] = st.T


def _gla(p, wa, ba, ng, s0, B, Tp, tc, t_valid):
    nt = Tp // tc
    c = GLA_CHUNK
    row = lambda b, h, t: b * nt + t
    kq = GLA_DK // GLA_DKH
    kv = 2 * GLA_DK // GLA_DVH
    kg = kv + GLA_HEADS
    ka = GLA_IN_MAIN // LANES
    return pl.pallas_call(
        functools.partial(_gla_kernel, c=c, nsub=tc // c, t_valid=t_valid),
        grid=(B, GLA_HEADS, nt),
        in_specs=[
            pl.BlockSpec((tc, GLA_DKH), lambda b, h, t: (row(b, h, t), h)),
            pl.BlockSpec((tc, GLA_DKH), lambda b, h, t: (row(b, h, t), kq + h)),
            pl.BlockSpec((tc, GLA_DVH), lambda b, h, t: (row(b, h, t), kv + h)),
            pl.BlockSpec((tc, GLA_DVH), lambda b, h, t: (row(b, h, t), kg + h)),
            pl.BlockSpec((tc, LANES), lambda b, h, t: (row(b, h, t), ka)),
            pl.BlockSpec((LANES, GLA_DKH), lambda b, h, t: (0, h)),
            pl.BlockSpec((1, GLA_DKH), lambda b, h, t: (0, h)),
            pl.BlockSpec((1, GLA_DVH), lambda b, h, t: (0, 0)),
            pl.BlockSpec((1, 1, GLA_DKH, GLA_DVH), lambda b, h, t: (b, h, 0, 0)),
        ],
        out_specs=[
            pl.BlockSpec((tc, GLA_DVH), lambda b, h, t: (row(b, h, t), h)),
            pl.BlockSpec((1, 1, GLA_DKH, GLA_DVH), lambda b, h, t: (b, h, 0, 0)),
        ],
        out_shape=[
            jax.ShapeDtypeStruct((B * Tp, GLA_DV), BF16),
            jax.ShapeDtypeStruct((B, GLA_HEADS, GLA_DKH, GLA_DVH), F32),
        ],
        scratch_shapes=[pltpu.VMEM((GLA_DVH, GLA_DKH), F32)],
        compiler_params=_cparams("parallel", "parallel", "arbitrary"),
        name="gla",
    )(p, p, p, p, p, wa, ba, ng, s0)


def _to_row_tiles(x):
    parts = [x[:, c * LANES:(c + 1) * LANES] for c in range(x.shape[1] // LANES)]
    return jnp.swapaxes(jnp.stack(parts, axis=0), 0, 1)


def _from_row_tiles(x3):
    xt = jnp.swapaxes(x3, 0, 1)
    return jnp.concatenate([xt[c] for c in range(xt.shape[0])], axis=1)


def _proj_res_ln_kernel(a_ref, w_ref, x_ref, gate_ref, lg_ref, lb_ref, sc_ref, sh_ref, xo_ref, uo_ref, u3_ref):
    y = _dot(a_ref[...].astype(BF16), w_ref[...])
    h = DEEPNORM_ALPHA * x_ref[...] + (1.0 + gate_ref[0]) * y
    xn = _layer_norm(h, lg_ref[...], lb_ref[...])
    xo_ref[...] = xn
    u = xn * (1.0 + sc_ref[0]) + sh_ref[0]
    uo_ref[...] = u.astype(uo_ref.dtype)
    u3_ref[...] = _to_row_tiles(u)


def _proj_res_ln(a, w, x, gate, lg, lb, sc, sh, T, tm):
    N, D = x.shape
    g3, g_spec = _row_mod(gate, T, tm)
    sc3, sc_spec = _row_mod(sc, T, tm)
    sh3, sh_spec = _row_mod(sh, T, tm)
    row = lambda i: (i, 0)
    fixed = lambda i: (0, 0)
    return pl.pallas_call(
        _proj_res_ln_kernel,
        grid=(N // tm,),
        in_specs=[
            pl.BlockSpec((tm, a.shape[1]), row),
            pl.BlockSpec(w.shape, fixed),
            pl.BlockSpec((tm, D), row),
            g_spec,
            pl.BlockSpec((1, D), fixed),
            pl.BlockSpec((1, D), fixed),
            sc_spec,
            sh_spec,
        ],
        out_specs=[pl.BlockSpec((tm, D), row), pl.BlockSpec((tm, D), row),
                   pl.BlockSpec((tm, D // LANES, LANES), lambda i: (i, 0, 0))],
        out_shape=[jax.ShapeDtypeStruct((N, D), F32), jax.ShapeDtypeStruct((N, D), BF16),
                   jax.ShapeDtypeStruct((N, D // LANES, LANES), F32)],
        compiler_params=_cparams("parallel"),
        name="proj_res_ln",
    )(a, w, x, g3, lg.reshape(1, D), lb.reshape(1, D), sc3, sh3)


def _router_kernel(u_ref, wr_ref, rb_ref, idx_ref, w_ref, rank_ref, cnt_ref, carry_ref, *, tm):
    @pl.when(pl.program_id(0) == 0)
    def _():
        carry_ref[...] = jnp.zeros_like(carry_ref)

    s = _sigmoid(_dot_nt(wr_ref[...], u_ref[...]))
    sb = s + rb_ref[...]
    neg = -jnp.inf
    sub = lax.broadcasted_iota(jnp.int32, (GROUP_SIZE, tm), 0)
    gs = GROUP_SIZE
    s_g = [s[g * gs:(g + 1) * gs] for g in range(N_GROUPS)]
    sb_g = [sb[g * gs:(g + 1) * gs] for g in range(N_GROUPS)]
    eid_g = [sub + g * gs for g in range(N_GROUPS)]

    score = []
    for g in range(N_GROUPS):
        m1 = jnp.max(sb_g[g], axis=0, keepdims=True)
        first = jnp.min(jnp.where(sb_g[g] == m1, sub, gs), axis=0, keepdims=True)
        m2 = jnp.max(jnp.where(sub == first, neg, sb_g[g]), axis=0, keepdims=True)
        score.append(m1 + m2)

    chosen = [jnp.zeros((1, tm), jnp.int32) for _ in range(N_GROUPS)]
    for _ in range(TOPK_GROUPS):
        m = score[0]
        for g in range(1, N_GROUPS):
            m = jnp.maximum(m, score[g])
        gi = jnp.full((1, tm), N_GROUPS, jnp.int32)
        for g in range(N_GROUPS - 1, -1, -1):
            gi = jnp.where(score[g] == m, g, gi)
        for g in range(N_GROUPS):
            hit = gi == g
            chosen[g] = jnp.where(hit, 1, chosen[g])
            score[g] = jnp.where(hit, neg, score[g])
    cand = [jnp.where(chosen[g] > 0, sb_g[g], neg) for g in range(N_GROUPS)]

    onehot = [jnp.zeros((gs, tm), F32) for _ in range(N_GROUPS)]
    idxs, ws = [], []
    for _ in range(TOP_K):
        mx = cand[0]
        for g in range(1, N_GROUPS):
            mx = jnp.maximum(mx, cand[g])
        m = jnp.max(mx, axis=0, keepdims=True)
        lo = jnp.where(cand[0] == m, eid_g[0], N_EXPERTS)
        for g in range(1, N_GROUPS):
            lo = jnp.minimum(lo, jnp.where(cand[g] == m, eid_g[g], N_EXPERTS))
        idx = jnp.min(lo, axis=0, keepdims=True)
        wsum = jnp.zeros((gs, tm), F32)
        for g in range(N_GROUPS):
            hit = eid_g[g] == idx
            wsum = wsum + jnp.where(hit, s_g[g], 0.0)
            cand[g] = jnp.where(hit, neg, cand[g])
            onehot[g] = jnp.where(hit, 1.0, onehot[g])
        idxs.append(idx)
        ws.append(jnp.sum(wsum, axis=0, keepdims=True))
    total = ws[0]
    for k in range(1, TOP_K):
        total = total + ws[k]
    for k in range(TOP_K):
        idx_ref[k:k + 1, :] = idxs[k]
        w_ref[k:k + 1, :] = ws[k] / total * ROUTED_SCALE

    oh = jnp.concatenate(onehot, axis=0)
    r = lax.broadcasted_iota(jnp.int32, (tm, tm), 0)
    c = lax.broadcasted_iota(jnp.int32, (tm, tm), 1)
    before = (r < c).astype(BF16)
    pre = _dot(oh.astype(BF16), before) + carry_ref[...]
    for k in range(TOP_K):
        acc = jnp.zeros((gs, tm), F32)
        for g in range(N_GROUPS):
            acc = acc + jnp.where(eid_g[g] == idxs[k], pre[g * gs:(g + 1) * gs], 0.0)
        rank_ref[k:k + 1, :] = jnp.sum(acc, axis=0, keepdims=True).astype(jnp.int32)
    carry_ref[...] = carry_ref[...] + jnp.sum(oh, axis=1, keepdims=True)
    cnt_ref[...] = jnp.broadcast_to(carry_ref[...], cnt_ref.shape).astype(jnp.int32)


def _router(u, wr_t, rb, tm):
    N, D = u.shape
    col = lambda i: (0, i)
    fixed = lambda i: (0, 0)
    return pl.pallas_call(
        functools.partial(_router_kernel, tm=tm),
        grid=(N // tm,),
        in_specs=[
            pl.BlockSpec((tm, D), lambda i: (i, 0)),
            pl.BlockSpec((N_EXPERTS, D), fixed),
            pl.BlockSpec((N_EXPERTS, 1), fixed),
        ],
        out_specs=[
            pl.BlockSpec((TOP_K, tm), col),
            pl.BlockSpec((TOP_K, tm), col),
            pl.BlockSpec((TOP_K, tm), col),
            pl.BlockSpec((N_EXPERTS, LANES), fixed),
        ],
        out_shape=[
            jax.ShapeDtypeStruct((TOP_K, N), jnp.int32),
            jax.ShapeDtypeStruct((TOP_K, N), F32),
            jax.ShapeDtypeStruct((TOP_K, N), jnp.int32),
            jax.ShapeDtypeStruct((N_EXPERTS, LANES), jnp.int32),
        ],
        scratch_shapes=[pltpu.VMEM((N_EXPERTS, 1), F32)],
        compiler_params=_cparams("arbitrary"),
        name="router",
    )(u, wr_t, rb)


def _experts_kernel(be_ref, live_ref, x_ref, wg_ref, wu_ref, wd_ref, o_ref, wgb_ref, wub_ref, wdb_ref):
    i = pl.program_id(0)
    e = be_ref[i]
    prev = be_ref[jnp.maximum(i - 1, 0)]

    @pl.when((i == 0) | (e != prev))
    def _():
        wgb_ref[...] = wg_ref[...].astype(BF16)
        wub_ref[...] = wu_ref[...].astype(BF16)
        wdb_ref[...] = wd_ref[...].astype(BF16)

    @pl.when(live_ref[i] > 0)
    def _():
        x = _from_row_tiles(x_ref[...]).astype(BF16)
        h = (_silu(_dot(x, wgb_ref[...])) * _dot(x, wub_ref[...])).astype(BF16)
        o_ref[...] = _to_row_tiles(_dot(h, wdb_ref[...]))

    @pl.when(live_ref[i] == 0)
    def _():
        o_ref[...] = jnp.zeros_like(o_ref)


EXPERT_ROWS = 512


def _experts(xs3, block_e, live, wg, wu, wd, layer, bm):
    R, S, _ = xs3.shape
    D = S * LANES
    return pl.pallas_call(
        _experts_kernel,
        grid_spec=pltpu.PrefetchScalarGridSpec(
            num_scalar_prefetch=2,
            grid=(R // bm,),
            in_specs=[
                pl.BlockSpec((bm, S, LANES), lambda i, be, lv: (i * lv[i], 0, 0)),
                pl.BlockSpec((None, None, D, D_EXPERT), lambda i, be, lv: (layer, be[i], 0, 0)),
                pl.BlockSpec((None, None, D, D_EXPERT), lambda i, be, lv: (layer, be[i], 0, 0)),
                pl.BlockSpec((None, None, D_EXPERT, D), lambda i, be, lv: (layer, be[i], 0, 0)),
            ],
            out_specs=pl.BlockSpec((bm, S, LANES), lambda i, be, lv: (i, 0, 0)),
            scratch_shapes=[
                pltpu.VMEM((D, D_EXPERT), BF16),
                pltpu.VMEM((D, D_EXPERT), BF16),
                pltpu.VMEM((D_EXPERT, D), BF16),
            ],
        ),
        out_shape=jax.ShapeDtypeStruct((R, S, LANES), F32),
        compiler_params=_cparams("arbitrary"),
        name="experts",
    )(block_e, live, xs3, wg, wu, wd)


ISSUE_TOKENS = 2


ZERO_ROWS = 128


def _dispatch_kernel(pad_end_ref, clear_from_ref, dest_ref, u_ref, *rest, tmd, bm, first):
    if first:
        xs_hbm, sem, zero_ref, zsem = rest

        @pl.when(pl.program_id(0) == 0)
        def _():
            zero_ref[...] = jnp.zeros_like(zero_ref)

            def clear_block(b, carry):
                for s in range(bm // ZERO_ROWS):
                    rows = pl.ds(b * bm + s * ZERO_ROWS, ZERO_ROWS)
                    pltpu.make_async_copy(zero_ref, xs_hbm.at[rows], zsem).start()
                return carry

            def block_cleared(b, carry):
                for s in range(bm // ZERO_ROWS):
                    pltpu.make_async_copy(zero_ref, xs_hbm.at[pl.ds(0, ZERO_ROWS)], zsem).wait()
                return carry

            def clear(e, carry):
                return lax.fori_loop(clear_from_ref[e] // bm, pad_end_ref[e] // bm, clear_block, carry)

            def cleared(e, carry):
                return lax.fori_loop(clear_from_ref[e] // bm, pad_end_ref[e] // bm, block_cleared, carry)

            first_unused = pad_end_ref[N_EXPERTS - 1] // bm
            n_blocks = xs_hbm.shape[0] // bm
            lax.fori_loop(0, N_EXPERTS, clear, 0)
            lax.fori_loop(first_unused, n_blocks, clear_block, 0)
            lax.fori_loop(0, N_EXPERTS, cleared, 0)
            lax.fori_loop(first_unused, n_blocks, block_cleared, 0)
    else:
        _, xs_hbm, sem = rest

    def issue(t2, carry):
        base = t2 * (ISSUE_TOKENS * TOP_K)
        rows = [dest_ref[base + r] for r in range(ISSUE_TOKENS * TOP_K)]
        for r, row in enumerate(rows):
            pltpu.make_async_copy(u_ref.at[t2 * ISSUE_TOKENS + r // TOP_K], xs_hbm.at[row], sem).start(priority=r % 2)
        return carry

    lax.fori_loop(0, tmd // ISSUE_TOKENS, issue, 0)
    for k in range(TOP_K):
        pltpu.make_async_copy(u_ref, xs_hbm.at[pl.ds(0, tmd)], sem).wait()


def _dispatch(u3, dest, xs_prev, pad_end, clear_from, R, tmd, bm):
    N, S, _ = u3.shape
    first = xs_prev is None
    in_specs = [
        pl.BlockSpec((tmd * TOP_K,), lambda i, pe, pd: (i,), memory_space=pltpu.SMEM),
        pl.BlockSpec((tmd, S, LANES), lambda i, pe, pd: (i, 0, 0)),
    ]
    scratch = [pltpu.SemaphoreType.DMA(())]
    if first:
        scratch += [pltpu.VMEM((ZERO_ROWS, S, LANES), F32), pltpu.SemaphoreType.DMA(())]
    else:
        in_specs.append(pl.BlockSpec(memory_space=pl.ANY))
    return pl.pallas_call(
        functools.partial(_dispatch_kernel, tmd=tmd, bm=bm, first=first),
        grid_spec=pltpu.PrefetchScalarGridSpec(
            num_scalar_prefetch=2,
            grid=(N // tmd,),
            in_specs=in_specs,
            out_specs=pl.BlockSpec(memory_space=pl.ANY),
            scratch_shapes=scratch,
        ),
        out_shape=jax.ShapeDtypeStruct((R, S, LANES), F32),
        input_output_aliases={} if first else {4: 0},
        compiler_params=_cparams("arbitrary"),
        name="moe_dispatch",
    )(pad_end, clear_from, dest, u3, *(() if first else (xs_prev,)))


def _combine_kernel(dest_ref, w_ref, y_hbm, o_ref, buf_ref, sem, *, tmc):
    def issue(t2, carry):
        first = t2 * (ISSUE_TOKENS * TOP_K)
        rows = [dest_ref[first + r] for r in range(ISSUE_TOKENS * TOP_K)]
        for r, row in enumerate(rows):
            pltpu.make_async_copy(y_hbm.at[row], buf_ref.at[first + r], sem).start(priority=r % 2)
        return carry

    lax.fori_loop(0, tmc // ISSUE_TOKENS, issue, 0)
    pltpu.make_async_copy(y_hbm.at[pl.ds(0, tmc * TOP_K)], buf_ref, sem).wait()

    def reduce(t, carry):
        acc = w_ref[t * TOP_K] * buf_ref[t * TOP_K]
        for k in range(1, TOP_K):
            acc = acc + w_ref[t * TOP_K + k] * buf_ref[t * TOP_K + k]
        o_ref[t] = acc
        return carry

    lax.fori_loop(0, tmc, reduce, 0)


def _combine(y3, dest, w, N, tmc):
    _, S, _ = y3.shape
    return pl.pallas_call(
        functools.partial(_combine_kernel, tmc=tmc),
        grid=(N // tmc,),
        in_specs=[
            pl.BlockSpec((tmc * TOP_K,), lambda i: (i,), memory_space=pltpu.SMEM),
            pl.BlockSpec((tmc * TOP_K,), lambda i: (i,), memory_space=pltpu.SMEM),
            pl.BlockSpec(memory_space=pl.ANY),
        ],
        out_specs=pl.BlockSpec((tmc, S, LANES), lambda i: (i, 0, 0)),
        out_shape=jax.ShapeDtypeStruct((N, S, LANES), F32),
        scratch_shapes=[pltpu.VMEM((tmc * TOP_K, S, LANES), F32), pltpu.SemaphoreType.DMA(())],
        compiler_params=_cparams("arbitrary"),
        name="moe_combine",
    )(dest, w, y3)


def _moe_final_kernel(u_ref, x_ref, r_ref, wgu_ref, wd_ref, gate_ref, lg_ref, lb_ref, xo_ref):
    gu = _dot(u_ref[...], wgu_ref[...])
    h = (_silu(gu[:, :D_SHARED]) * gu[:, D_SHARED:]).astype(BF16)
    y = _from_row_tiles(r_ref[...]) + _dot(h, wd_ref[...])
    hres = DEEPNORM_ALPHA * x_ref[...] + (1.0 + gate_ref[0]) * y
    xo_ref[...] = _layer_norm(hres, lg_ref[...], lb_ref[...])


def _moe_final(u, x, routed, wgu, wd, gate, lg, lb, T, tm):
    N, D = x.shape
    g3, g_spec = _row_mod(gate, T, tm)
    row = lambda i: (i, 0)
    fixed = lambda i: (0, 0)
    return pl.pallas_call(
        _moe_final_kernel,
        grid=(N // tm,),
        in_specs=[
            pl.BlockSpec((tm, D), row),
            pl.BlockSpec((tm, D), row),
            pl.BlockSpec((tm, D // LANES, LANES), lambda i: (i, 0, 0)),
            pl.BlockSpec(wgu.shape, fixed),
            pl.BlockSpec(wd.shape, fixed),
            g_spec,
            pl.BlockSpec((1, D), fixed),
            pl.BlockSpec((1, D), fixed),
        ],
        out_specs=pl.BlockSpec((tm, D), row),
        out_shape=jax.ShapeDtypeStruct((N, D), F32),
        compiler_params=_cparams("parallel"),
        name="moe_final",
    )(u, x, routed, wgu, wd, g3, lg.reshape(1, D), lb.reshape(1, D))


LOG2E = 1.4426950408889634


def _fox_kernel(qi_ref, kj_ref, q_ref, k_ref, v_ref, fq_ref, fk_ref, o_ref, m_ref, l_ref, acc_ref, *, bq):
    i = qi_ref[pl.program_id(1)]
    j = kj_ref[pl.program_id(1)]

    @pl.when(j == 0)
    def _():
        m_ref[...] = jnp.full_like(m_ref, MASK_NEG)
        l_ref[...] = jnp.zeros_like(l_ref)
        acc_ref[...] = jnp.zeros_like(acc_ref)

    def block(diagonal):
        if diagonal:
            visible = (lax.broadcasted_iota(jnp.int32, (bq, bq), 0) <= lax.broadcasted_iota(jnp.int32, (bq, bq), 1))
        qk_scale = FOX_HD ** -0.5 * LOG2E
        fq2 = fq_ref[...] * LOG2E
        fk2 = fk_ref[...] * LOG2E
        for h in range(FOX_HEADS):
            hs = slice(h * FOX_HD, (h + 1) * FOX_HD)
            s = _dot_nt(k_ref[:, hs], q_ref[:, hs]) * qk_scale + (fq2[h:h + 1, :] - fk2[:, h:h + 1])
            if diagonal:
                s = jnp.where(visible, s, MASK_NEG)
            m_prev = m_ref[h:h + 1, :]
            m_new = jnp.maximum(m_prev, jnp.max(s, axis=0, keepdims=True))
            alpha = jnp.exp2(m_prev - m_new)
            p = jnp.exp2(s - m_new)
            l_ref[h:h + 1, :] = alpha * l_ref[h:h + 1, :] + jnp.sum(p, axis=0, keepdims=True)
            acc_ref[hs, :] = alpha * acc_ref[hs, :] + _dot_tn(v_ref[:, hs], p.astype(BF16))
            m_ref[h:h + 1, :] = m_new

    pl.when(j < i)(lambda: block(False))

    @pl.when(j == i)
    def _():
        block(True)
        for h in range(FOX_HEADS):
            hs = slice(h * FOX_HD, (h + 1) * FOX_HD)
            o_ref[:, hs] = (acc_ref[hs, :] / l_ref[h:h + 1, :]).T.astype(o_ref.dtype)


def _fox_attention(q, kb, vb, cum, cum_t, B, T, bq):
    nq = T // bq
    pairs = [(i, j) for i in range(nq) for j in range(i + 1)]
    qi = jnp.array([p[0] for p in pairs], jnp.int32)
    kj = jnp.array([p[1] for p in pairs], jnp.int32)
    q_map = lambda b, s, qi, kj: (b * nq + qi[s], 0)
    kv_map = lambda b, s, qi, kj: (b * nq + kj[s], 0)
    return pl.pallas_call(
        functools.partial(_fox_kernel, bq=bq),
        grid_spec=pltpu.PrefetchScalarGridSpec(
            num_scalar_prefetch=2,
            grid=(B, len(pairs)),
            in_specs=[
                pl.BlockSpec((bq, FOX_DIM), q_map),
                pl.BlockSpec((bq, FOX_DIM), kv_map),
                pl.BlockSpec((bq, FOX_DIM), kv_map),
                pl.BlockSpec((FOX_HEADS, bq), lambda b, s, qi, kj: (b, qi[s])),
                pl.BlockSpec((bq, LANES), kv_map),
            ],
            out_specs=pl.BlockSpec((bq, FOX_DIM), q_map),
            scratch_shapes=[
                pltpu.VMEM((FOX_HEADS, bq), F32),
                pltpu.VMEM((FOX_HEADS, bq), F32),
                pltpu.VMEM((FOX_DIM, bq), F32),
            ],
        ),
        out_shape=jax.ShapeDtypeStruct((B * T, FOX_DIM), BF16),
        compiler_params=_cparams("parallel", "arbitrary"),
        name="fox_attention",
    )(qi, kj, q, kb, vb, cum_t, cum)


def _decode_kernel(pt_ref, q_ref, kn_ref, vn_ref, fq_ref, fkn_ref, *rest, tq, npg):
    kc, vc, lfc = rest[:npg], rest[npg:2 * npg], rest[2 * npg:3 * npg]
    o_ref, m_ref, l_ref, acc_ref, carry_ref = rest[3 * npg:]
    j = pl.program_id(1)
    H = FOX_HEADS
    scale = FOX_HD ** -0.5
    q = [q_ref[:, h * FOX_HD:(h + 1) * FOX_HD].astype(BF16) for h in range(H)]
    fq = jnp.concatenate([fq_ref[:, h:h + 1] for h in range(H)], axis=0)

    def per_head_rows(x):
        return jnp.concatenate([jnp.broadcast_to(x[h:h + 1, :], (tq, x.shape[1])) for h in range(H)], axis=0)

    def attend(s, values, state):
        m_prev, l_prev, acc_prev = state
        m_new = jnp.maximum(m_prev, jnp.max(s, axis=1, keepdims=True))
        alpha = jnp.exp(m_prev - m_new)
        p = jnp.exp(s - m_new)
        l_new = alpha * l_prev + jnp.sum(p, axis=1, keepdims=True)
        pv = []
        for h in range(H):
            o = None
            for i, value in enumerate(values):
                ph = p[h * tq:(h + 1) * tq, i * PAGE_SIZE:(i + 1) * PAGE_SIZE].astype(BF16)
                t = _dot(ph, value(h))
                o = t if o is None else o + t
            pv.append(o)
        return m_new, l_new, alpha * acc_prev + jnp.concatenate(pv, axis=0)

    @pl.when(j == 0)
    def _():
        carry_ref[...] = jnp.zeros_like(carry_ref)
        r = lax.broadcasted_iota(jnp.int32, (H * tq, PAGE_SIZE), 0) % tq
        c = lax.broadcasted_iota(jnp.int32, (H * tq, PAGE_SIZE), 1)
        s = jnp.concatenate(
            [_dot_nt(q[h], kn_ref[0, :, h * FOX_HD:(h + 1) * FOX_HD]) for h in range(H)], axis=0)
        s = s * scale + fq - per_head_rows(fkn_ref[0])
        s = jnp.where(c <= r, s, MASK_NEG)
        state = (jnp.full((H * tq, 1), MASK_NEG, F32), jnp.zeros((H * tq, 1), F32),
                 jnp.zeros((H * tq, FOX_HD), F32))
        m_ref[...], l_ref[...], acc_ref[...] = attend(
            s, [lambda h: vn_ref[0, :, h * FOX_HD:(h + 1) * FOX_HD]], state)

    r = lax.broadcasted_iota(jnp.int32, (PAGE_SIZE, PAGE_SIZE), 0)
    c = lax.broadcasted_iota(jnp.int32, (PAGE_SIZE, PAGE_SIZE), 1)
    after = (r > c).astype(BF16)
    carry = carry_ref[...]
    blocks = []
    for i in range(npg):
        lf = lfc[i][...]
        suffix = _dot_exact_rhs(lf, after) + carry
        carry = carry + jnp.sum(lf, axis=1, keepdims=True)
        kt = jnp.swapaxes(kc[i][...], 0, 1)
        sk = jnp.concatenate([_dot_nt(q[h], kt[h].astype(BF16)) for h in range(H)], axis=0)
        blocks.append(sk * scale + per_head_rows(suffix))
    carry_ref[...] = carry
    s = jnp.concatenate(blocks, axis=1) + fq
    vts = [jnp.swapaxes(vc[i][...], 0, 1) for i in range(npg)]
    values = [lambda h, i=i: vts[i][h].astype(BF16) for i in range(npg)]
    m_new, l_new, acc_new = attend(s, values, (m_ref[...], l_ref[...], acc_ref[...]))
    m_ref[...] = m_new
    l_ref[...] = l_new
    acc_ref[...] = acc_new

    @pl.when(j == pl.num_programs(1) - 1)
    def _():
        out = acc_new / l_new
        for h in range(H):
            o_ref[:, h * FOX_HD:(h + 1) * FOX_HD] = out[h * tq:(h + 1) * tq, :]


DECODE_PAGES_PER_STEP = 8


def _decode_attention(page_table, q, k_new, v_new, cum, cum_new_t, cache_k, cache_v, cache_lf_t, B, tq):
    n_pages = page_table.shape[1]
    npg = DECODE_PAGES_PER_STEP
    assert n_pages % npg == 0

    def page(i, nd):
        return lambda b, j, pt: (pt[b * n_pages + (n_pages - 1 - (j * npg + i))],) + (0,) * nd

    kv_specs = [pl.BlockSpec((None, PAGE_SIZE, FOX_HEADS, FOX_HD), page(i, 3)) for i in range(npg)]
    lf_specs = [pl.BlockSpec((None, FOX_HEADS, PAGE_SIZE), page(i, 2)) for i in range(npg)]
    return pl.pallas_call(
        functools.partial(_decode_kernel, tq=tq, npg=npg),
        grid_spec=pltpu.PrefetchScalarGridSpec(
            num_scalar_prefetch=1,
            grid=(B, n_pages // npg),
            in_specs=[
                pl.BlockSpec((tq, FOX_DIM), lambda b, j, pt: (b, 0)),
                pl.BlockSpec((1, PAGE_SIZE, FOX_DIM), lambda b, j, pt: (b, 0, 0)),
                pl.BlockSpec((1, PAGE_SIZE, FOX_DIM), lambda b, j, pt: (b, 0, 0)),
                pl.BlockSpec((tq, LANES), lambda b, j, pt: (b, 0)),
                pl.BlockSpec((1, FOX_HEADS, PAGE_SIZE), lambda b, j, pt: (b, 0, 0)),
            ] + kv_specs + kv_specs + lf_specs,
            out_specs=pl.BlockSpec((tq, FOX_DIM), lambda b, j, pt: (b, 0)),
            scratch_shapes=[
                pltpu.VMEM((FOX_HEADS * tq, 1), F32),
                pltpu.VMEM((FOX_HEADS * tq, 1), F32),
                pltpu.VMEM((FOX_HEADS * tq, FOX_HD), F32),
                pltpu.VMEM((FOX_HEADS, 1), F32),
            ],
        ),
        out_shape=jax.ShapeDtypeStruct((B * tq, FOX_DIM), F32),
        compiler_params=_cparams("parallel", "arbitrary"),
        name="decode_attention",
    )(page_table.reshape(-1), q, k_new, v_new, cum, cum_new_t, *([cache_k] * npg), *([cache_v] * npg),
      *([cache_lf_t] * npg))


def _moe(groups, W, l):
    bm = EXPERT_ROWS
    routes = [_router(g["u"], W["router_wt"][l], W["router_bias"][l].reshape(N_EXPERTS, 1), g["tm"])
              for g in groups]
    counts = [r[3][:, 0] for r in routes]
    total = sum(counts)
    padded = (total + bm - 1) // bm * bm
    pad_end = jnp.cumsum(padded)
    n_blocks = -(-sum(g["N"] for g in groups) * TOP_K // bm) + N_EXPERTS
    starts = jnp.arange(n_blocks, dtype=jnp.int32) * bm
    block_e = jnp.minimum(jnp.sum(pad_end[None, :] <= starts[:, None], axis=1), N_EXPERTS - 1).astype(jnp.int32)
    live = (starts < pad_end[-1]).astype(jnp.int32)
    offset = pad_end - padded
    clear_from = (offset + counts[0]) // bm * bm
    xs3, dests = None, []
    for g, (idx_t, _, rank_t, _), cnt in zip(groups, routes, counts):
        first = jnp.sum(jnp.where(idx_t[:, :, None] == jnp.arange(N_EXPERTS), offset, 0), axis=-1)
        dests.append((first + rank_t).T.reshape(-1))
        xs3 = _dispatch(g["u3"], dests[-1], xs3, pad_end, clear_from, n_blocks * bm, min(256, g["N"]), bm)
        offset = offset + cnt
    y3 = _experts(xs3, block_e, live, W["exp_w_gate"], W["exp_w_up"], W["exp_w_down"], l, bm)
    for g, (_, w_t, _, _), dest in zip(groups, routes, dests):
        routed3 = _combine(y3, dest, w_t.T.reshape(-1), g["N"], min(128, g["N"]))
        g["x"] = _moe_final(g["u"], g["x"], routed3, W["sh_wgu"][l], W["sh_wd"][l], g["g_m"], W["ln_g"][l, 1],
                            W["ln_b"][l, 1], g["T"], g["tm"])


def _mixer(g, W, l):
    B, T, N, tm, past = g["B"], g["T"], g["N"], g["tm"], g["past"]
    sh_a, sc_a, g_a, sh_m, sc_m, g_m = jnp.split(g["mods"][l], 6, axis=-1)
    if l < N_A:
        p = _modproj(g["x"], sc_a, sh_a, W["gla_w_in"][l], T, tm, F32)
        if T >= GLA_CHUNK:
            Tp, tc = T, min(512, T)
        else:
            Tp = tc = GLA_CHUNK
            p = jnp.pad(p.reshape(B, T, -1), ((0, 0), (0, Tp - T), (0, 0))).reshape(B * Tp, -1)
        o, s_T = _gla(p, W["gla_w_a2"][l], W["gla_b_a"][l], W["gla_norm_g"][l], g["s0"][l], B, Tp, tc, T)
        if Tp != T:
            o = o.reshape(B, Tp, -1)[:, :T].reshape(N, -1)
        g["gla"].append(s_T)
        w_o = W["gla_w_o"][l]
    else:
        q = _modproj(g["x"], sc_a, sh_a, W["fox_w_q"][l - N_A], T, tm, BF16 if past is None else F32)
        if past is None:
            o = _fox_attention(q, g["kb"], g["vb"], g["cum"], g["cum_t"], B, T, min(256, T))
        else:
            o = _decode_attention(past["page_table"], q, g["kb"], g["vb"], g["cum"], g["cum_t"], past["k"],
                                  past["v"], past["lf_t"], B, T)
        w_o = W["fox_w_o"][l - N_A]
    g["x"], g["u"], g["u3"] = _proj_res_ln(o, w_o, g["x"], g_a, W["ln_g"][l, 0], W["ln_b"][l, 0], sc_m, sh_m, T, tm)
    g["g_m"] = g_m


def _shared_kv(g, W):
    B, T = g["B"], g["T"]
    sh_kv, sc_kv = jnp.split(g["kv_mod"], 2, axis=-1)
    g["k"], g["v"], kb, vb, g["lf"], g["cum"] = _kv_proj(g["x"], sc_kv, sh_kv, W["kv_w"], W["fox_b_f"], T, g["tm"])
    cum_h = g["cum"][:, :FOX_HEADS].reshape(B, T, FOX_HEADS)
    if g["past"] is None:
        g["cum_t"] = cum_h.transpose(0, 2, 1).reshape(B * FOX_HEADS, T)
        g["kb"], g["vb"] = kb, vb
    else:
        rows = ((0, 0), (0, PAGE_SIZE - T), (0, 0))
        g["cum_t"] = jnp.pad(cum_h, rows).transpose(0, 2, 1)
        g["kb"] = jnp.pad(kb.reshape(B, T, FOX_DIM), rows)
        g["vb"] = jnp.pad(vb.reshape(B, T, FOX_DIM), rows)


def _group(x, mods, kv_mod, s0, past):
    B, T, D = x.shape
    return dict(x=x.reshape(B * T, D), B=B, T=T, N=B * T, tm=min(512, B * T), mods=mods, kv_mod=kv_mod, s0=s0,
                past=past, gla=[])


def _group_outputs(g):
    B, T = g["B"], g["T"]
    return (g["x"].reshape(B, T, D_MODEL), jnp.stack(g["gla"]),
            g["k"].reshape(B, T, FOX_HEADS, FOX_HD), g["v"].reshape(B, T, FOX_HEADS, FOX_HD),
            g["lf"][:, :FOX_HEADS].reshape(B, T, FOX_HEADS))


def kernel(x_prompt, x_sample, c_prompt, c_sample, state_gla, cache_k, cache_v, cache_logf, page_table, gla_w_in, gla_w_a2, gla_b_a, gla_norm_g, gla_w_o, kv_w_mod, kv_b_mod, kv_w, fox_b_f, fox_w_q, fox_w_o, ada_w, ada_b, ln_g, ln_b, router_w, router_bias, exp_w_gate, exp_w_up, exp_w_down, sh_w_gate, sh_w_up, sh_w_down):
    Bp, Tp, D = x_prompt.shape
    Bs, Ts, _ = x_sample.shape
    lane_pad = lambda w, n: jnp.pad(w, [(0, 0)] * (w.ndim - 1) + [(0, n - w.shape[-1])])
    W = dict(
        gla_w_in=jnp.concatenate(
            [gla_w_in[..., :GLA_IN_MAIN], lane_pad(gla_w_in[..., GLA_IN_MAIN:], LANES)], axis=-1).astype(BF16),
        gla_w_a2=jnp.pad(gla_w_a2, ((0, 0), (0, LANES - GLA_GATE_RANK), (0, 0))).astype(BF16),
        gla_b_a=gla_b_a.reshape(N_A, 1, GLA_DK),
        gla_norm_g=gla_norm_g.reshape(N_A, 1, GLA_DVH),
        gla_w_o=gla_w_o.astype(BF16),
        kv_w=jnp.concatenate([kv_w[:, :2 * FOX_DIM], lane_pad(kv_w[:, 2 * FOX_DIM:], LANES)], axis=-1).astype(BF16),
        fox_b_f=lane_pad(fox_b_f.reshape(1, FOX_HEADS), LANES),
        fox_w_q=fox_w_q.astype(BF16),
        fox_w_o=fox_w_o.astype(BF16),
        ln_g=ln_g, ln_b=ln_b,
        router_wt=jnp.swapaxes(router_w, 1, 2).astype(BF16),
        router_bias=router_bias,
        exp_w_gate=exp_w_gate, exp_w_up=exp_w_up, exp_w_down=exp_w_down,
        sh_wgu=jnp.concatenate([sh_w_gate, sh_w_up], axis=-1).astype(BF16),
        sh_wd=sh_w_down.astype(BF16),
    )
    c_all = jnp.concatenate([c_prompt, c_sample], axis=0)
    mods = _cond_linear(c_all, ada_w, ada_b, 1536)
    kv_mod = _cond_linear(c_all, kv_w_mod[None], kv_b_mod[None], 1024)[0]

    s0_prompt = jnp.zeros((N_A, Bp, GLA_HEADS, GLA_DKH, GLA_DVH), F32)
    past = dict(page_table=page_table, k=cache_k, v=cache_v, lf_t=jnp.swapaxes(cache_logf, 1, 2))
    groups = [_group(x_prompt, mods[:, :Bp], kv_mod[:Bp], s0_prompt, None),
              _group(x_sample, mods[:, Bp:], kv_mod[Bp:], state_gla, past)]
    for l in range(DEPTH):
        for g in groups:
            _mixer(g, W, l)
        _moe(groups, W, l)
        if l == N_A - 1:
            for g in groups:
                _shared_kv(g, W)
    y_p, gla_p, k_p, v_p, lf_p = _group_outputs(groups[0])
    y_s, gla_s, k_s, v_s, lf_s = _group_outputs(groups[1])
    return (y_p, y_s, gla_p, gla_s, k_p, v_p, lf_p, k_s, v_s, lf_s)
```

```python
import functools

import jax
import jax.numpy as jnp
from jax import lax
from jax.experimental import pallas as pl
from jax.experimental.pallas import tpu as pltpu

F32 = jnp.float32
BF16 = jnp.bfloat16

D_MODEL = 1024
DEPTH = 4
N_A = DEPTH // 2
PAGE_SIZE = 128
GLA_HEADS = 4
GLA_DK = D_MODEL // 2
GLA_DV = D_MODEL
GLA_DKH = GLA_DK // GLA_HEADS
GLA_DVH = GLA_DV // GLA_HEADS
GLA_GATE_RANK = 16
GLA_GATE_TAU = 16.0
GLA_CHUNK = 64
FOX_HEADS = 8
FOX_HD = D_MODEL // FOX_HEADS
FOX_DIM = FOX_HEADS * FOX_HD
N_EXPERTS = 64
TOP_K = 8
N_GROUPS = 8
GROUP_SIZE = N_EXPERTS // N_GROUPS
TOPK_GROUPS = 4
D_EXPERT = 256
D_SHARED = 256
ROUTED_SCALE = 2.5
DEEPNORM_ALPHA = (2 * DEPTH) ** 0.25
LN_EPS = 1e-5
RMS_EPS = 1e-6

LANES = 128
GLA_IN_MAIN = 2 * GLA_DK + 2 * GLA_DV
GLA_IN_PAD = GLA_IN_MAIN + LANES
KV_PAD = 2 * FOX_DIM + LANES
MASK_NEG = -1e30
VMEM_LIMIT = 52 * 1024 * 1024


def _cparams(*sem):
    return pltpu.CompilerParams(dimension_semantics=sem, vmem_limit_bytes=VMEM_LIMIT)


def _dot(a, b):
    return jnp.dot(a, b, preferred_element_type=F32)


def _dot_nt(a, b):
    return lax.dot_general(a, b, (((1,), (1,)), ((), ())), preferred_element_type=F32)


def _dot_tn(a, b):
    return lax.dot_general(a, b, (((0,), (0,)), ((), ())), preferred_element_type=F32)


def _sigmoid(x):
    return 1.0 / (1.0 + jnp.exp(-x))


def _silu(x):
    return x * _sigmoid(x)


def _log_sigmoid(x):
    return jnp.minimum(x, 0.0) - jnp.log1p(jnp.exp(-jnp.abs(x)))


def _split3(x):
    hi = x.astype(BF16)
    r = x - hi.astype(F32)
    mid = r.astype(BF16)
    lo = (r - mid.astype(F32)).astype(BF16)
    return hi, mid, lo


def _dot_exact_lhs(mat01, x):
    hi, mid, lo = _split3(x)
    return _dot(mat01, hi) + _dot(mat01, mid) + _dot(mat01, lo)


def _dot_exact_rhs(x, mat01):
    hi, mid, lo = _split3(x)
    return _dot(hi, mat01) + _dot(mid, mat01) + _dot(lo, mat01)


def _layer_norm(h, g, b):
    mu = jnp.mean(h, axis=-1, keepdims=True)
    xc = h - mu
    var = jnp.mean(xc * xc, axis=-1, keepdims=True)
    return xc * lax.rsqrt(var + LN_EPS) * g + b


def _row_mod(m, T, tm):
    B, D = m.shape
    if T % tm == 0:
        tiles_per_batch = T // tm
        return m.reshape(B, 1, D), pl.BlockSpec((1, 1, D), lambda i: (i // tiles_per_batch, 0, 0))
    assert tm % T == 0
    e = jnp.repeat(m, T, axis=0).reshape(B * T // tm, tm, D)
    return e, pl.BlockSpec((1, tm, D), lambda i: (i, 0, 0))


def _cond_kernel(c_ref, w_ref, b_ref, o_ref):
    s = _silu(c_ref[...]).astype(BF16)
    o_ref[0] = _dot(s, w_ref[0].astype(BF16)) + b_ref[0]


def _cond_linear(c, w, b, tn):
    M, D = c.shape
    L, _, N = w.shape
    return pl.pallas_call(
        _cond_kernel,
        grid=(L, N // tn),
        in_specs=[
            pl.BlockSpec((M, D), lambda l, j: (0, 0)),
            pl.BlockSpec((1, D, tn), lambda l, j: (l, 0, j)),
            pl.BlockSpec((1, 1, tn), lambda l, j: (l, 0, j)),
        ],
        out_specs=pl.BlockSpec((1, M, tn), lambda l, j: (l, 0, j)),
        out_shape=jax.ShapeDtypeStruct((L, M, N), F32),
        compiler_params=_cparams("parallel", "parallel"),
        name="cond_linear",
    )(c, w, b.reshape(L, 1, N))


def _modproj_kernel(x_ref, sc_ref, sh_ref, w_ref, o_ref):
    u = (x_ref[...] * (1.0 + sc_ref[0]) + sh_ref[0]).astype(BF16)
    o_ref[...] = _dot(u, w_ref[...]).astype(o_ref.dtype)


def _modproj(x, sc, sh, w, T, tm, out_dtype):
    N, D = x.shape
    Dout = w.shape[1]
    sc3, sc_spec = _row_mod(sc, T, tm)
    sh3, sh_spec = _row_mod(sh, T, tm)
    return pl.pallas_call(
        _modproj_kernel,
        grid=(N // tm,),
        in_specs=[
            pl.BlockSpec((tm, D), lambda i: (i, 0)),
            sc_spec,
            sh_spec,
            pl.BlockSpec((D, Dout), lambda i: (0, 0)),
        ],
        out_specs=pl.BlockSpec((tm, Dout), lambda i: (i, 0)),
        out_shape=jax.ShapeDtypeStruct((N, Dout), out_dtype),
        compiler_params=_cparams("parallel"),
        name="modproj",
    )(x, sc3, sh3, w)


def _kv_kernel(x_ref, sc_ref, sh_ref, w_ref, bf_ref, k_ref, v_ref, kb_ref, vb_ref, lf_ref, cum_ref,
               carry_ref, *, T, tm):
    u = (x_ref[...] * (1.0 + sc_ref[0]) + sh_ref[0]).astype(BF16)
    p = _dot(u, w_ref[...])
    k = p[:, :FOX_DIM]
    v = p[:, FOX_DIM:2 * FOX_DIM]
    k_ref[...] = k
    v_ref[...] = v
    kb_ref[...] = k.astype(BF16)
    vb_ref[...] = v.astype(BF16)
    lf = _log_sigmoid(p[:, 2 * FOX_DIM:] + bf_ref[...])
    lf_ref[...] = lf
    if T % tm == 0:
        @pl.when(pl.program_id(0) % (T // tm) == 0)
        def _():
            carry_ref[...] = jnp.zeros_like(carry_ref)

        sb = LANES
        r = lax.broadcasted_iota(jnp.int32, (sb, sb), 0)
        c = lax.broadcasted_iota(jnp.int32, (sb, sb), 1)
        tri = (c <= r).astype(BF16)
        carry = carry_ref[...]
        for s in range(tm // sb):
            cs = _dot_exact_lhs(tri, lf[s * sb:(s + 1) * sb]) + carry
            cum_ref[s * sb:(s + 1) * sb, :] = cs
            carry = cs[sb - 1:sb, :]
        carry_ref[...] = carry
    else:
        r = lax.broadcasted_iota(jnp.int32, (tm, tm), 0)
        c = lax.broadcasted_iota(jnp.int32, (tm, tm), 1)
        tri = ((c <= r) & ((r // T) == (c // T))).astype(BF16)
        cum_ref[...] = _dot_exact_lhs(tri, lf)


def _kv_proj(x, sc, sh, w, bf, T, tm):
    N, D = x.shape
    sc3, sc_spec = _row_mod(sc, T, tm)
    sh3, sh_spec = _row_mod(sh, T, tm)
    row = lambda i: (i, 0)
    return pl.pallas_call(
        functools.partial(_kv_kernel, T=T, tm=tm),
        grid=(N // tm,),
        in_specs=[
            pl.BlockSpec((tm, D), row),
            sc_spec,
            sh_spec,
            pl.BlockSpec((D, KV_PAD), lambda i: (0, 0)),
            pl.BlockSpec((1, LANES), lambda i: (0, 0)),
        ],
        out_specs=[
            pl.BlockSpec((tm, FOX_DIM), row),
            pl.BlockSpec((tm, FOX_DIM), row),
            pl.BlockSpec((tm, FOX_DIM), row),
            pl.BlockSpec((tm, FOX_DIM), row),
            pl.BlockSpec((tm, LANES), row),
            pl.BlockSpec((tm, LANES), row),
        ],
        out_shape=[
            jax.ShapeDtypeStruct((N, FOX_DIM), F32),
            jax.ShapeDtypeStruct((N, FOX_DIM), F32),
            jax.ShapeDtypeStruct((N, FOX_DIM), BF16),
            jax.ShapeDtypeStruct((N, FOX_DIM), BF16),
            jax.ShapeDtypeStruct((N, LANES), F32),
            jax.ShapeDtypeStruct((N, LANES), F32),
        ],
        scratch_shapes=[pltpu.VMEM((1, LANES), F32)],
        compiler_params=_cparams("arbitrary"),
        name="kv_proj",
    )(x, sc3, sh3, w, bf)


def _gla_kernel(q_ref, k_ref, v_ref, g_ref, a_ref, wa_ref, ba_ref, ng_ref, s0_ref, o_ref, st_ref,
                state_ref, *, c, nsub, t_valid, hps):
    t = pl.program_id(2)

    @pl.when(t == 0)
    def _():
        for hh in range(hps):
            state_ref[hh] = s0_ref[0, hh].T

    tc = c * nsub
    r = lax.broadcasted_iota(jnp.int32, (tc, tc), 0)
    cc = lax.broadcasted_iota(jnp.int32, (tc, tc), 1)
    causal = (cc <= r) & ((r // c) == (cc // c))
    tri = causal.astype(BF16)
    scale = GLA_DKH ** -0.5
    z_all = _dot(a_ref[...].astype(BF16), wa_ref[...]) + ba_ref[...]
    finals = []
    for hh in range(hps):
        ks = slice(hh * GLA_DKH, (hh + 1) * GLA_DKH)
        vs = slice(hh * GLA_DVH, (hh + 1) * GLA_DVH)
        la = _log_sigmoid(z_all[:, ks]) / GLA_GATE_TAU
        if t_valid < c:
            la = jnp.where(lax.broadcasted_iota(jnp.int32, la.shape, 0) % c < t_valid, la, 0.0)
        bc = _dot_exact_lhs(tri, la)
        b_last = [bc[(i + 1) * c - 1:(i + 1) * c, :] for i in range(nsub)]
        b_end = jnp.concatenate([jnp.broadcast_to(b, (c, b.shape[1])) for b in b_last], axis=0)
        k = k_ref[:, ks]
        v = v_ref[:, vs].astype(BF16)
        q_dec = ((q_ref[:, ks] * scale) * jnp.exp(bc)).astype(BF16)
        k_dec = (k * jnp.exp(-bc)).astype(BF16)
        k_end = (k * jnp.exp(b_end - bc)).astype(BF16)
        att = jnp.where(causal, _dot_nt(q_dec, k_dec), 0.0).astype(BF16)
        intra = _dot(att, v)
        st = state_ref[hh]
        inter = []
        for i in range(nsub):
            sl = slice(i * c, (i + 1) * c)
            inter.append(_dot_nt(q_dec[sl], st.astype(BF16)))
            st = st * jnp.exp(b_last[i]) + _dot_tn(v[sl], k_end[sl])
        state_ref[hh] = st
        finals.append(st)
        o = jnp.concatenate(inter, axis=0) + intra
        o = o * lax.rsqrt(jnp.mean(o * o, axis=-1, keepdims=True) + RMS_EPS) * ng_ref[...]
        o_ref[:, vs] = (o * _silu(g_ref[:, vs])).astype(o_ref.dtype)

    @pl.when(t == pl.num_programs(2) - 1)
    def _():
        for hh in range(hps):
            st_ref[0, hh] = finals[hh].T


def _gla(p, wa, ba, ng, s0, B, Tp, tc, t_valid, hps):
    nt = Tp // tc
    c = GLA_CHUNK
    row = lambda b, h, t: b * nt + t
    dk, dv = hps * GLA_DKH, hps * GLA_DVH
    kq = GLA_DK // dk
    kv = 2 * GLA_DK // dv
    kg = kv + GLA_HEADS // hps
    ka = GLA_IN_MAIN // LANES
    return pl.pallas_call(
        functools.partial(_gla_kernel, c=c, nsub=tc // c, t_valid=t_valid, hps=hps),
        grid=(B, GLA_HEADS // hps, nt),
        in_specs=[
            pl.BlockSpec((tc, dk), lambda b, h, t: (row(b, h, t), h)),
            pl.BlockSpec((tc, dk), lambda b, h, t: (row(b, h, t), kq + h)),
            pl.BlockSpec((tc, dv), lambda b, h, t: (row(b, h, t), kv + h)),
            pl.BlockSpec((tc, dv), lambda b, h, t: (row(b, h, t), kg + h)),
            pl.BlockSpec((tc, LANES), lambda b, h, t: (row(b, h, t), ka)),
            pl.BlockSpec((LANES, dk), lambda b, h, t: (0, h)),
            pl.BlockSpec((1, dk), lambda b, h, t: (0, h)),
            pl.BlockSpec((1, GLA_DVH), lambda b, h, t: (0, 0)),
            pl.BlockSpec((1, hps, GLA_DKH, GLA_DVH), lambda b, h, t: (b, h, 0, 0)),
        ],
        out_specs=[
            pl.BlockSpec((tc, dv), lambda b, h, t: (row(b, h, t), h)),
            pl.BlockSpec((1, hps, GLA_DKH, GLA_DVH), lambda b, h, t: (b, h, 0, 0)),
        ],
        out_shape=[
            jax.ShapeDtypeStruct((B * Tp, GLA_DV), BF16),
            jax.ShapeDtypeStruct((B, GLA_HEADS, GLA_DKH, GLA_DVH), F32),
        ],
        scratch_shapes=[pltpu.VMEM((hps, GLA_DVH, GLA_DKH), F32)],
        compiler_params=_cparams("parallel", "parallel", "arbitrary"),
        name="gla",
    )(p, p, p, p, p, wa, ba, ng, s0)


def _to_row_tiles(x):
    parts = [x[:, c * LANES:(c + 1) * LANES] for c in range(x.shape[1] // LANES)]
    return jnp.swapaxes(jnp.stack(parts, axis=0), 0, 1)


def _from_row_tiles(x3):
    xt = jnp.swapaxes(x3, 0, 1)
    return jnp.concatenate([xt[c] for c in range(xt.shape[0])], axis=1)


def _proj_res_ln_kernel(a_ref, w_ref, x_ref, gate_ref, lg_ref, lb_ref, sc_ref, sh_ref, xo_ref, uo_ref, u3_ref):
    y = _dot(a_ref[...].astype(BF16), w_ref[...])
    h = DEEPNORM_ALPHA * x_ref[...] + (1.0 + gate_ref[0]) * y
    xn = _layer_norm(h, lg_ref[...], lb_ref[...])
    xo_ref[...] = xn
    u = xn * (1.0 + sc_ref[0]) + sh_ref[0]
    uo_ref[...] = u.astype(uo_ref.dtype)
    u3_ref[...] = _to_row_tiles(u).astype(u3_ref.dtype)


def _proj_res_ln(a, w, x, gate, lg, lb, sc, sh, T, tm):
    N, D = x.shape
    g3, g_spec = _row_mod(gate, T, tm)
    sc3, sc_spec = _row_mod(sc, T, tm)
    sh3, sh_spec = _row_mod(sh, T, tm)
    row = lambda i: (i, 0)
    fixed = lambda i: (0, 0)
    return pl.pallas_call(
        _proj_res_ln_kernel,
        grid=(N // tm,),
        in_specs=[
            pl.BlockSpec((tm, a.shape[1]), row),
            pl.BlockSpec(w.shape, fixed),
            pl.BlockSpec((tm, D), row),
            g_spec,
            pl.BlockSpec((1, D), fixed),
            pl.BlockSpec((1, D), fixed),
            sc_spec,
            sh_spec,
        ],
        out_specs=[pl.BlockSpec((tm, D), row), pl.BlockSpec((tm, D), row),
                   pl.BlockSpec((tm, D // LANES, LANES), lambda i: (i, 0, 0))],
        out_shape=[jax.ShapeDtypeStruct((N, D), F32), jax.ShapeDtypeStruct((N, D), BF16),
                   jax.ShapeDtypeStruct((N, D // LANES, LANES), BF16)],
        compiler_params=_cparams("parallel"),
        name="proj_res_ln",
    )(a, w, x, g3, lg.reshape(1, D), lb.reshape(1, D), sc3, sh3)


def _router_kernel(u_ref, wr_ref, rb_ref, idx_ref, w_ref, rank_ref, cnt_ref, carry_ref, *, tm):
    @pl.when(pl.program_id(0) == 0)
    def _():
        carry_ref[...] = jnp.zeros_like(carry_ref)

    s = _sigmoid(_dot_nt(wr_ref[...], u_ref[...]))
    sb = s + rb_ref[...]
    neg = -jnp.inf
    sub = lax.broadcasted_iota(jnp.int32, (GROUP_SIZE, tm), 0)
    gs = GROUP_SIZE
    s_g = [s[g * gs:(g + 1) * gs] for g in range(N_GROUPS)]
    sb_g = [sb[g * gs:(g + 1) * gs] for g in range(N_GROUPS)]
    eid_g = [sub + g * gs for g in range(N_GROUPS)]

    score = []
    for g in range(N_GROUPS):
        m1 = jnp.max(sb_g[g], axis=0, keepdims=True)
        first = jnp.min(jnp.where(sb_g[g] == m1, sub, gs), axis=0, keepdims=True)
        m2 = jnp.max(jnp.where(sub == first, neg, sb_g[g]), axis=0, keepdims=True)
        score.append(m1 + m2)

    chosen = [jnp.zeros((1, tm), jnp.int32) for _ in range(N_GROUPS)]
    for _ in range(TOPK_GROUPS):
        m = score[0]
        for g in range(1, N_GROUPS):
            m = jnp.maximum(m, score[g])
        gi = jnp.full((1, tm), N_GROUPS, jnp.int32)
        for g in range(N_GROUPS - 1, -1, -1):
            gi = jnp.where(score[g] == m, g, gi)
        for g in range(N_GROUPS):
            hit = gi == g
            chosen[g] = jnp.where(hit, 1, chosen[g])
            score[g] = jnp.where(hit, neg, score[g])
    cand = [jnp.where(chosen[g] > 0, sb_g[g], neg) for g in range(N_GROUPS)]

    onehot = [jnp.zeros((gs, tm), F32) for _ in range(N_GROUPS)]
    idxs, ws = [], []
    for _ in range(TOP_K):
        mx = cand[0]
        for g in range(1, N_GROUPS):
            mx = jnp.maximum(mx, cand[g])
        m = jnp.max(mx, axis=0, keepdims=True)
        lo = jnp.where(cand[0] == m, eid_g[0], N_EXPERTS)
        for g in range(1, N_GROUPS):
            lo = jnp.minimum(lo, jnp.where(cand[g] == m, eid_g[g], N_EXPERTS))
        idx = jnp.min(lo, axis=0, keepdims=True)
        wsum = jnp.zeros((gs, tm), F32)
        for g in range(N_GROUPS):
            hit = eid_g[g] == idx
            wsum = wsum + jnp.where(hit, s_g[g], 0.0)
            cand[g] = jnp.where(hit, neg, cand[g])
            onehot[g] = jnp.where(hit, 1.0, onehot[g])
        idxs.append(idx)
        ws.append(jnp.sum(wsum, axis=0, keepdims=True))
    total = ws[0]
    for k in range(1, TOP_K):
        total = total + ws[k]
    for k in range(TOP_K):
        idx_ref[k:k + 1, :] = idxs[k]
        w_ref[k:k + 1, :] = ws[k] / total * ROUTED_SCALE

    oh = jnp.concatenate(onehot, axis=0)
    r = lax.broadcasted_iota(jnp.int32, (tm, tm), 0)
    c = lax.broadcasted_iota(jnp.int32, (tm, tm), 1)
    before = (r < c).astype(BF16)
    pre = _dot(oh.astype(BF16), before) + carry_ref[...]
    for k in range(TOP_K):
        acc = jnp.zeros((gs, tm), F32)
        for g in range(N_GROUPS):
            acc = acc + jnp.where(eid_g[g] == idxs[k], pre[g * gs:(g + 1) * gs], 0.0)
        rank_ref[k:k + 1, :] = jnp.sum(acc, axis=0, keepdims=True).astype(jnp.int32)
    carry_ref[...] = carry_ref[...] + jnp.sum(oh, axis=1, keepdims=True)
    cnt_ref[...] = jnp.broadcast_to(carry_ref[...], cnt_ref.shape).astype(jnp.int32)


def _router(u, wr_t, rb, tm):
    N, D = u.shape
    col = lambda i: (0, i)
    fixed = lambda i: (0, 0)
    return pl.pallas_call(
        functools.partial(_router_kernel, tm=tm),
        grid=(N // tm,),
        in_specs=[
            pl.BlockSpec((tm, D), lambda i: (i, 0)),
            pl.BlockSpec((N_EXPERTS, D), fixed),
            pl.BlockSpec((N_EXPERTS, 1), fixed),
        ],
        out_specs=[
            pl.BlockSpec((TOP_K, tm), col),
            pl.BlockSpec((TOP_K, tm), col),
            pl.BlockSpec((TOP_K, tm), col),
            pl.BlockSpec((N_EXPERTS, LANES), fixed),
        ],
        out_shape=[
            jax.ShapeDtypeStruct((TOP_K, N), jnp.int32),
            jax.ShapeDtypeStruct((TOP_K, N), F32),
            jax.ShapeDtypeStruct((TOP_K, N), jnp.int32),
            jax.ShapeDtypeStruct((N_EXPERTS, LANES), jnp.int32),
        ],
        scratch_shapes=[pltpu.VMEM((N_EXPERTS, 1), F32)],
        compiler_params=_cparams("arbitrary"),
        name="router",
    )(u, wr_t, rb)


def _experts_kernel(be_ref, live_ref, x_ref, wg_ref, wu_ref, wd_ref, o_ref, wgb_ref, wub_ref, wdb_ref):
    i = pl.program_id(0)
    e = be_ref[i]
    prev = be_ref[jnp.maximum(i - 1, 0)]

    @pl.when((i == 0) | (e != prev))
    def _():
        wgb_ref[...] = wg_ref[...].astype(BF16)
        wub_ref[...] = wu_ref[...].astype(BF16)
        wdb_ref[...] = wd_ref[...].astype(BF16)

    @pl.when(live_ref[i] > 0)
    def _():
        x = _from_row_tiles(x_ref[...].astype(F32)).astype(BF16)
        h = (_silu(_dot(x, wgb_ref[...])) * _dot(x, wub_ref[...])).astype(BF16)
        o_ref[...] = _to_row_tiles(_dot(h, wdb_ref[...]))

    @pl.when(live_ref[i] == 0)
    def _():
        o_ref[...] = jnp.zeros_like(o_ref)


EXPERT_ROWS = 512


def _experts(xs3, block_e, live, wg, wu, wd, layer, bm):
    R, S, _ = xs3.shape
    D = S * LANES
    return pl.pallas_call(
        _experts_kernel,
        grid_spec=pltpu.PrefetchScalarGridSpec(
            num_scalar_prefetch=2,
            grid=(R // bm,),
            in_specs=[
                pl.BlockSpec((bm, S, LANES), lambda i, be, lv: (i * lv[i], 0, 0)),
                pl.BlockSpec((None, None, D, D_EXPERT), lambda i, be, lv: (layer, be[i], 0, 0)),
                pl.BlockSpec((None, None, D, D_EXPERT), lambda i, be, lv: (layer, be[i], 0, 0)),
                pl.BlockSpec((None, None, D_EXPERT, D), lambda i, be, lv: (layer, be[i], 0, 0)),
            ],
            out_specs=pl.BlockSpec((bm, S, LANES), lambda i, be, lv: (i, 0, 0)),
            scratch_shapes=[
                pltpu.VMEM((D, D_EXPERT), BF16),
                pltpu.VMEM((D, D_EXPERT), BF16),
                pltpu.VMEM((D_EXPERT, D), BF16),
            ],
        ),
        out_shape=jax.ShapeDtypeStruct((R, S, LANES), F32),
        compiler_params=_cparams("arbitrary"),
        name="experts",
    )(block_e, live, xs3, wg, wu, wd)


ISSUE_TOKENS = 2


ZERO_ROWS = 128


def _dispatch_kernel(pad_end_ref, clear_from_ref, dest_ref, u_ref, *rest, tmd, bm, first):
    if first:
        xs_hbm, sem, zero_ref, zsem = rest

        @pl.when(pl.program_id(0) == 0)
        def _():
            zero_ref[...] = jnp.zeros_like(zero_ref)

            def clear_block(b, carry):
                for s in range(bm // ZERO_ROWS):
                    rows = pl.ds(b * bm + s * ZERO_ROWS, ZERO_ROWS)
                    pltpu.make_async_copy(zero_ref, xs_hbm.at[rows], zsem).start()
                return carry

            def block_cleared(b, carry):
                for s in range(bm // ZERO_ROWS):
                    pltpu.make_async_copy(zero_ref, xs_hbm.at[pl.ds(0, ZERO_ROWS)], zsem).wait()
                return carry

            def clear(e, carry):
                return lax.fori_loop(clear_from_ref[e] // bm, pad_end_ref[e] // bm, clear_block, carry)

            def cleared(e, carry):
                return lax.fori_loop(clear_from_ref[e] // bm, pad_end_ref[e] // bm, block_cleared, carry)

            first_unused = pad_end_ref[N_EXPERTS - 1] // bm
            n_blocks = xs_hbm.shape[0] // bm
            lax.fori_loop(0, N_EXPERTS, clear, 0)
            lax.fori_loop(first_unused, n_blocks, clear_block, 0)
            lax.fori_loop(0, N_EXPERTS, cleared, 0)
            lax.fori_loop(first_unused, n_blocks, block_cleared, 0)
    else:
        _, xs_hbm, sem = rest

    def issue(t2, carry):
        base = t2 * (ISSUE_TOKENS * TOP_K)
        rows = [dest_ref[base + r] for r in range(ISSUE_TOKENS * TOP_K)]
        for r, row in enumerate(rows):
            pltpu.make_async_copy(u_ref.at[t2 * ISSUE_TOKENS + r // TOP_K], xs_hbm.at[row], sem).start(priority=r % 2)
        return carry

    lax.fori_loop(0, tmd // ISSUE_TOKENS, issue, 0)
    for k in range(TOP_K):
        pltpu.make_async_copy(u_ref, xs_hbm.at[pl.ds(0, tmd)], sem).wait()


def _dispatch(u3, dest, xs_prev, pad_end, clear_from, R, tmd, bm):
    N, S, _ = u3.shape
    first = xs_prev is None
    in_specs = [
        pl.BlockSpec((tmd * TOP_K,), lambda i, pe, pd: (i,), memory_space=pltpu.SMEM),
        pl.BlockSpec((tmd, S, LANES), lambda i, pe, pd: (i, 0, 0)),
    ]
    scratch = [pltpu.SemaphoreType.DMA(())]
    if first:
        scratch += [pltpu.VMEM((ZERO_ROWS, S, LANES), u3.dtype), pltpu.SemaphoreType.DMA(())]
    else:
        in_specs.append(pl.BlockSpec(memory_space=pl.ANY))
    return pl.pallas_call(
        functools.partial(_dispatch_kernel, tmd=tmd, bm=bm, first=first),
        grid_spec=pltpu.PrefetchScalarGridSpec(
            num_scalar_prefetch=2,
            grid=(N // tmd,),
            in_specs=in_specs,
            out_specs=pl.BlockSpec(memory_space=pl.ANY),
            scratch_shapes=scratch,
        ),
        out_shape=jax.ShapeDtypeStruct((R, S, LANES), u3.dtype),
        input_output_aliases={} if first else {4: 0},
        compiler_params=_cparams("arbitrary"),
        name="moe_dispatch",
    )(pad_end, clear_from, dest, u3, *(() if first else (xs_prev,)))


def _combine_kernel(dest_ref, w_ref, y_hbm, o_ref, buf_ref, sem, *, tmc):
    def issue(t2, carry):
        first = t2 * (ISSUE_TOKENS * TOP_K)
        rows = [dest_ref[first + r] for r in range(ISSUE_TOKENS * TOP_K)]
        for r, row in enumerate(rows):
            pltpu.make_async_copy(y_hbm.at[row], buf_ref.at[first + r], sem).start(priority=r % 2)
        return carry

    lax.fori_loop(0, tmc // ISSUE_TOKENS, issue, 0)
    pltpu.make_async_copy(y_hbm.at[pl.ds(0, tmc * TOP_K)], buf_ref, sem).wait()

    def reduce(t, carry):
        acc = w_ref[t * TOP_K] * buf_ref[t * TOP_K]
        for k in range(1, TOP_K):
            acc = acc + w_ref[t * TOP_K + k] * buf_ref[t * TOP_K + k]
        o_ref[t] = acc
        return carry

    lax.fori_loop(0, tmc, reduce, 0)


def _combine(y3, dest, w, N, tmc):
    _, S, _ = y3.shape
    return pl.pallas_call(
        functools.partial(_combine_kernel, tmc=tmc),
        grid=(N // tmc,),
        in_specs=[
            pl.BlockSpec((tmc * TOP_K,), lambda i: (i,), memory_space=pltpu.SMEM),
            pl.BlockSpec((tmc * TOP_K,), lambda i: (i,), memory_space=pltpu.SMEM),
            pl.BlockSpec(memory_space=pl.ANY),
        ],
        out_specs=pl.BlockSpec((tmc, S, LANES), lambda i: (i, 0, 0)),
        out_shape=jax.ShapeDtypeStruct((N, S, LANES), F32),
        scratch_shapes=[pltpu.VMEM((tmc * TOP_K, S, LANES), F32), pltpu.SemaphoreType.DMA(())],
        compiler_params=_cparams("arbitrary"),
        name="moe_combine",
    )(dest, w, y3)


def _moe_final_kernel(u_ref, x_ref, r_ref, wgu_ref, wd_ref, gate_ref, lg_ref, lb_ref, xo_ref):
    gu = _dot(u_ref[...], wgu_ref[...])
    h = (_silu(gu[:, :D_SHARED]) * gu[:, D_SHARED:]).astype(BF16)
    y = _from_row_tiles(r_ref[...]) + _dot(h, wd_ref[...])
    hres = DEEPNORM_ALPHA * x_ref[...] + (1.0 + gate_ref[0]) * y
    xo_ref[...] = _layer_norm(hres, lg_ref[...], lb_ref[...])


def _moe_final(u, x, routed, wgu, wd, gate, lg, lb, T, tm):
    N, D = x.shape
    g3, g_spec = _row_mod(gate, T, tm)
    row = lambda i: (i, 0)
    fixed = lambda i: (0, 0)
    return pl.pallas_call(
        _moe_final_kernel,
        grid=(N // tm,),
        in_specs=[
            pl.BlockSpec((tm, D), row),
            pl.BlockSpec((tm, D), row),
            pl.BlockSpec((tm, D // LANES, LANES), lambda i: (i, 0, 0)),
            pl.BlockSpec(wgu.shape, fixed),
            pl.BlockSpec(wd.shape, fixed),
            g_spec,
            pl.BlockSpec((1, D), fixed),
            pl.BlockSpec((1, D), fixed),
        ],
        out_specs=pl.BlockSpec((tm, D), row),
        out_shape=jax.ShapeDtypeStruct((N, D), F32),
        compiler_params=_cparams("parallel"),
        name="moe_final",
    )(u, x, routed, wgu, wd, g3, lg.reshape(1, D), lb.reshape(1, D))


LOG2E = 1.4426950408889634


def _fox_kernel(qi_ref, kj_ref, q_ref, k_ref, v_ref, fq_ref, fk_ref, o_ref, m_ref, l_ref, acc_ref, *, bq):
    i = qi_ref[pl.program_id(1)]
    j = kj_ref[pl.program_id(1)]

    @pl.when(j == 0)
    def _():
        m_ref[...] = jnp.full_like(m_ref, MASK_NEG)
        l_ref[...] = jnp.zeros_like(l_ref)
        acc_ref[...] = jnp.zeros_like(acc_ref)

    def block(diagonal):
        if diagonal:
            visible = (lax.broadcasted_iota(jnp.int32, (bq, bq), 0) <= lax.broadcasted_iota(jnp.int32, (bq, bq), 1))
        qk_scale = FOX_HD ** -0.5 * LOG2E
        fq2 = fq_ref[...] * LOG2E
        fk2 = fk_ref[...] * LOG2E
        for h in range(FOX_HEADS):
            hs = slice(h * FOX_HD, (h + 1) * FOX_HD)
            s = _dot_nt(k_ref[:, hs], q_ref[:, hs]) * qk_scale + (fq2[h:h + 1, :] - fk2[:, h:h + 1])
            if diagonal:
                s = jnp.where(visible, s, MASK_NEG)
            m_prev = m_ref[h:h + 1, :]
            m_new = jnp.maximum(m_prev, jnp.max(s, axis=0, keepdims=True))
            alpha = jnp.exp2(m_prev - m_new)
            p = jnp.exp2(s - m_new)
            l_ref[h:h + 1, :] = alpha * l_ref[h:h + 1, :] + jnp.sum(p, axis=0, keepdims=True)
            acc_ref[hs, :] = alpha * acc_ref[hs, :] + _dot_tn(v_ref[:, hs], p.astype(BF16))
            m_ref[h:h + 1, :] = m_new

    pl.when(j < i)(lambda: block(False))

    @pl.when(j == i)
    def _():
        block(True)
        for h in range(FOX_HEADS):
            hs = slice(h * FOX_HD, (h + 1) * FOX_HD)
            o_ref[:, hs] = (acc_ref[hs, :] / l_ref[h:h + 1, :]).T.astype(o_ref.dtype)


def _fox_attention(q, kb, vb, cum, cum_t, B, T, bq):
    nq = T // bq
    pairs = [(i, j) for i in range(nq) for j in range(i + 1)]
    qi = jnp.array([p[0] for p in pairs], jnp.int32)
    kj = jnp.array([p[1] for p in pairs], jnp.int32)
    q_map = lambda b, s, qi, kj: (b * nq + qi[s], 0)
    kv_map = lambda b, s, qi, kj: (b * nq + kj[s], 0)
    return pl.pallas_call(
        functools.partial(_fox_kernel, bq=bq),
        grid_spec=pltpu.PrefetchScalarGridSpec(
            num_scalar_prefetch=2,
            grid=(B, len(pairs)),
            in_specs=[
                pl.BlockSpec((bq, FOX_DIM), q_map),
                pl.BlockSpec((bq, FOX_DIM), kv_map),
                pl.BlockSpec((bq, FOX_DIM), kv_map),
                pl.BlockSpec((FOX_HEADS, bq), lambda b, s, qi, kj: (b, qi[s])),
                pl.BlockSpec((bq, LANES), kv_map),
            ],
            out_specs=pl.BlockSpec((bq, FOX_DIM), q_map),
            scratch_shapes=[
                pltpu.VMEM((FOX_HEADS, bq), F32),
                pltpu.VMEM((FOX_HEADS, bq), F32),
                pltpu.VMEM((FOX_DIM, bq), F32),
            ],
        ),
        out_shape=jax.ShapeDtypeStruct((B * T, FOX_DIM), BF16),
        compiler_params=_cparams("parallel", "arbitrary"),
        name="fox_attention",
    )(qi, kj, q, kb, vb, cum_t, cum)


def _decode_kernel(pt_ref, q_ref, kn_ref, vn_ref, fq_ref, fkn_ref, *rest, tq, npg):
    kc, vc, lfc = rest[:npg], rest[npg:2 * npg], rest[2 * npg:3 * npg]
    o_ref, m_ref, l_ref, acc_ref, carry_ref = rest[3 * npg:]
    j = pl.program_id(1)
    H = FOX_HEADS
    scale = FOX_HD ** -0.5
    q = [q_ref[:, h * FOX_HD:(h + 1) * FOX_HD].astype(BF16) for h in range(H)]
    fq = jnp.concatenate([fq_ref[:, h:h + 1] for h in range(H)], axis=0)

    def per_head_rows(x):
        return jnp.concatenate([jnp.broadcast_to(x[h:h + 1, :], (tq, x.shape[1])) for h in range(H)], axis=0)

    def attend(s, values, state):
        m_prev, l_prev, acc_prev = state
        m_new = jnp.maximum(m_prev, jnp.max(s, axis=1, keepdims=True))
        alpha = jnp.exp(m_prev - m_new)
        p = jnp.exp(s - m_new)
        l_new = alpha * l_prev + jnp.sum(p, axis=1, keepdims=True)
        pv = []
        for h in range(H):
            o = None
            for i, value in enumerate(values):
                ph = p[h * tq:(h + 1) * tq, i * PAGE_SIZE:(i + 1) * PAGE_SIZE].astype(BF16)
                t = _dot(ph, value(h))
                o = t if o is None else o + t
            pv.append(o)
        return m_new, l_new, alpha * acc_prev + jnp.concatenate(pv, axis=0)

    @pl.when(j == 0)
    def _():
        carry_ref[...] = jnp.zeros_like(carry_ref)
        r = lax.broadcasted_iota(jnp.int32, (H * tq, PAGE_SIZE), 0) % tq
        c = lax.broadcasted_iota(jnp.int32, (H * tq, PAGE_SIZE), 1)
        s = jnp.concatenate(
            [_dot_nt(q[h], kn_ref[0, :, h * FOX_HD:(h + 1) * FOX_HD]) for h in range(H)], axis=0)
        s = s * scale + fq - per_head_rows(fkn_ref[0])
        s = jnp.where(c <= r, s, MASK_NEG)
        state = (jnp.full((H * tq, 1), MASK_NEG, F32), jnp.zeros((H * tq, 1), F32),
                 jnp.zeros((H * tq, FOX_HD), F32))
        m_ref[...], l_ref[...], acc_ref[...] = attend(
            s, [lambda h: vn_ref[0, :, h * FOX_HD:(h + 1) * FOX_HD]], state)

    r = lax.broadcasted_iota(jnp.int32, (PAGE_SIZE, PAGE_SIZE), 0)
    c = lax.broadcasted_iota(jnp.int32, (PAGE_SIZE, PAGE_SIZE), 1)
    after = (r > c).astype(BF16)
    carry = carry_ref[...]
    blocks = []
    for i in range(npg):
        lf = lfc[i][...]
        suffix = _dot_exact_rhs(lf, after) + carry
        carry = carry + jnp.sum(lf, axis=1, keepdims=True)
        kt = jnp.swapaxes(kc[i][...], 0, 1)
        sk = jnp.concatenate([_dot_nt(q[h], kt[h].astype(BF16)) for h in range(H)], axis=0)
        blocks.append(sk * scale + per_head_rows(suffix))
    carry_ref[...] = carry
    s = jnp.concatenate(blocks, axis=1) + fq
    vts = [jnp.swapaxes(vc[i][...], 0, 1) for i in range(npg)]
    values = [lambda h, i=i: vts[i][h].astype(BF16) for i in range(npg)]
    m_new, l_new, acc_new = attend(s, values, (m_ref[...], l_ref[...], acc_ref[...]))
    m_ref[...] = m_new
    l_ref[...] = l_new
    acc_ref[...] = acc_new

    @pl.when(j == pl.num_programs(1) - 1)
    def _():
        out = acc_new / l_new
        for h in range(H):
            o_ref[:, h * FOX_HD:(h + 1) * FOX_HD] = out[h * tq:(h + 1) * tq, :]


DECODE_PAGES_PER_STEP = 8


def _decode_attention(page_table, q, k_new, v_new, cum, cum_new_t, cache_k, cache_v, cache_lf_t, B, tq):
    n_pages = page_table.shape[1]
    npg = DECODE_PAGES_PER_STEP
    assert n_pages % npg == 0

    def page(i, nd):
        return lambda b, j, pt: (pt[b * n_pages + (n_pages - 1 - (j * npg + i))],) + (0,) * nd

    kv_specs = [pl.BlockSpec((None, PAGE_SIZE, FOX_HEADS, FOX_HD), page(i, 3)) for i in range(npg)]
    lf_specs = [pl.BlockSpec((None, FOX_HEADS, PAGE_SIZE), page(i, 2)) for i in range(npg)]
    return pl.pallas_call(
        functools.partial(_decode_kernel, tq=tq, npg=npg),
        grid_spec=pltpu.PrefetchScalarGridSpec(
            num_scalar_prefetch=1,
            grid=(B, n_pages // npg),
            in_specs=[
                pl.BlockSpec((tq, FOX_DIM), lambda b, j, pt: (b, 0)),
                pl.BlockSpec((1, PAGE_SIZE, FOX_DIM), lambda b, j, pt: (b, 0, 0)),
                pl.BlockSpec((1, PAGE_SIZE, FOX_DIM), lambda b, j, pt: (b, 0, 0)),
                pl.BlockSpec((tq, LANES), lambda b, j, pt: (b, 0)),
                pl.BlockSpec((1, FOX_HEADS, PAGE_SIZE), lambda b, j, pt: (b, 0, 0)),
            ] + kv_specs + kv_specs + lf_specs,
            out_specs=pl.BlockSpec((tq, FOX_DIM), lambda b, j, pt: (b, 0)),
            scratch_shapes=[
                pltpu.VMEM((FOX_HEADS * tq, 1), F32),
                pltpu.VMEM((FOX_HEADS * tq, 1), F32),
                pltpu.VMEM((FOX_HEADS * tq, FOX_HD), F32),
                pltpu.VMEM((FOX_HEADS, 1), F32),
            ],
        ),
        out_shape=jax.ShapeDtypeStruct((B * tq, FOX_DIM), F32),
        compiler_params=_cparams("parallel", "arbitrary"),
        name="decode_attention",
    )(page_table.reshape(-1), q, k_new, v_new, cum, cum_new_t, *([cache_k] * npg), *([cache_v] * npg),
      *([cache_lf_t] * npg))


def _moe(groups, W, l):
    bm = EXPERT_ROWS
    routes = [_router(g["u"], W["router_wt"][l], W["router_bias"][l].reshape(N_EXPERTS, 1), g["tm"])
              for g in groups]
    counts = [r[3][:, 0] for r in routes]
    total = sum(counts)
    padded = (total + bm - 1) // bm * bm
    pad_end = jnp.cumsum(padded)
    n_blocks = -(-sum(g["N"] for g in groups) * TOP_K // bm) + N_EXPERTS
    starts = jnp.arange(n_blocks, dtype=jnp.int32) * bm
    block_e = jnp.minimum(jnp.sum(pad_end[None, :] <= starts[:, None], axis=1), N_EXPERTS - 1).astype(jnp.int32)
    live = (starts < pad_end[-1]).astype(jnp.int32)
    offset = pad_end - padded
    clear_from = (offset + counts[0]) // bm * bm
    xs3, dests = None, []
    for g, (idx_t, _, rank_t, _), cnt in zip(groups, routes, counts):
        first = jnp.sum(jnp.where(idx_t[:, :, None] == jnp.arange(N_EXPERTS), offset, 0), axis=-1)
        dests.append((first + rank_t).T.reshape(-1))
        xs3 = _dispatch(g["u3"], dests[-1], xs3, pad_end, clear_from, n_blocks * bm, min(512, g["N"]), bm)
        offset = offset + cnt
    y3 = _experts(xs3, block_e, live, W["exp_w_gate"], W["exp_w_up"], W["exp_w_down"], l, bm)
    for g, (_, w_t, _, _), dest in zip(groups, routes, dests):
        routed3 = _combine(y3, dest, w_t.T.reshape(-1), g["N"], min(256, g["N"]))
        g["x"] = _moe_final(g["u"], g["x"], routed3, W["sh_wgu"][l], W["sh_wd"][l], g["g_m"], W["ln_g"][l, 1],
                            W["ln_b"][l, 1], g["T"], g["tm"])


def _mixer(g, W, l):
    B, T, N, tm, past = g["B"], g["T"], g["N"], g["tm"], g["past"]
    sh_a, sc_a, g_a, sh_m, sc_m, g_m = jnp.split(g["mods"][l], 6, axis=-1)
    if l < N_A:
        p = _modproj(g["x"], sc_a, sh_a, W["gla_w_in"][l], T, tm, F32)
        if T >= GLA_CHUNK:
            Tp, tc, hps = T, min(512, T), 2
        else:
            Tp, tc, hps = GLA_CHUNK, GLA_CHUNK, GLA_HEADS
            p = jnp.pad(p.reshape(B, T, -1), ((0, 0), (0, Tp - T), (0, 0))).reshape(B * Tp, -1)
        o, s_T = _gla(p, W["gla_w_a2"][l], W["gla_b_a"][l], W["gla_norm_g"][l], g["s0"][l], B, Tp, tc, T, hps)
        if Tp != T:
            o = o.reshape(B, Tp, -1)[:, :T].reshape(N, -1)
        g["gla"].append(s_T)
        w_o = W["gla_w_o"][l]
    else:
        q = _modproj(g["x"], sc_a, sh_a, W["fox_w_q"][l - N_A], T, tm, BF16 if past is None else F32)
        if past is None:
            o = _fox_attention(q, g["kb"], g["vb"], g["cum"], g["cum_t"], B, T, min(256, T))
        else:
            o = _decode_attention(past["page_table"], q, g["kb"], g["vb"], g["cum"], g["cum_t"], past["k"],
                                  past["v"], past["lf_t"], B, T)
        w_o = W["fox_w_o"][l - N_A]
    g["x"], g["u"], g["u3"] = _proj_res_ln(o, w_o, g["x"], g_a, W["ln_g"][l, 0], W["ln_b"][l, 0], sc_m, sh_m, T, tm)
    g["g_m"] = g_m


def _shared_kv(g, W):
    B, T = g["B"], g["T"]
    sh_kv, sc_kv = jnp.split(g["kv_mod"], 2, axis=-1)
    g["k"], g["v"], kb, vb, g["lf"], g["cum"] = _kv_proj(g["x"], sc_kv, sh_kv, W["kv_w"], W["fox_b_f"], T, g["tm"])
    cum_h = g["cum"][:, :FOX_HEADS].reshape(B, T, FOX_HEADS)
    if g["past"] is None:
        g["cum_t"] = cum_h.transpose(0, 2, 1).reshape(B * FOX_HEADS, T)
        g["kb"], g["vb"] = kb, vb
    else:
        rows = ((0, 0), (0, PAGE_SIZE - T), (0, 0))
        g["cum_t"] = jnp.pad(cum_h, rows).transpose(0, 2, 1)
        g["kb"] = jnp.pad(kb.reshape(B, T, FOX_DIM), rows)
        g["vb"] = jnp.pad(vb.reshape(B, T, FOX_DIM), rows)


def _group(x, mods, kv_mod, s0, past):
    B, T, D = x.shape
    return dict(x=x.reshape(B * T, D), B=B, T=T, N=B * T, tm=min(512, B * T), mods=mods, kv_mod=kv_mod, s0=s0,
                past=past, gla=[])


def _group_outputs(g):
    B, T = g["B"], g["T"]
    return (g["x"].reshape(B, T, D_MODEL), jnp.stack(g["gla"]),
            g["k"].reshape(B, T, FOX_HEADS, FOX_HD), g["v"].reshape(B, T, FOX_HEADS, FOX_HD),
            g["lf"][:, :FOX_HEADS].reshape(B, T, FOX_HEADS))


def kernel(x_prompt, x_sample, c_prompt, c_sample, state_gla, cache_k, cache_v, cache_logf, page_table, gla_w_in, gla_w_a2, gla_b_a, gla_norm_g, gla_w_o, kv_w_mod, kv_b_mod, kv_w, fox_b_f, fox_w_q, fox_w_o, ada_w, ada_b, ln_g, ln_b, router_w, router_bias, exp_w_gate, exp_w_up, exp_w_down, sh_w_gate, sh_w_up, sh_w_down):
    Bp, Tp, D = x_prompt.shape
    Bs, Ts, _ = x_sample.shape
    lane_pad = lambda w, n: jnp.pad(w, [(0, 0)] * (w.ndim - 1) + [(0, n - w.shape[-1])])
    W = dict(
        gla_w_in=jnp.concatenate(
            [gla_w_in[..., :GLA_IN_MAIN], lane_pad(gla_w_in[..., GLA_IN_MAIN:], LANES)], axis=-1).astype(BF16),
        gla_w_a2=jnp.pad(gla_w_a2, ((0, 0), (0, LANES - GLA_GATE_RANK), (0, 0))).astype(BF16),
        gla_b_a=gla_b_a.reshape(N_A, 1, GLA_DK),
        gla_norm_g=gla_norm_g.reshape(N_A, 1, GLA_DVH),
        gla_w_o=gla_w_o.astype(BF16),
        kv_w=jnp.concatenate([kv_w[:, :2 * FOX_DIM], lane_pad(kv_w[:, 2 * FOX_DIM:], LANES)], axis=-1).astype(BF16),
        fox_b_f=lane_pad(fox_b_f.reshape(1, FOX_HEADS), LANES),
        fox_w_q=fox_w_q.astype(BF16),
        fox_w_o=fox_w_o.astype(BF16),
        ln_g=ln_g, ln_b=ln_b,
        router_wt=jnp.swapaxes(router_w, 1, 2).astype(BF16),
        router_bias=router_bias,
        exp_w_gate=exp_w_gate, exp_w_up=exp_w_up, exp_w_down=exp_w_down,
        sh_wgu=jnp.concatenate([sh_w_gate, sh_w_up], axis=-1).astype(BF16),
        sh_wd=sh_w_down.astype(BF16),
    )
    c_all = jnp.concatenate([c_prompt, c_sample], axis=0)
    mods = _cond_linear(c_all, ada_w, ada_b, 1536)
    kv_mod = _cond_linear(c_all, kv_w_mod[None], kv_b_mod[None], 1024)[0]

    s0_prompt = jnp.zeros((N_A, Bp, GLA_HEADS, GLA_DKH, GLA_DVH), F32)
    past = dict(page_table=page_table, k=cache_k, v=cache_v, lf_t=jnp.swapaxes(cache_logf, 1, 2))
    groups = [_group(x_prompt, mods[:, :Bp], kv_mod[:Bp], s0_prompt, None),
              _group(x_sample, mods[:, Bp:], kv_mod[Bp:], state_gla, past)]
    for l in range(DEPTH):
        for g in groups:
            _mixer(g, W, l)
        _moe(groups, W, l)
        if l == N_A - 1:
            for g in groups:
                _shared_kv(g, W)
    y_p, gla_p, k_p, v_p, lf_p = _group_outputs(groups[0])
    y_s, gla_s, k_s, v_s, lf_s = _group_outputs(groups[1])
    return (y_p, y_s, gla_p, gla_s, k_p, v_p, lf_p, k_s, v_s, lf_s)
```

```python
import functools

import jax
import jax.numpy as jnp
from jax import lax
from jax.experimental import pallas as pl
from jax.experimental.pallas import tpu as pltpu

F32 = jnp.float32
BF16 = jnp.bfloat16

D_MODEL = 1024
DEPTH = 4
N_A = DEPTH // 2
PAGE_SIZE = 128
GLA_HEADS = 4
GLA_DK = D_MODEL // 2
GLA_DV = D_MODEL
GLA_DKH = GLA_DK // GLA_HEADS
GLA_DVH = GLA_DV // GLA_HEADS
GLA_GATE_RANK = 16
GLA_GATE_TAU = 16.0
GLA_CHUNK = 64
FOX_HEADS = 8
FOX_HD = D_MODEL // FOX_HEADS
FOX_DIM = FOX_HEADS * FOX_HD
N_EXPERTS = 64
TOP_K = 8
N_GROUPS = 8
GROUP_SIZE = N_EXPERTS // N_GROUPS
TOPK_GROUPS = 4
D_EXPERT = 256
D_SHARED = 256
ROUTED_SCALE = 2.5
DEEPNORM_ALPHA = (2 * DEPTH) ** 0.25
LN_EPS = 1e-5
RMS_EPS = 1e-6

LANES = 128
GLA_IN_MAIN = 2 * GLA_DK + 2 * GLA_DV
GLA_IN_PAD = GLA_IN_MAIN + LANES
KV_PAD = 2 * FOX_DIM + LANES
MASK_NEG = -1e30
VMEM_LIMIT = 52 * 1024 * 1024


def _cparams(*sem):
    return pltpu.CompilerParams(dimension_semantics=sem, vmem_limit_bytes=VMEM_LIMIT)


def _dot(a, b):
    return jnp.dot(a, b, preferred_element_type=F32)


def _dot_nt(a, b):
    return lax.dot_general(a, b, (((1,), (1,)), ((), ())), preferred_element_type=F32)


def _dot_tn(a, b):
    return lax.dot_general(a, b, (((0,), (0,)), ((), ())), preferred_element_type=F32)


def _sigmoid(x):
    return 1.0 / (1.0 + jnp.exp(-x))


def _silu(x):
    return x * _sigmoid(x)


def _log_sigmoid(x):
    return jnp.minimum(x, 0.0) - jnp.log1p(jnp.exp(-jnp.abs(x)))


def _split3(x):
    hi = x.astype(BF16)
    r = x - hi.astype(F32)
    mid = r.astype(BF16)
    lo = (r - mid.astype(F32)).astype(BF16)
    return hi, mid, lo


def _dot_exact_lhs(mat01, x):
    hi, mid, lo = _split3(x)
    return _dot(mat01, hi) + _dot(mat01, mid) + _dot(mat01, lo)


def _dot_exact_rhs(x, mat01):
    hi, mid, lo = _split3(x)
    return _dot(hi, mat01) + _dot(mid, mat01) + _dot(lo, mat01)


def _layer_norm(h, g, b):
    mu = jnp.mean(h, axis=-1, keepdims=True)
    xc = h - mu
    var = jnp.mean(xc * xc, axis=-1, keepdims=True)
    return xc * lax.rsqrt(var + LN_EPS) * g + b


def _row_mod(m, T, tm):
    B, D = m.shape
    if T % tm == 0:
        tiles_per_batch = T // tm
        return m.reshape(B, 1, D), pl.BlockSpec((1, 1, D), lambda i: (i // tiles_per_batch, 0, 0))
    assert tm % T == 0
    e = jnp.repeat(m, T, axis=0).reshape(B * T // tm, tm, D)
    return e, pl.BlockSpec((1, tm, D), lambda i: (i, 0, 0))


def _cond_kernel(c_ref, w_ref, b_ref, o_ref):
    s = _silu(c_ref[...]).astype(BF16)
    o_ref[0] = _dot(s, w_ref[0].astype(BF16)) + b_ref[0]


def _cond_linear(c, w, b, tn):
    M, D = c.shape
    L, _, N = w.shape
    return pl.pallas_call(
        _cond_kernel,
        grid=(L, N // tn),
        in_specs=[
            pl.BlockSpec((M, D), lambda l, j: (0, 0)),
            pl.BlockSpec((1, D, tn), lambda l, j: (l, 0, j)),
            pl.BlockSpec((1, 1, tn), lambda l, j: (l, 0, j)),
        ],
        out_specs=pl.BlockSpec((1, M, tn), lambda l, j: (l, 0, j)),
        out_shape=jax.ShapeDtypeStruct((L, M, N), F32),
        compiler_params=_cparams("parallel", "parallel"),
        name="cond_linear",
    )(c, w, b.reshape(L, 1, N))


def _modproj_kernel(x_ref, sc_ref, sh_ref, w_ref, o_ref):
    u = (x_ref[...] * (1.0 + sc_ref[0]) + sh_ref[0]).astype(BF16)
    o_ref[...] = _dot(u, w_ref[...]).astype(o_ref.dtype)


def _modproj(x, sc, sh, w, T, tm, out_dtype):
    N, D = x.shape
    Dout = w.shape[1]
    sc3, sc_spec = _row_mod(sc, T, tm)
    sh3, sh_spec = _row_mod(sh, T, tm)
    return pl.pallas_call(
        _modproj_kernel,
        grid=(N // tm,),
        in_specs=[
            pl.BlockSpec((tm, D), lambda i: (i, 0)),
            sc_spec,
            sh_spec,
            pl.BlockSpec((D, Dout), lambda i: (0, 0)),
        ],
        out_specs=pl.BlockSpec((tm, Dout), lambda i: (i, 0)),
        out_shape=jax.ShapeDtypeStruct((N, Dout), out_dtype),
        compiler_params=_cparams("parallel"),
        name="modproj",
    )(x, sc3, sh3, w)


def _kv_kernel(x_ref, sc_ref, sh_ref, w_ref, bf_ref, k_ref, v_ref, kb_ref, vb_ref, lf_ref, cum_ref,
               carry_ref, *, T, tm):
    u = (x_ref[...] * (1.0 + sc_ref[0]) + sh_ref[0]).astype(BF16)
    p = _dot(u, w_ref[...])
    k = p[:, :FOX_DIM]
    v = p[:, FOX_DIM:2 * FOX_DIM]
    k_ref[...] = k
    v_ref[...] = v
    kb_ref[...] = k.astype(BF16)
    vb_ref[...] = v.astype(BF16)
    lf = _log_sigmoid(p[:, 2 * FOX_DIM:] + bf_ref[...])
    lf_ref[...] = lf
    if T % tm == 0:
        @pl.when(pl.program_id(0) % (T // tm) == 0)
        def _():
            carry_ref[...] = jnp.zeros_like(carry_ref)

        sb = LANES
        r = lax.broadcasted_iota(jnp.int32, (sb, sb), 0)
        c = lax.broadcasted_iota(jnp.int32, (sb, sb), 1)
        tri = (c <= r).astype(BF16)
        carry = carry_ref[...]
        for s in range(tm // sb):
            cs = _dot_exact_lhs(tri, lf[s * sb:(s + 1) * sb]) + carry
            cum_ref[s * sb:(s + 1) * sb, :] = cs
            carry = cs[sb - 1:sb, :]
        carry_ref[...] = carry
    else:
        r = lax.broadcasted_iota(jnp.int32, (tm, tm), 0)
        c = lax.broadcasted_iota(jnp.int32, (tm, tm), 1)
        tri = ((c <= r) & ((r // T) == (c // T))).astype(BF16)
        cum_ref[...] = _dot_exact_lhs(tri, lf)


def _kv_proj(x, sc, sh, w, bf, T, tm):
    N, D = x.shape
    sc3, sc_spec = _row_mod(sc, T, tm)
    sh3, sh_spec = _row_mod(sh, T, tm)
    row = lambda i: (i, 0)
    return pl.pallas_call(
        functools.partial(_kv_kernel, T=T, tm=tm),
        grid=(N // tm,),
        in_specs=[
            pl.BlockSpec((tm, D), row),
            sc_spec,
            sh_spec,
            pl.BlockSpec((D, KV_PAD), lambda i: (0, 0)),
            pl.BlockSpec((1, LANES), lambda i: (0, 0)),
        ],
        out_specs=[
            pl.BlockSpec((tm, FOX_DIM), row),
            pl.BlockSpec((tm, FOX_DIM), row),
            pl.BlockSpec((tm, FOX_DIM), row),
            pl.BlockSpec((tm, FOX_DIM), row),
            pl.BlockSpec((tm, LANES), row),
            pl.BlockSpec((tm, LANES), row),
        ],
        out_shape=[
            jax.ShapeDtypeStruct((N, FOX_DIM), F32),
            jax.ShapeDtypeStruct((N, FOX_DIM), F32),
            jax.ShapeDtypeStruct((N, FOX_DIM), BF16),
            jax.ShapeDtypeStruct((N, FOX_DIM), BF16),
            jax.ShapeDtypeStruct((N, LANES), F32),
            jax.ShapeDtypeStruct((N, LANES), F32),
        ],
        scratch_shapes=[pltpu.VMEM((1, LANES), F32)],
        compiler_params=_cparams("arbitrary"),
        name="kv_proj",
    )(x, sc3, sh3, w, bf)


def _gla_kernel(q_ref, k_ref, v_ref, g_ref, a_ref, wa_ref, ba_ref, ng_ref, s0_ref, o_ref, st_ref,
                state_ref, *, c, nsub, t_valid, hps):
    t = pl.program_id(2)

    @pl.when(t == 0)
    def _():
        for hh in range(hps):
            state_ref[hh] = s0_ref[0, hh].T

    tc = c * nsub
    r = lax.broadcasted_iota(jnp.int32, (tc, tc), 0)
    cc = lax.broadcasted_iota(jnp.int32, (tc, tc), 1)
    causal = (cc <= r) & ((r // c) == (cc // c))
    tri = causal.astype(BF16)
    scale = GLA_DKH ** -0.5
    z_all = _dot(a_ref[...].astype(BF16), wa_ref[...]) + ba_ref[...]
    finals = []
    for hh in range(hps):
        ks = slice(hh * GLA_DKH, (hh + 1) * GLA_DKH)
        vs = slice(hh * GLA_DVH, (hh + 1) * GLA_DVH)
        la = _log_sigmoid(z_all[:, ks]) / GLA_GATE_TAU
        if t_valid < c:
            la = jnp.where(lax.broadcasted_iota(jnp.int32, la.shape, 0) % c < t_valid, la, 0.0)
        bc = _dot_exact_lhs(tri, la)
        b_last = [bc[(i + 1) * c - 1:(i + 1) * c, :] for i in range(nsub)]
        b_end = jnp.concatenate([jnp.broadcast_to(b, (c, b.shape[1])) for b in b_last], axis=0)
        k = k_ref[:, ks]
        v = v_ref[:, vs].astype(BF16)
        q_dec = ((q_ref[:, ks] * scale) * jnp.exp(bc)).astype(BF16)
        k_dec = (k * jnp.exp(-bc)).astype(BF16)
        k_end = (k * jnp.exp(b_end - bc)).astype(BF16)
        att = jnp.where(causal, _dot_nt(q_dec, k_dec), 0.0).astype(BF16)
        intra = _dot(att, v)
        st = state_ref[hh]
        inter = []
        for i in range(nsub):
            sl = slice(i * c, (i + 1) * c)
            inter.append(_dot_nt(q_dec[sl], st.astype(BF16)))
            st = st * jnp.exp(b_last[i]) + _dot_tn(v[sl], k_end[sl])
        state_ref[hh] = st
        finals.append(st)
        o = jnp.concatenate(inter, axis=0) + intra
        o = o * lax.rsqrt(jnp.mean(o * o, axis=-1, keepdims=True) + RMS_EPS) * ng_ref[...]
        o_ref[:, vs] = (o * _silu(g_ref[:, vs])).astype(o_ref.dtype)

    @pl.when(t == pl.num_programs(2) - 1)
    def _():
        for hh in range(hps):
            st_ref[0, hh] = finals[hh].T


def _gla(p, wa, ba, ng, s0, B, Tp, tc, t_valid, hps):
    nt = Tp // tc
    c = GLA_CHUNK
    row = lambda b, h, t: b * nt + t
    dk, dv = hps * GLA_DKH, hps * GLA_DVH
    kq = GLA_DK // dk
    kv = 2 * GLA_DK // dv
    kg = kv + GLA_HEADS // hps
    ka = GLA_IN_MAIN // LANES
    return pl.pallas_call(
        functools.partial(_gla_kernel, c=c, nsub=tc // c, t_valid=t_valid, hps=hps),
        grid=(B, GLA_HEADS // hps, nt),
        in_specs=[
            pl.BlockSpec((tc, dk), lambda b, h, t: (row(b, h, t), h)),
            pl.BlockSpec((tc, dk), lambda b, h, t: (row(b, h, t), kq + h)),
            pl.BlockSpec((tc, dv), lambda b, h, t: (row(b, h, t), kv + h)),
            pl.BlockSpec((tc, dv), lambda b, h, t: (row(b, h, t), kg + h)),
            pl.BlockSpec((tc, LANES), lambda b, h, t: (row(b, h, t), ka)),
            pl.BlockSpec((LANES, dk), lambda b, h, t: (0, h)),
            pl.BlockSpec((1, dk), lambda b, h, t: (0, h)),
            pl.BlockSpec((1, GLA_DVH), lambda b, h, t: (0, 0)),
            pl.BlockSpec((1, hps, GLA_DKH, GLA_DVH), lambda b, h, t: (b, h, 0, 0)),
        ],
        out_specs=[
            pl.BlockSpec((tc, dv), lambda b, h, t: (row(b, h, t), h)),
            pl.BlockSpec((1, hps, GLA_DKH, GLA_DVH), lambda b, h, t: (b, h, 0, 0)),
        ],
        out_shape=[
            jax.ShapeDtypeStruct((B * Tp, GLA_DV), BF16),
            jax.ShapeDtypeStruct((B, GLA_HEADS, GLA_DKH, GLA_DVH), F32),
        ],
        scratch_shapes=[pltpu.VMEM((hps, GLA_DVH, GLA_DKH), F32)],
        compiler_params=_cparams("parallel", "parallel", "arbitrary"),
        name="gla",
    )(p, p, p, p, p, wa, ba, ng, s0)


def _to_row_tiles(x):
    parts = [x[:, c * LANES:(c + 1) * LANES] for c in range(x.shape[1] // LANES)]
    return jnp.swapaxes(jnp.stack(parts, axis=0), 0, 1)


def _from_row_tiles(x3):
    xt = jnp.swapaxes(x3, 0, 1)
    return jnp.concatenate([xt[c] for c in range(xt.shape[0])], axis=1)


def _proj_res_ln_kernel(a_ref, w_ref, x_ref, gate_ref, lg_ref, lb_ref, sc_ref, sh_ref, xo_ref, uo_ref, u3_ref):
    y = _dot(a_ref[...].astype(BF16), w_ref[...])
    h = DEEPNORM_ALPHA * x_ref[...] + (1.0 + gate_ref[0]) * y
    xn = _layer_norm(h, lg_ref[...], lb_ref[...])
    xo_ref[...] = xn
    u = xn * (1.0 + sc_ref[0]) + sh_ref[0]
    uo_ref[...] = u.astype(uo_ref.dtype)
    u3_ref[...] = _to_row_tiles(u.astype(u3_ref.dtype))


def _proj_res_ln(a, w, x, gate, lg, lb, sc, sh, T, tm):
    N, D = x.shape
    g3, g_spec = _row_mod(gate, T, tm)
    sc3, sc_spec = _row_mod(sc, T, tm)
    sh3, sh_spec = _row_mod(sh, T, tm)
    row = lambda i: (i, 0)
    fixed = lambda i: (0, 0)
    return pl.pallas_call(
        _proj_res_ln_kernel,
        grid=(N // tm,),
        in_specs=[
            pl.BlockSpec((tm, a.shape[1]), row),
            pl.BlockSpec(w.shape, fixed),
            pl.BlockSpec((tm, D), row),
            g_spec,
            pl.BlockSpec((1, D), fixed),
            pl.BlockSpec((1, D), fixed),
            sc_spec,
            sh_spec,
        ],
        out_specs=[pl.BlockSpec((tm, D), row), pl.BlockSpec((tm, D), row),
                   pl.BlockSpec((tm, D // LANES, LANES), lambda i: (i, 0, 0))],
        out_shape=[jax.ShapeDtypeStruct((N, D), F32), jax.ShapeDtypeStruct((N, D), BF16),
                   jax.ShapeDtypeStruct((N, D // LANES, LANES), BF16)],
        compiler_params=_cparams("parallel"),
        name="proj_res_ln",
    )(a, w, x, g3, lg.reshape(1, D), lb.reshape(1, D), sc3, sh3)


def _router_kernel(u_ref, wr_ref, rb_ref, idx_ref, w_ref, rank_ref, cnt_ref, carry_ref, *, tm):
    @pl.when(pl.program_id(0) == 0)
    def _():
        carry_ref[...] = jnp.zeros_like(carry_ref)

    s = _sigmoid(_dot_nt(wr_ref[...], u_ref[...]))
    sb = s + rb_ref[...]
    neg = -jnp.inf
    sub = lax.broadcasted_iota(jnp.int32, (GROUP_SIZE, tm), 0)
    gs = GROUP_SIZE
    s_g = [s[g * gs:(g + 1) * gs] for g in range(N_GROUPS)]
    sb_g = [sb[g * gs:(g + 1) * gs] for g in range(N_GROUPS)]
    eid_g = [sub + g * gs for g in range(N_GROUPS)]

    score = []
    for g in range(N_GROUPS):
        m1 = jnp.max(sb_g[g], axis=0, keepdims=True)
        first = jnp.min(jnp.where(sb_g[g] == m1, sub, gs), axis=0, keepdims=True)
        m2 = jnp.max(jnp.where(sub == first, neg, sb_g[g]), axis=0, keepdims=True)
        score.append(m1 + m2)

    chosen = [jnp.zeros((1, tm), jnp.int32) for _ in range(N_GROUPS)]
    for _ in range(TOPK_GROUPS):
        m = score[0]
        for g in range(1, N_GROUPS):
            m = jnp.maximum(m, score[g])
        gi = jnp.full((1, tm), N_GROUPS, jnp.int32)
        for g in range(N_GROUPS - 1, -1, -1):
            gi = jnp.where(score[g] == m, g, gi)
        for g in range(N_GROUPS):
            hit = gi == g
            chosen[g] = jnp.where(hit, 1, chosen[g])
            score[g] = jnp.where(hit, neg, score[g])
    cand = [jnp.where(chosen[g] > 0, sb_g[g], neg) for g in range(N_GROUPS)]

    onehot = [jnp.zeros((gs, tm), F32) for _ in range(N_GROUPS)]
    idxs, ws = [], []
    for _ in range(TOP_K):
        mx = cand[0]
        for g in range(1, N_GROUPS):
            mx = jnp.maximum(mx, cand[g])
        m = jnp.max(mx, axis=0, keepdims=True)
        lo = jnp.where(cand[0] == m, eid_g[0], N_EXPERTS)
        for g in range(1, N_GROUPS):
            lo = jnp.minimum(lo, jnp.where(cand[g] == m, eid_g[g], N_EXPERTS))
        idx = jnp.min(lo, axis=0, keepdims=True)
        wsum = jnp.zeros((gs, tm), F32)
        for g in range(N_GROUPS):
            hit = eid_g[g] == idx
            wsum = wsum + jnp.where(hit, s_g[g], 0.0)
            cand[g] = jnp.where(hit, neg, cand[g])
            onehot[g] = jnp.where(hit, 1.0, onehot[g])
        idxs.append(idx)
        ws.append(jnp.sum(wsum, axis=0, keepdims=True))
    total = ws[0]
    for k in range(1, TOP_K):
        total = total + ws[k]
    for k in range(TOP_K):
        idx_ref[k:k + 1, :] = idxs[k]
        w_ref[k:k + 1, :] = ws[k] / total * ROUTED_SCALE

    oh = jnp.concatenate(onehot, axis=0)
    r = lax.broadcasted_iota(jnp.int32, (tm, tm), 0)
    c = lax.broadcasted_iota(jnp.int32, (tm, tm), 1)
    before = (r < c).astype(BF16)
    pre = _dot(oh.astype(BF16), before) + carry_ref[...]
    for k in range(TOP_K):
        acc = jnp.zeros((gs, tm), F32)
        for g in range(N_GROUPS):
            acc = acc + jnp.where(eid_g[g] == idxs[k], pre[g * gs:(g + 1) * gs], 0.0)
        rank_ref[k:k + 1, :] = jnp.sum(acc, axis=0, keepdims=True).astype(jnp.int32)
    carry_ref[...] = carry_ref[...] + jnp.sum(oh, axis=1, keepdims=True)
    cnt_ref[...] = jnp.broadcast_to(carry_ref[...], cnt_ref.shape).astype(jnp.int32)


def _router(u, wr_t, rb, tm):
    N, D = u.shape
    col = lambda i: (0, i)
    fixed = lambda i: (0, 0)
    return pl.pallas_call(
        functools.partial(_router_kernel, tm=tm),
        grid=(N // tm,),
        in_specs=[
            pl.BlockSpec((tm, D), lambda i: (i, 0)),
            pl.BlockSpec((N_EXPERTS, D), fixed),
            pl.BlockSpec((N_EXPERTS, 1), fixed),
        ],
        out_specs=[
            pl.BlockSpec((TOP_K, tm), col),
            pl.BlockSpec((TOP_K, tm), col),
            pl.BlockSpec((TOP_K, tm), col),
            pl.BlockSpec((N_EXPERTS, LANES), fixed),
        ],
        out_shape=[
            jax.ShapeDtypeStruct((TOP_K, N), jnp.int32),
            jax.ShapeDtypeStruct((TOP_K, N), F32),
            jax.ShapeDtypeStruct((TOP_K, N), jnp.int32),
            jax.ShapeDtypeStruct((N_EXPERTS, LANES), jnp.int32),
        ],
        scratch_shapes=[pltpu.VMEM((N_EXPERTS, 1), F32)],
        compiler_params=_cparams("arbitrary"),
        name="router",
    )(u, wr_t, rb)


def _experts_kernel(be_ref, live_ref, x_ref, wg_ref, wu_ref, wd_ref, o_ref, wgb_ref, wub_ref, wdb_ref):
    i = pl.program_id(0)
    e = be_ref[i]
    prev = be_ref[jnp.maximum(i - 1, 0)]

    @pl.when((i == 0) | (e != prev))
    def _():
        wgb_ref[...] = wg_ref[...].astype(BF16)
        wub_ref[...] = wu_ref[...].astype(BF16)
        wdb_ref[...] = wd_ref[...].astype(BF16)

    @pl.when(live_ref[i] > 0)
    def _():
        x = _from_row_tiles(x_ref[...]).astype(BF16)
        h = (_silu(_dot(x, wgb_ref[...])) * _dot(x, wub_ref[...])).astype(BF16)
        o_ref[...] = _to_row_tiles(_dot(h, wdb_ref[...]))

    @pl.when(live_ref[i] == 0)
    def _():
        o_ref[...] = jnp.zeros_like(o_ref)


EXPERT_ROWS = 512


def _experts(xs3, block_e, live, wg, wu, wd, layer, bm):
    R, S, _ = xs3.shape
    D = S * LANES
    return pl.pallas_call(
        _experts_kernel,
        grid_spec=pltpu.PrefetchScalarGridSpec(
            num_scalar_prefetch=2,
            grid=(R // bm,),
            in_specs=[
                pl.BlockSpec((bm, S, LANES), lambda i, be, lv: (i * lv[i], 0, 0)),
                pl.BlockSpec((None, None, D, D_EXPERT), lambda i, be, lv: (layer, be[i], 0, 0)),
                pl.BlockSpec((None, None, D, D_EXPERT), lambda i, be, lv: (layer, be[i], 0, 0)),
                pl.BlockSpec((None, None, D_EXPERT, D), lambda i, be, lv: (layer, be[i], 0, 0)),
            ],
            out_specs=pl.BlockSpec((bm, S, LANES), lambda i, be, lv: (i, 0, 0)),
            scratch_shapes=[
                pltpu.VMEM((D, D_EXPERT), BF16),
                pltpu.VMEM((D, D_EXPERT), BF16),
                pltpu.VMEM((D_EXPERT, D), BF16),
            ],
        ),
        out_shape=jax.ShapeDtypeStruct((R, S, LANES), F32),
        compiler_params=_cparams("arbitrary"),
        name="experts",
    )(block_e, live, xs3, wg, wu, wd)


ISSUE_TOKENS = 2


ZERO_ROWS = 128


def _dispatch_kernel(pad_end_ref, clear_from_ref, dest_ref, u_ref, *rest, tmd, bm, first):
    if first:
        xs_hbm, sem, zero_ref, zsem = rest

        @pl.when(pl.program_id(0) == 0)
        def _():
            zero_ref[...] = jnp.zeros_like(zero_ref)

            def clear_block(b, carry):
                for s in range(bm // ZERO_ROWS):
                    rows = pl.ds(b * bm + s * ZERO_ROWS, ZERO_ROWS)
                    pltpu.make_async_copy(zero_ref, xs_hbm.at[rows], zsem).start()
                return carry

            def block_cleared(b, carry):
                for s in range(bm // ZERO_ROWS):
                    pltpu.make_async_copy(zero_ref, xs_hbm.at[pl.ds(0, ZERO_ROWS)], zsem).wait()
                return carry

            def clear(e, carry):
                return lax.fori_loop(clear_from_ref[e] // bm, pad_end_ref[e] // bm, clear_block, carry)

            def cleared(e, carry):
                return lax.fori_loop(clear_from_ref[e] // bm, pad_end_ref[e] // bm, block_cleared, carry)

            first_unused = pad_end_ref[N_EXPERTS - 1] // bm
            n_blocks = xs_hbm.shape[0] // bm
            lax.fori_loop(0, N_EXPERTS, clear, 0)
            lax.fori_loop(first_unused, n_blocks, clear_block, 0)
            lax.fori_loop(0, N_EXPERTS, cleared, 0)
            lax.fori_loop(first_unused, n_blocks, block_cleared, 0)
    else:
        _, xs_hbm, sem = rest

    def issue(t2, carry):
        base = t2 * (ISSUE_TOKENS * TOP_K)
        rows = [dest_ref[base + r] for r in range(ISSUE_TOKENS * TOP_K)]
        for r, row in enumerate(rows):
            pltpu.make_async_copy(u_ref.at[t2 * ISSUE_TOKENS + r // TOP_K], xs_hbm.at[row], sem).start(priority=r % 2)
        return carry

    lax.fori_loop(0, tmd // ISSUE_TOKENS, issue, 0)
    for k in range(TOP_K):
        pltpu.make_async_copy(u_ref, xs_hbm.at[pl.ds(0, tmd)], sem).wait()


def _dispatch(u3, dest, xs_prev, pad_end, clear_from, R, tmd, bm):
    N, S, _ = u3.shape
    first = xs_prev is None
    in_specs = [
        pl.BlockSpec((tmd * TOP_K,), lambda i, pe, pd: (i,), memory_space=pltpu.SMEM),
        pl.BlockSpec((tmd, S, LANES), lambda i, pe, pd: (i, 0, 0)),
    ]
    scratch = [pltpu.SemaphoreType.DMA(())]
    if first:
        scratch += [pltpu.VMEM((ZERO_ROWS, S, LANES), u3.dtype), pltpu.SemaphoreType.DMA(())]
    else:
        in_specs.append(pl.BlockSpec(memory_space=pl.ANY))
    return pl.pallas_call(
        functools.partial(_dispatch_kernel, tmd=tmd, bm=bm, first=first),
        grid_spec=pltpu.PrefetchScalarGridSpec(
            num_scalar_prefetch=2,
            grid=(N // tmd,),
            in_specs=in_specs,
            out_specs=pl.BlockSpec(memory_space=pl.ANY),
            scratch_shapes=scratch,
        ),
        out_shape=jax.ShapeDtypeStruct((R, S, LANES), u3.dtype),
        input_output_aliases={} if first else {4: 0},
        compiler_params=_cparams("arbitrary"),
        name="moe_dispatch",
    )(pad_end, clear_from, dest, u3, *(() if first else (xs_prev,)))


def _combine_kernel(dest_ref, w_ref, y_hbm, o_ref, buf_ref, sem, *, tmc):
    def issue(t2, carry):
        first = t2 * (ISSUE_TOKENS * TOP_K)
        rows = [dest_ref[first + r] for r in range(ISSUE_TOKENS * TOP_K)]
        for r, row in enumerate(rows):
            pltpu.make_async_copy(y_hbm.at[row], buf_ref.at[first + r], sem).start(priority=r % 2)
        return carry

    lax.fori_loop(0, tmc // ISSUE_TOKENS, issue, 0)
    pltpu.make_async_copy(y_hbm.at[pl.ds(0, tmc * TOP_K)], buf_ref, sem).wait()

    def reduce(t, carry):
        acc = w_ref[t * TOP_K] * buf_ref[t * TOP_K]
        for k in range(1, TOP_K):
            acc = acc + w_ref[t * TOP_K + k] * buf_ref[t * TOP_K + k]
        o_ref[t] = acc
        return carry

    lax.fori_loop(0, tmc, reduce, 0)


def _combine(y3, dest, w, N, tmc):
    _, S, _ = y3.shape
    return pl.pallas_call(
        functools.partial(_combine_kernel, tmc=tmc),
        grid=(N // tmc,),
        in_specs=[
            pl.BlockSpec((tmc * TOP_K,), lambda i: (i,), memory_space=pltpu.SMEM),
            pl.BlockSpec((tmc * TOP_K,), lambda i: (i,), memory_space=pltpu.SMEM),
            pl.BlockSpec(memory_space=pl.ANY),
        ],
        out_specs=pl.BlockSpec((tmc, S, LANES), lambda i: (i, 0, 0)),
        out_shape=jax.ShapeDtypeStruct((N, S, LANES), F32),
        scratch_shapes=[pltpu.VMEM((tmc * TOP_K, S, LANES), F32), pltpu.SemaphoreType.DMA(())],
        compiler_params=_cparams("arbitrary"),
        name="moe_combine",
    )(dest, w, y3)


def _moe_final_kernel(u_ref, x_ref, r_ref, wgu_ref, wd_ref, gate_ref, lg_ref, lb_ref, xo_ref):
    gu = _dot(u_ref[...], wgu_ref[...])
    h = (_silu(gu[:, :D_SHARED]) * gu[:, D_SHARED:]).astype(BF16)
    y = _from_row_tiles(r_ref[...]) + _dot(h, wd_ref[...])
    hres = DEEPNORM_ALPHA * x_ref[...] + (1.0 + gate_ref[0]) * y
    xo_ref[...] = _layer_norm(hres, lg_ref[...], lb_ref[...])


def _moe_final(u, x, routed, wgu, wd, gate, lg, lb, T, tm):
    N, D = x.shape
    g3, g_spec = _row_mod(gate, T, tm)
    row = lambda i: (i, 0)
    fixed = lambda i: (0, 0)
    return pl.pallas_call(
        _moe_final_kernel,
        grid=(N // tm,),
        in_specs=[
            pl.BlockSpec((tm, D), row),
            pl.BlockSpec((tm, D), row),
            pl.BlockSpec((tm, D // LANES, LANES), lambda i: (i, 0, 0)),
            pl.BlockSpec(wgu.shape, fixed),
            pl.BlockSpec(wd.shape, fixed),
            g_spec,
            pl.BlockSpec((1, D), fixed),
            pl.BlockSpec((1, D), fixed),
        ],
        out_specs=pl.BlockSpec((tm, D), row),
        out_shape=jax.ShapeDtypeStruct((N, D), F32),
        compiler_params=_cparams("parallel"),
        name="moe_final",
    )(u, x, routed, wgu, wd, g3, lg.reshape(1, D), lb.reshape(1, D))


LOG2E = 1.4426950408889634


def _fox_kernel(qi_ref, kj_ref, q_ref, k_ref, v_ref, fq_ref, fk_ref, o_ref, m_ref, l_ref, acc_ref, *, bq):
    i = qi_ref[pl.program_id(1)]
    j = kj_ref[pl.program_id(1)]

    @pl.when(j == 0)
    def _():
        m_ref[...] = jnp.full_like(m_ref, MASK_NEG)
        l_ref[...] = jnp.zeros_like(l_ref)
        acc_ref[...] = jnp.zeros_like(acc_ref)

    def block(diagonal):
        if diagonal:
            visible = (lax.broadcasted_iota(jnp.int32, (bq, bq), 0) <= lax.broadcasted_iota(jnp.int32, (bq, bq), 1))
        qk_scale = FOX_HD ** -0.5 * LOG2E
        fq2 = fq_ref[...] * LOG2E
        fk2 = fk_ref[...] * LOG2E
        for h in range(FOX_HEADS):
            hs = slice(h * FOX_HD, (h + 1) * FOX_HD)
            s = _dot_nt(k_ref[:, hs], q_ref[:, hs]) * qk_scale + (fq2[h:h + 1, :] - fk2[:, h:h + 1])
            if diagonal:
                s = jnp.where(visible, s, MASK_NEG)
            m_prev = m_ref[h:h + 1, :]
            m_new = jnp.maximum(m_prev, jnp.max(s, axis=0, keepdims=True))
            alpha = jnp.exp2(m_prev - m_new)
            p = jnp.exp2(s - m_new)
            l_ref[h:h + 1, :] = alpha * l_ref[h:h + 1, :] + jnp.sum(p, axis=0, keepdims=True)
            acc_ref[hs, :] = alpha * acc_ref[hs, :] + _dot_tn(v_ref[:, hs], p.astype(BF16))
            m_ref[h:h + 1, :] = m_new

    pl.when(j < i)(lambda: block(False))

    @pl.when(j == i)
    def _():
        block(True)
        for h in range(FOX_HEADS):
            hs = slice(h * FOX_HD, (h + 1) * FOX_HD)
            o_ref[:, hs] = (acc_ref[hs, :] / l_ref[h:h + 1, :]).T.astype(o_ref.dtype)


def _fox_attention(q, kb, vb, cum, cum_t, B, T, bq):
    nq = T // bq
    pairs = [(i, j) for i in range(nq) for j in range(i + 1)]
    qi = jnp.array([p[0] for p in pairs], jnp.int32)
    kj = jnp.array([p[1] for p in pairs], jnp.int32)
    q_map = lambda b, s, qi, kj: (b * nq + qi[s], 0)
    kv_map = lambda b, s, qi, kj: (b * nq + kj[s], 0)
    return pl.pallas_call(
        functools.partial(_fox_kernel, bq=bq),
        grid_spec=pltpu.PrefetchScalarGridSpec(
            num_scalar_prefetch=2,
            grid=(B, len(pairs)),
            in_specs=[
                pl.BlockSpec((bq, FOX_DIM), q_map),
                pl.BlockSpec((bq, FOX_DIM), kv_map),
                pl.BlockSpec((bq, FOX_DIM), kv_map),
                pl.BlockSpec((FOX_HEADS, bq), lambda b, s, qi, kj: (b, qi[s])),
                pl.BlockSpec((bq, LANES), kv_map),
            ],
            out_specs=pl.BlockSpec((bq, FOX_DIM), q_map),
            scratch_shapes=[
                pltpu.VMEM((FOX_HEADS, bq), F32),
                pltpu.VMEM((FOX_HEADS, bq), F32),
                pltpu.VMEM((FOX_DIM, bq), F32),
            ],
        ),
        out_shape=jax.ShapeDtypeStruct((B * T, FOX_DIM), BF16),
        compiler_params=_cparams("parallel", "arbitrary"),
        name="fox_attention",
    )(qi, kj, q, kb, vb, cum_t, cum)


def _decode_kernel(pt_ref, q_ref, kn_ref, vn_ref, fq_ref, fkn_ref, *rest, tq, npg):
    kc, vc, lfc = rest[:npg], rest[npg:2 * npg], rest[2 * npg:3 * npg]
    o_ref, m_ref, l_ref, acc_ref, carry_ref = rest[3 * npg:]
    j = pl.program_id(1)
    H = FOX_HEADS
    scale = FOX_HD ** -0.5
    q = [q_ref[:, h * FOX_HD:(h + 1) * FOX_HD].astype(BF16) for h in range(H)]
    fq = jnp.concatenate([fq_ref[:, h:h + 1] for h in range(H)], axis=0)

    def per_head_rows(x):
        return jnp.concatenate([jnp.broadcast_to(x[h:h + 1, :], (tq, x.shape[1])) for h in range(H)], axis=0)

    def attend(s, values, state):
        m_prev, l_prev, acc_prev = state
        m_new = jnp.maximum(m_prev, jnp.max(s, axis=1, keepdims=True))
        alpha = jnp.exp(m_prev - m_new)
        p = jnp.exp(s - m_new)
        l_new = alpha * l_prev + jnp.sum(p, axis=1, keepdims=True)
        pv = []
        for h in range(H):
            o = None
            for i, value in enumerate(values):
                ph = p[h * tq:(h + 1) * tq, i * PAGE_SIZE:(i + 1) * PAGE_SIZE].astype(BF16)
                t = _dot(ph, value(h))
                o = t if o is None else o + t
            pv.append(o)
        return m_new, l_new, alpha * acc_prev + jnp.concatenate(pv, axis=0)

    @pl.when(j == 0)
    def _():
        carry_ref[...] = jnp.zeros_like(carry_ref)
        r = lax.broadcasted_iota(jnp.int32, (H * tq, PAGE_SIZE), 0) % tq
        c = lax.broadcasted_iota(jnp.int32, (H * tq, PAGE_SIZE), 1)
        s = jnp.concatenate(
            [_dot_nt(q[h], kn_ref[0, :, h * FOX_HD:(h + 1) * FOX_HD]) for h in range(H)], axis=0)
        s = s * scale + fq - per_head_rows(fkn_ref[0])
        s = jnp.where(c <= r, s, MASK_NEG)
        state = (jnp.full((H * tq, 1), MASK_NEG, F32), jnp.zeros((H * tq, 1), F32),
                 jnp.zeros((H * tq, FOX_HD), F32))
        m_ref[...], l_ref[...], acc_ref[...] = attend(
            s, [lambda h: vn_ref[0, :, h * FOX_HD:(h + 1) * FOX_HD]], state)

    r = lax.broadcasted_iota(jnp.int32, (PAGE_SIZE, PAGE_SIZE), 0)
    c = lax.broadcasted_iota(jnp.int32, (PAGE_SIZE, PAGE_SIZE), 1)
    after = (r > c).astype(BF16)
    carry = carry_ref[...]
    blocks = []
    for i in range(npg):
        lf = lfc[i][...]
        suffix = _dot_exact_rhs(lf, after) + carry
        carry = carry + jnp.sum(lf, axis=1, keepdims=True)
        kt = jnp.swapaxes(kc[i][...].astype(BF16), 0, 1)
        sk = jnp.concatenate([_dot_nt(q[h], kt[h]) for h in range(H)], axis=0)
        blocks.append(sk * scale + per_head_rows(suffix))
    carry_ref[...] = carry
    s = jnp.concatenate(blocks, axis=1) + fq
    vts = [jnp.swapaxes(vc[i][...].astype(BF16), 0, 1) for i in range(npg)]
    values = [lambda h, i=i: vts[i][h] for i in range(npg)]
    m_new, l_new, acc_new = attend(s, values, (m_ref[...], l_ref[...], acc_ref[...]))
    m_ref[...] = m_new
    l_ref[...] = l_new
    acc_ref[...] = acc_new

    @pl.when(j == pl.num_programs(1) - 1)
    def _():
        out = acc_new / l_new
        for h in range(H):
            o_ref[:, h * FOX_HD:(h + 1) * FOX_HD] = out[h * tq:(h + 1) * tq, :]


DECODE_PAGES_PER_STEP = 16


def _decode_attention(page_table, q, k_new, v_new, cum, cum_new_t, cache_k, cache_v, cache_lf_t, B, tq):
    n_pages = page_table.shape[1]
    npg = DECODE_PAGES_PER_STEP
    assert n_pages % npg == 0

    def page(i, nd):
        return lambda b, j, pt: (pt[b * n_pages + (n_pages - 1 - (j * npg + i))],) + (0,) * nd

    kv_specs = [pl.BlockSpec((None, PAGE_SIZE, FOX_HEADS, FOX_HD), page(i, 3)) for i in range(npg)]
    lf_specs = [pl.BlockSpec((None, FOX_HEADS, PAGE_SIZE), page(i, 2)) for i in range(npg)]
    return pl.pallas_call(
        functools.partial(_decode_kernel, tq=tq, npg=npg),
        grid_spec=pltpu.PrefetchScalarGridSpec(
            num_scalar_prefetch=1,
            grid=(B, n_pages // npg),
            in_specs=[
                pl.BlockSpec((tq, FOX_DIM), lambda b, j, pt: (b, 0)),
                pl.BlockSpec((1, PAGE_SIZE, FOX_DIM), lambda b, j, pt: (b, 0, 0)),
                pl.BlockSpec((1, PAGE_SIZE, FOX_DIM), lambda b, j, pt: (b, 0, 0)),
                pl.BlockSpec((tq, LANES), lambda b, j, pt: (b, 0)),
                pl.BlockSpec((1, FOX_HEADS, PAGE_SIZE), lambda b, j, pt: (b, 0, 0)),
            ] + kv_specs + kv_specs + lf_specs,
            out_specs=pl.BlockSpec((tq, FOX_DIM), lambda b, j, pt: (b, 0)),
            scratch_shapes=[
                pltpu.VMEM((FOX_HEADS * tq, 1), F32),
                pltpu.VMEM((FOX_HEADS * tq, 1), F32),
                pltpu.VMEM((FOX_HEADS * tq, FOX_HD), F32),
                pltpu.VMEM((FOX_HEADS, 1), F32),
            ],
        ),
        out_shape=jax.ShapeDtypeStruct((B * tq, FOX_DIM), F32),
        compiler_params=_cparams("parallel", "arbitrary"),
        name="decode_attention",
    )(page_table.reshape(-1), q, k_new, v_new, cum, cum_new_t, *([cache_k] * npg), *([cache_v] * npg),
      *([cache_lf_t] * npg))


def _moe(groups, W, l):
    bm = EXPERT_ROWS
    routes = [_router(g["u"], W["router_wt"][l], W["router_bias"][l].reshape(N_EXPERTS, 1), g["tm"])
              for g in groups]
    counts = [r[3][:, 0] for r in routes]
    total = sum(counts)
    padded = (total + bm - 1) // bm * bm
    pad_end = jnp.cumsum(padded)
    n_blocks = -(-sum(g["N"] for g in groups) * TOP_K // bm) + N_EXPERTS
    starts = jnp.arange(n_blocks, dtype=jnp.int32) * bm
    block_e = jnp.minimum(jnp.sum(pad_end[None, :] <= starts[:, None], axis=1), N_EXPERTS - 1).astype(jnp.int32)
    live = (starts < pad_end[-1]).astype(jnp.int32)
    offset = pad_end - padded
    clear_from = (offset + counts[0]) // bm * bm
    xs3, dests = None, []
    for g, (idx_t, _, rank_t, _), cnt in zip(groups, routes, counts):
        first = jnp.sum(jnp.where(idx_t[:, :, None] == jnp.arange(N_EXPERTS), offset, 0), axis=-1)
        dests.append((first + rank_t).T.reshape(-1))
        xs3 = _dispatch(g["u3"], dests[-1], xs3, pad_end, clear_from, n_blocks * bm, min(512, g["N"]), bm)
        offset = offset + cnt
    y3 = _experts(xs3, block_e, live, W["exp_w_gate"], W["exp_w_up"], W["exp_w_down"], l, bm)
    for g, (_, w_t, _, _), dest in zip(groups, routes, dests):
        routed3 = _combine(y3, dest, w_t.T.reshape(-1), g["N"], min(512, g["N"]))
        g["x"] = _moe_final(g["u"], g["x"], routed3, W["sh_wgu"][l], W["sh_wd"][l], g["g_m"], W["ln_g"][l, 1],
                            W["ln_b"][l, 1], g["T"], g["tm"])


def _mixer(g, W, l):
    B, T, N, tm, past = g["B"], g["T"], g["N"], g["tm"], g["past"]
    sh_a, sc_a, g_a, sh_m, sc_m, g_m = jnp.split(g["mods"][l], 6, axis=-1)
    if l < N_A:
        p = _modproj(g["x"], sc_a, sh_a, W["gla_w_in"][l], T, tm, F32)
        if T >= GLA_CHUNK:
            Tp, tc, hps = T, min(512, T), 2
        else:
            Tp, tc, hps = GLA_CHUNK, GLA_CHUNK, GLA_HEADS
            p = jnp.pad(p.reshape(B, T, -1), ((0, 0), (0, Tp - T), (0, 0))).reshape(B * Tp, -1)
        o, s_T = _gla(p, W["gla_w_a2"][l], W["gla_b_a"][l], W["gla_norm_g"][l], g["s0"][l], B, Tp, tc, T, hps)
        if Tp != T:
            o = o.reshape(B, Tp, -1)[:, :T].reshape(N, -1)
        g["gla"].append(s_T)
        w_o = W["gla_w_o"][l]
    else:
        q = _modproj(g["x"], sc_a, sh_a, W["fox_w_q"][l - N_A], T, tm, BF16 if past is None else F32)
        if past is None:
            o = _fox_attention(q, g["kb"], g["vb"], g["cum"], g["cum_t"], B, T, min(256, T))
        else:
            o = _decode_attention(past["page_table"], q, g["kb"], g["vb"], g["cum"], g["cum_t"], past["k"],
                                  past["v"], past["lf_t"], B, T)
        w_o = W["fox_w_o"][l - N_A]
    g["x"], g["u"], g["u3"] = _proj_res_ln(o, w_o, g["x"], g_a, W["ln_g"][l, 0], W["ln_b"][l, 0], sc_m, sh_m, T, tm)
    g["g_m"] = g_m


def _shared_kv(g, W):
    B, T = g["B"], g["T"]
    sh_kv, sc_kv = jnp.split(g["kv_mod"], 2, axis=-1)
    g["k"], g["v"], kb, vb, g["lf"], g["cum"] = _kv_proj(g["x"], sc_kv, sh_kv, W["kv_w"], W["fox_b_f"], T, g["tm"])
    cum_h = g["cum"][:, :FOX_HEADS].reshape(B, T, FOX_HEADS)
    if g["past"] is None:
        g["cum_t"] = cum_h.transpose(0, 2, 1).reshape(B * FOX_HEADS, T)
        g["kb"], g["vb"] = kb, vb
    else:
        rows = ((0, 0), (0, PAGE_SIZE - T), (0, 0))
        g["cum_t"] = jnp.pad(cum_h, rows).transpose(0, 2, 1)
        g["kb"] = jnp.pad(kb.reshape(B, T, FOX_DIM), rows)
        g["vb"] = jnp.pad(vb.reshape(B, T, FOX_DIM), rows)


def _group(x, mods, kv_mod, s0, past):
    B, T, D = x.shape
    return dict(x=x.reshape(B * T, D), B=B, T=T, N=B * T, tm=min(512, B * T), mods=mods, kv_mod=kv_mod, s0=s0,
                past=past, gla=[])


def _group_outputs(g):
    B, T = g["B"], g["T"]
    return (g["x"].reshape(B, T, D_MODEL), jnp.stack(g["gla"]),
            g["k"].reshape(B, T, FOX_HEADS, FOX_HD), g["v"].reshape(B, T, FOX_HEADS, FOX_HD),
            g["lf"][:, :FOX_HEADS].reshape(B, T, FOX_HEADS))


def kernel(x_prompt, x_sample, c_prompt, c_sample, state_gla, cache_k, cache_v, cache_logf, page_table, gla_w_in, gla_w_a2, gla_b_a, gla_norm_g, gla_w_o, kv_w_mod, kv_b_mod, kv_w, fox_b_f, fox_w_q, fox_w_o, ada_w, ada_b, ln_g, ln_b, router_w, router_bias, exp_w_gate, exp_w_up, exp_w_down, sh_w_gate, sh_w_up, sh_w_down):
    Bp, Tp, D = x_prompt.shape
    Bs, Ts, _ = x_sample.shape
    lane_pad = lambda w, n: jnp.pad(w, [(0, 0)] * (w.ndim - 1) + [(0, n - w.shape[-1])])
    W = dict(
        gla_w_in=jnp.concatenate(
            [gla_w_in[..., :GLA_IN_MAIN], lane_pad(gla_w_in[..., GLA_IN_MAIN:], LANES)], axis=-1).astype(BF16),
        gla_w_a2=jnp.pad(gla_w_a2, ((0, 0), (0, LANES - GLA_GATE_RANK), (0, 0))).astype(BF16),
        gla_b_a=gla_b_a.reshape(N_A, 1, GLA_DK),
        gla_norm_g=gla_norm_g.reshape(N_A, 1, GLA_DVH),
        gla_w_o=gla_w_o.astype(BF16),
        kv_w=jnp.concatenate([kv_w[:, :2 * FOX_DIM], lane_pad(kv_w[:, 2 * FOX_DIM:], LANES)], axis=-1).astype(BF16),
        fox_b_f=lane_pad(fox_b_f.reshape(1, FOX_HEADS), LANES),
        fox_w_q=fox_w_q.astype(BF16),
        fox_w_o=fox_w_o.astype(BF16),
        ln_g=ln_g, ln_b=ln_b,
        router_wt=jnp.swapaxes(router_w, 1, 2).astype(BF16),
        router_bias=router_bias,
        exp_w_gate=exp_w_gate, exp_w_up=exp_w_up, exp_w_down=exp_w_down,
        sh_wgu=jnp.concatenate([sh_w_gate, sh_w_up], axis=-1).astype(BF16),
        sh_wd=sh_w_down.astype(BF16),
    )
    c_all = jnp.concatenate([c_prompt, c_sample], axis=0)
    mods = _cond_linear(c_all, ada_w, ada_b, 1536)
    kv_mod = _cond_linear(c_all, kv_w_mod[None], kv_b_mod[None], 1024)[0]

    s0_prompt = jnp.zeros((N_A, Bp, GLA_HEADS, GLA_DKH, GLA_DVH), F32)
    past = dict(page_table=page_table, k=cache_k, v=cache_v, lf_t=jnp.swapaxes(cache_logf, 1, 2))
    groups = [_group(x_prompt, mods[:, :Bp], kv_mod[:Bp], s0_prompt, None),
              _group(x_sample, mods[:, Bp:], kv_mod[Bp:], state_gla, past)]
    for l in range(DEPTH):
        for g in groups:
            _mixer(g, W, l)
        _moe(groups, W, l)
        if l == N_A - 1:
            for g in groups:
                _shared_kv(g, W)
    y_p, gla_p, k_p, v_p, lf_p = _group_outputs(groups[0])
    y_s, gla_s, k_s, v_s, lf_s = _group_outputs(groups[1])
    return (y_p, y_s, gla_p, gla_s, k_p, v_p, lf_p, k_s, v_s, lf_s)
```

```python
import functools

import jax
import jax.numpy as jnp
from jax import lax
from jax.experimental import pallas as pl
from jax.experimental.pallas import tpu as pltpu

F32 = jnp.float32
BF16 = jnp.bfloat16

D_MODEL = 1024
DEPTH = 4
N_A = DEPTH // 2
PAGE_SIZE = 128
GLA_HEADS = 4
GLA_DK = D_MODEL // 2
GLA_DV = D_MODEL
GLA_DKH = GLA_DK // GLA_HEADS
GLA_DVH = GLA_DV // GLA_HEADS
GLA_GATE_RANK = 16
GLA_GATE_TAU = 16.0
GLA_CHUNK = 64
FOX_HEADS = 8
FOX_HD = D_MODEL // FOX_HEADS
FOX_DIM = FOX_HEADS * FOX_HD
N_EXPERTS = 64
TOP_K = 8
N_GROUPS = 8
GROUP_SIZE = N_EXPERTS // N_GROUPS
TOPK_GROUPS = 4
D_EXPERT = 256
D_SHARED = 256
ROUTED_SCALE = 2.5
DEEPNORM_ALPHA = (2 * DEPTH) ** 0.25
LN_EPS = 1e-5
RMS_EPS = 1e-6

LANES = 128
GLA_IN_MAIN = 2 * GLA_DK + 2 * GLA_DV
GLA_IN_PAD = GLA_IN_MAIN + LANES
KV_PAD = 2 * FOX_DIM + LANES
MASK_NEG = -1e30
VMEM_LIMIT = 52 * 1024 * 1024


def _cparams(*sem):
    return pltpu.CompilerParams(dimension_semantics=sem, vmem_limit_bytes=VMEM_LIMIT)


def _dot(a, b):
    return jnp.dot(a, b, preferred_element_type=F32)


def _dot_nt(a, b):
    return lax.dot_general(a, b, (((1,), (1,)), ((), ())), preferred_element_type=F32)


def _dot_tn(a, b):
    return lax.dot_general(a, b, (((0,), (0,)), ((), ())), preferred_element_type=F32)


def _sigmoid(x):
    return 1.0 / (1.0 + jnp.exp(-x))


def _silu(x):
    return x * _sigmoid(x)


def _log_sigmoid(x):
    return jnp.minimum(x, 0.0) - jnp.log1p(jnp.exp(-jnp.abs(x)))


def _split3(x):
    hi = x.astype(BF16)
    r = x - hi.astype(F32)
    mid = r.astype(BF16)
    lo = (r - mid.astype(F32)).astype(BF16)
    return hi, mid, lo


def _dot_exact_lhs(mat01, x):
    hi, mid, lo = _split3(x)
    return _dot(mat01, hi) + _dot(mat01, mid) + _dot(mat01, lo)


def _dot_exact_rhs(x, mat01):
    hi, mid, lo = _split3(x)
    return _dot(hi, mat01) + _dot(mid, mat01) + _dot(lo, mat01)


def _layer_norm(h, g, b):
    mu = jnp.mean(h, axis=-1, keepdims=True)
    xc = h - mu
    var = jnp.mean(xc * xc, axis=-1, keepdims=True)
    return xc * lax.rsqrt(var + LN_EPS) * g + b


def _row_mod(m, T, tm):
    B, D = m.shape
    if T % tm == 0:
        tiles_per_batch = T // tm
        return m.reshape(B, 1, D), pl.BlockSpec((1, 1, D), lambda i: (i // tiles_per_batch, 0, 0))
    assert tm % T == 0
    e = jnp.repeat(m, T, axis=0).reshape(B * T // tm, tm, D)
    return e, pl.BlockSpec((1, tm, D), lambda i: (i, 0, 0))


def _cond_kernel(c_ref, w_ref, b_ref, o_ref):
    s = _silu(c_ref[...]).astype(BF16)
    o_ref[0] = _dot(s, w_ref[0].astype(BF16)) + b_ref[0]


def _cond_linear(c, w, b, tn):
    M, D = c.shape
    L, _, N = w.shape
    return pl.pallas_call(
        _cond_kernel,
        grid=(L, N // tn),
        in_specs=[
            pl.BlockSpec((M, D), lambda l, j: (0, 0)),
            pl.BlockSpec((1, D, tn), lambda l, j: (l, 0, j)),
            pl.BlockSpec((1, 1, tn), lambda l, j: (l, 0, j)),
        ],
        out_specs=pl.BlockSpec((1, M, tn), lambda l, j: (l, 0, j)),
        out_shape=jax.ShapeDtypeStruct((L, M, N), F32),
        compiler_params=_cparams("parallel", "parallel"),
        name="cond_linear",
    )(c, w, b.reshape(L, 1, N))


def _modproj_kernel(x_ref, sc_ref, sh_ref, w_ref, o_ref):
    u = (x_ref[...] * (1.0 + sc_ref[0]) + sh_ref[0]).astype(BF16)
    o_ref[...] = _dot(u, w_ref[...]).astype(o_ref.dtype)


def _modproj(x, sc, sh, w, T, tm, out_dtype):
    N, D = x.shape
    Dout = w.shape[1]
    sc3, sc_spec = _row_mod(sc, T, tm)
    sh3, sh_spec = _row_mod(sh, T, tm)
    return pl.pallas_call(
        _modproj_kernel,
        grid=(N // tm,),
        in_specs=[
            pl.BlockSpec((tm, D), lambda i: (i, 0)),
            sc_spec,
            sh_spec,
            pl.BlockSpec((D, Dout), lambda i: (0, 0)),
        ],
        out_specs=pl.BlockSpec((tm, Dout), lambda i: (i, 0)),
        out_shape=jax.ShapeDtypeStruct((N, Dout), out_dtype),
        compiler_params=_cparams("parallel"),
        name="modproj",
    )(x, sc3, sh3, w)


def _kv_kernel(x_ref, sc_ref, sh_ref, w_ref, bf_ref, k_ref, v_ref, kb_ref, vb_ref, lf_ref, cum_ref,
               carry_ref, *, T, tm):
    u = (x_ref[...] * (1.0 + sc_ref[0]) + sh_ref[0]).astype(BF16)
    p = _dot(u, w_ref[...])
    k = p[:, :FOX_DIM]
    v = p[:, FOX_DIM:2 * FOX_DIM]
    k_ref[...] = k
    v_ref[...] = v
    kb_ref[...] = k.astype(BF16)
    vb_ref[...] = v.astype(BF16)
    lf = _log_sigmoid(p[:, 2 * FOX_DIM:] + bf_ref[...])
    lf_ref[...] = lf
    if T % tm == 0:
        @pl.when(pl.program_id(0) % (T // tm) == 0)
        def _():
            carry_ref[...] = jnp.zeros_like(carry_ref)

        sb = LANES
        r = lax.broadcasted_iota(jnp.int32, (sb, sb), 0)
        c = lax.broadcasted_iota(jnp.int32, (sb, sb), 1)
        tri = (c <= r).astype(BF16)
        carry = carry_ref[...]
        for s in range(tm // sb):
            cs = _dot_exact_lhs(tri, lf[s * sb:(s + 1) * sb]) + carry
            cum_ref[s * sb:(s + 1) * sb, :] = cs
            carry = cs[sb - 1:sb, :]
        carry_ref[...] = carry
    else:
        r = lax.broadcasted_iota(jnp.int32, (tm, tm), 0)
        c = lax.broadcasted_iota(jnp.int32, (tm, tm), 1)
        tri = ((c <= r) & ((r // T) == (c // T))).astype(BF16)
        cum_ref[...] = _dot_exact_lhs(tri, lf)


def _kv_proj(x, sc, sh, w, bf, T, tm):
    N, D = x.shape
    sc3, sc_spec = _row_mod(sc, T, tm)
    sh3, sh_spec = _row_mod(sh, T, tm)
    row = lambda i: (i, 0)
    return pl.pallas_call(
        functools.partial(_kv_kernel, T=T, tm=tm),
        grid=(N // tm,),
        in_specs=[
            pl.BlockSpec((tm, D), row),
            sc_spec,
            sh_spec,
            pl.BlockSpec((D, KV_PAD), lambda i: (0, 0)),
            pl.BlockSpec((1, LANES), lambda i: (0, 0)),
        ],
        out_specs=[
            pl.BlockSpec((tm, FOX_DIM), row),
            pl.BlockSpec((tm, FOX_DIM), row),
            pl.BlockSpec((tm, FOX_DIM), row),
            pl.BlockSpec((tm, FOX_DIM), row),
            pl.BlockSpec((tm, LANES), row),
            pl.BlockSpec((tm, LANES), row),
        ],
        out_shape=[
            jax.ShapeDtypeStruct((N, FOX_DIM), F32),
            jax.ShapeDtypeStruct((N, FOX_DIM), F32),
            jax.ShapeDtypeStruct((N, FOX_DIM), BF16),
            jax.ShapeDtypeStruct((N, FOX_DIM), BF16),
            jax.ShapeDtypeStruct((N, LANES), F32),
            jax.ShapeDtypeStruct((N, LANES), F32),
        ],
        scratch_shapes=[pltpu.VMEM((1, LANES), F32)],
        compiler_params=_cparams("arbitrary"),
        name="kv_proj",
    )(x, sc3, sh3, w, bf)


def _gla_kernel(q_ref, k_ref, v_ref, g_ref, a_ref, wa_ref, ba_ref, ng_ref, s0_ref, o_ref, st_ref,
                state_ref, *, c, nsub, t_valid, hps):
    t = pl.program_id(2)

    @pl.when(t == 0)
    def _():
        for hh in range(hps):
            state_ref[hh] = s0_ref[0, hh].T

    tc = c * nsub
    r = lax.broadcasted_iota(jnp.int32, (tc, tc), 0)
    cc = lax.broadcasted_iota(jnp.int32, (tc, tc), 1)
    causal = (cc <= r) & ((r // c) == (cc // c))
    tri = causal.astype(BF16)
    scale = GLA_DKH ** -0.5
    z_all = _dot(a_ref[...].astype(BF16), wa_ref[...]) + ba_ref[...]
    finals = []
    for hh in range(hps):
        ks = slice(hh * GLA_DKH, (hh + 1) * GLA_DKH)
        vs = slice(hh * GLA_DVH, (hh + 1) * GLA_DVH)
        la = _log_sigmoid(z_all[:, ks]) / GLA_GATE_TAU
        if t_valid < c:
            la = jnp.where(lax.broadcasted_iota(jnp.int32, la.shape, 0) % c < t_valid, la, 0.0)
        bc = _dot_exact_lhs(tri, la)
        b_last = [bc[(i + 1) * c - 1:(i + 1) * c, :] for i in range(nsub)]
        b_end = jnp.concatenate([jnp.broadcast_to(b, (c, b.shape[1])) for b in b_last], axis=0)
        k = k_ref[:, ks]
        v = v_ref[:, vs].astype(BF16)
        q_dec = ((q_ref[:, ks] * scale) * jnp.exp(bc)).astype(BF16)
        k_dec = (k * jnp.exp(-bc)).astype(BF16)
        k_end = (k * jnp.exp(b_end - bc)).astype(BF16)
        att = jnp.where(causal, _dot_nt(q_dec, k_dec), 0.0).astype(BF16)
        intra = _dot(att, v)
        st = state_ref[hh]
        inter = []
        for i in range(nsub):
            sl = slice(i * c, (i + 1) * c)
            inter.append(_dot_nt(q_dec[sl], st.astype(BF16)))
            st = st * jnp.exp(b_last[i]) + _dot_tn(v[sl], k_end[sl])
        state_ref[hh] = st
        finals.append(st)
        o = jnp.concatenate(inter, axis=0) + intra
        o = o * lax.rsqrt(jnp.mean(o * o, axis=-1, keepdims=True) + RMS_EPS) * ng_ref[...]
        o_ref[:, vs] = (o * _silu(g_ref[:, vs])).astype(o_ref.dtype)

    @pl.when(t == pl.num_programs(2) - 1)
    def _():
        for hh in range(hps):
            st_ref[0, hh] = finals[hh].T


def _gla(p, wa, ba, ng, s0, B, Tp, tc, t_valid, hps):
    nt = Tp // tc
    c = GLA_CHUNK
    row = lambda b, h, t: b * nt + t
    dk, dv = hps * GLA_DKH, hps * GLA_DVH
    kq = GLA_DK // dk
    kv = 2 * GLA_DK // dv
    kg = kv + GLA_HEADS // hps
    ka = GLA_IN_MAIN // LANES
    return pl.pallas_call(
        functools.partial(_gla_kernel, c=c, nsub=tc // c, t_valid=t_valid, hps=hps),
        grid=(B, GLA_HEADS // hps, nt),
        in_specs=[
            pl.BlockSpec((tc, dk), lambda b, h, t: (row(b, h, t), h)),
            pl.BlockSpec((tc, dk), lambda b, h, t: (row(b, h, t), kq + h)),
            pl.BlockSpec((tc, dv), lambda b, h, t: (row(b, h, t), kv + h)),
            pl.BlockSpec((tc, dv), lambda b, h, t: (row(b, h, t), kg + h)),
            pl.BlockSpec((tc, LANES), lambda b, h, t: (row(b, h, t), ka)),
            pl.BlockSpec((LANES, dk), lambda b, h, t: (0, h)),
            pl.BlockSpec((1, dk), lambda b, h, t: (0, h)),
            pl.BlockSpec((1, GLA_DVH), lambda b, h, t: (0, 0)),
            pl.BlockSpec((1, hps, GLA_DKH, GLA_DVH), lambda b, h, t: (b, h, 0, 0)),
        ],
        out_specs=[
            pl.BlockSpec((tc, dv), lambda b, h, t: (row(b, h, t), h)),
            pl.BlockSpec((1, hps, GLA_DKH, GLA_DVH), lambda b, h, t: (b, h, 0, 0)),
        ],
        out_shape=[
            jax.ShapeDtypeStruct((B * Tp, GLA_DV), BF16),
            jax.ShapeDtypeStruct((B, GLA_HEADS, GLA_DKH, GLA_DVH), F32),
        ],
        scratch_shapes=[pltpu.VMEM((hps, GLA_DVH, GLA_DKH), F32)],
        compiler_params=_cparams("parallel", "parallel", "arbitrary"),
        name="gla",
    )(p, p, p, p, p, wa, ba, ng, s0)


def _to_row_tiles(x):
    parts = [x[:, c * LANES:(c + 1) * LANES] for c in range(x.shape[1] // LANES)]
    return jnp.swapaxes(jnp.stack(parts, axis=0), 0, 1)


def _from_row_tiles(x3):
    xt = jnp.swapaxes(x3, 0, 1)
    return jnp.concatenate([xt[c] for c in range(xt.shape[0])], axis=1)


def _proj_res_ln_kernel(a_ref, w_ref, x_ref, gate_ref, lg_ref, lb_ref, sc_ref, sh_ref, xo_ref, uo_ref, u3_ref):
    y = _dot(a_ref[...].astype(BF16), w_ref[...])
    h = DEEPNORM_ALPHA * x_ref[...] + (1.0 + gate_ref[0]) * y
    xn = _layer_norm(h, lg_ref[...], lb_ref[...])
    xo_ref[...] = xn
    u = xn * (1.0 + sc_ref[0]) + sh_ref[0]
    uo_ref[...] = u.astype(uo_ref.dtype)
    u3_ref[...] = _to_row_tiles(u.astype(u3_ref.dtype))


def _proj_res_ln(a, w, x, gate, lg, lb, sc, sh, T, tm):
    N, D = x.shape
    g3, g_spec = _row_mod(gate, T, tm)
    sc3, sc_spec = _row_mod(sc, T, tm)
    sh3, sh_spec = _row_mod(sh, T, tm)
    row = lambda i: (i, 0)
    fixed = lambda i: (0, 0)
    return pl.pallas_call(
        _proj_res_ln_kernel,
        grid=(N // tm,),
        in_specs=[
            pl.BlockSpec((tm, a.shape[1]), row),
            pl.BlockSpec(w.shape, fixed),
            pl.BlockSpec((tm, D), row),
            g_spec,
            pl.BlockSpec((1, D), fixed),
            pl.BlockSpec((1, D), fixed),
            sc_spec,
            sh_spec,
        ],
        out_specs=[pl.BlockSpec((tm, D), row), pl.BlockSpec((tm, D), row),
                   pl.BlockSpec((tm, D // LANES, LANES), lambda i: (i, 0, 0))],
        out_shape=[jax.ShapeDtypeStruct((N, D), F32), jax.ShapeDtypeStruct((N, D), BF16),
                   jax.ShapeDtypeStruct((N, D // LANES, LANES), BF16)],
        compiler_params=_cparams("parallel"),
        name="proj_res_ln",
    )(a, w, x, g3, lg.reshape(1, D), lb.reshape(1, D), sc3, sh3)


def _router_kernel(u_ref, wr_ref, rb_ref, idx_ref, w_ref, rank_ref, cnt_ref, carry_ref, *, tm):
    @pl.when(pl.program_id(0) == 0)
    def _():
        carry_ref[...] = jnp.zeros_like(carry_ref)

    s = _sigmoid(_dot_nt(wr_ref[...], u_ref[...]))
    sb = s + rb_ref[...]
    neg = -jnp.inf
    sub = lax.broadcasted_iota(jnp.int32, (GROUP_SIZE, tm), 0)
    gs = GROUP_SIZE
    s_g = [s[g * gs:(g + 1) * gs] for g in range(N_GROUPS)]
    sb_g = [sb[g * gs:(g + 1) * gs] for g in range(N_GROUPS)]
    eid_g = [sub + g * gs for g in range(N_GROUPS)]

    score = []
    for g in range(N_GROUPS):
        m1 = jnp.max(sb_g[g], axis=0, keepdims=True)
        first = jnp.min(jnp.where(sb_g[g] == m1, sub, gs), axis=0, keepdims=True)
        m2 = jnp.max(jnp.where(sub == first, neg, sb_g[g]), axis=0, keepdims=True)
        score.append(m1 + m2)

    chosen = [jnp.zeros((1, tm), jnp.int32) for _ in range(N_GROUPS)]
    for _ in range(TOPK_GROUPS):
        m = score[0]
        for g in range(1, N_GROUPS):
            m = jnp.maximum(m, score[g])
        gi = jnp.full((1, tm), N_GROUPS, jnp.int32)
        for g in range(N_GROUPS - 1, -1, -1):
            gi = jnp.where(score[g] == m, g, gi)
        for g in range(N_GROUPS):
            hit = gi == g
            chosen[g] = jnp.where(hit, 1, chosen[g])
            score[g] = jnp.where(hit, neg, score[g])
    cand = [jnp.where(chosen[g] > 0, sb_g[g], neg) for g in range(N_GROUPS)]

    onehot = [jnp.zeros((gs, tm), F32) for _ in range(N_GROUPS)]
    idxs, ws = [], []
    for _ in range(TOP_K):
        mx = cand[0]
        for g in range(1, N_GROUPS):
            mx = jnp.maximum(mx, cand[g])
        m = jnp.max(mx, axis=0, keepdims=True)
        lo = jnp.where(cand[0] == m, eid_g[0], N_EXPERTS)
        for g in range(1, N_GROUPS):
            lo = jnp.minimum(lo, jnp.where(cand[g] == m, eid_g[g], N_EXPERTS))
        idx = jnp.min(lo, axis=0, keepdims=True)
        wsum = jnp.zeros((gs, tm), F32)
        for g in range(N_GROUPS):
            hit = eid_g[g] == idx
            wsum = wsum + jnp.where(hit, s_g[g], 0.0)
            cand[g] = jnp.where(hit, neg, cand[g])
            onehot[g] = jnp.where(hit, 1.0, onehot[g])
        idxs.append(idx)
        ws.append(jnp.sum(wsum, axis=0, keepdims=True))
    total = ws[0]
    for k in range(1, TOP_K):
        total = total + ws[k]
    for k in range(TOP_K):
        idx_ref[k:k + 1, :] = idxs[k]
        w_ref[k:k + 1, :] = ws[k] / total * ROUTED_SCALE

    oh = jnp.concatenate(onehot, axis=0)
    r = lax.broadcasted_iota(jnp.int32, (tm, tm), 0)
    c = lax.broadcasted_iota(jnp.int32, (tm, tm), 1)
    before = (r < c).astype(BF16)
    pre = _dot(oh.astype(BF16), before) + carry_ref[...]
    for k in range(TOP_K):
        acc = jnp.zeros((gs, tm), F32)
        for g in range(N_GROUPS):
            acc = acc + jnp.where(eid_g[g] == idxs[k], pre[g * gs:(g + 1) * gs], 0.0)
        rank_ref[k:k + 1, :] = jnp.sum(acc, axis=0, keepdims=True).astype(jnp.int32)
    carry_ref[...] = carry_ref[...] + jnp.sum(oh, axis=1, keepdims=True)
    cnt_ref[...] = jnp.broadcast_to(carry_ref[...], cnt_ref.shape).astype(jnp.int32)


def _router(u, wr_t, rb, tm):
    N, D = u.shape
    col = lambda i: (0, i)
    fixed = lambda i: (0, 0)
    return pl.pallas_call(
        functools.partial(_router_kernel, tm=tm),
        grid=(N // tm,),
        in_specs=[
            pl.BlockSpec((tm, D), lambda i: (i, 0)),
            pl.BlockSpec((N_EXPERTS, D), fixed),
            pl.BlockSpec((N_EXPERTS, 1), fixed),
        ],
        out_specs=[
            pl.BlockSpec((TOP_K, tm), col),
            pl.BlockSpec((TOP_K, tm), col),
            pl.BlockSpec((TOP_K, tm), col),
            pl.BlockSpec((N_EXPERTS, LANES), fixed),
        ],
        out_shape=[
            jax.ShapeDtypeStruct((TOP_K, N), jnp.int32),
            jax.ShapeDtypeStruct((TOP_K, N), F32),
            jax.ShapeDtypeStruct((TOP_K, N), jnp.int32),
            jax.ShapeDtypeStruct((N_EXPERTS, LANES), jnp.int32),
        ],
        scratch_shapes=[pltpu.VMEM((N_EXPERTS, 1), F32)],
        compiler_params=_cparams("arbitrary"),
        name="router",
    )(u, wr_t, rb)


def _experts_kernel(be_ref, live_ref, x_ref, wg_ref, wu_ref, wd_ref, o_ref, wgb_ref, wub_ref, wdb_ref):
    i = pl.program_id(0)
    e = be_ref[i]
    prev = be_ref[jnp.maximum(i - 1, 0)]

    @pl.when((i == 0) | (e != prev))
    def _():
        wgb_ref[...] = wg_ref[...].astype(BF16)
        wub_ref[...] = wu_ref[...].astype(BF16)
        wdb_ref[...] = wd_ref[...].astype(BF16)

    @pl.when(live_ref[i] > 0)
    def _():
        x = _from_row_tiles(x_ref[...]).astype(BF16)
        h = (_silu(_dot(x, wgb_ref[...])) * _dot(x, wub_ref[...])).astype(BF16)
        o_ref[...] = _to_row_tiles(_dot(h, wdb_ref[...]))

    @pl.when(live_ref[i] == 0)
    def _():
        o_ref[...] = jnp.zeros_like(o_ref)


EXPERT_ROWS = 512


def _experts(xs3, block_e, live, wg, wu, wd, layer, bm):
    R, S, _ = xs3.shape
    D = S * LANES
    return pl.pallas_call(
        _experts_kernel,
        grid_spec=pltpu.PrefetchScalarGridSpec(
            num_scalar_prefetch=2,
            grid=(R // bm,),
            in_specs=[
                pl.BlockSpec((bm, S, LANES), lambda i, be, lv: (i * lv[i], 0, 0)),
                pl.BlockSpec((None, None, D, D_EXPERT), lambda i, be, lv: (layer, be[i], 0, 0)),
                pl.BlockSpec((None, None, D, D_EXPERT), lambda i, be, lv: (layer, be[i], 0, 0)),
                pl.BlockSpec((None, None, D_EXPERT, D), lambda i, be, lv: (layer, be[i], 0, 0)),
            ],
            out_specs=pl.BlockSpec((bm, S, LANES), lambda i, be, lv: (i, 0, 0)),
            scratch_shapes=[
                pltpu.VMEM((D, D_EXPERT), BF16),
                pltpu.VMEM((D, D_EXPERT), BF16),
                pltpu.VMEM((D_EXPERT, D), BF16),
            ],
        ),
        out_shape=jax.ShapeDtypeStruct((R, S, LANES), F32),
        compiler_params=_cparams("arbitrary"),
        name="experts",
    )(block_e, live, xs3, wg, wu, wd)


ISSUE_TOKENS = 2


ZERO_ROWS = 128


def _dispatch_kernel(pad_end_ref, clear_from_ref, dest_ref, u_ref, *rest, tmd, bm, first):
    if first:
        xs_hbm, sem, zero_ref, zsem = rest

        @pl.when(pl.program_id(0) == 0)
        def _():
            zero_ref[...] = jnp.zeros_like(zero_ref)

            def clear_block(b, carry):
                for s in range(bm // ZERO_ROWS):
                    rows = pl.ds(b * bm + s * ZERO_ROWS, ZERO_ROWS)
                    pltpu.make_async_copy(zero_ref, xs_hbm.at[rows], zsem).start()
                return carry

            def block_cleared(b, carry):
                for s in range(bm // ZERO_ROWS):
                    pltpu.make_async_copy(zero_ref, xs_hbm.at[pl.ds(0, ZERO_ROWS)], zsem).wait()
                return carry

            def clear(e, carry):
                return lax.fori_loop(clear_from_ref[e] // bm, pad_end_ref[e] // bm, clear_block, carry)

            def cleared(e, carry):
                return lax.fori_loop(clear_from_ref[e] // bm, pad_end_ref[e] // bm, block_cleared, carry)

            first_unused = pad_end_ref[N_EXPERTS - 1] // bm
            n_blocks = xs_hbm.shape[0] // bm
            lax.fori_loop(0, N_EXPERTS, clear, 0)
            lax.fori_loop(first_unused, n_blocks, clear_block, 0)
            lax.fori_loop(0, N_EXPERTS, cleared, 0)
            lax.fori_loop(first_unused, n_blocks, block_cleared, 0)
    else:
        _, xs_hbm, sem = rest

    def issue(t2, carry):
        base = t2 * (ISSUE_TOKENS * TOP_K)
        rows = [dest_ref[base + r] for r in range(ISSUE_TOKENS * TOP_K)]
        for r, row in enumerate(rows):
            pltpu.make_async_copy(u_ref.at[t2 * ISSUE_TOKENS + r // TOP_K], xs_hbm.at[row], sem).start(priority=r % 2)
        return carry

    lax.fori_loop(0, tmd // ISSUE_TOKENS, issue, 0)
    for k in range(TOP_K):
        pltpu.make_async_copy(u_ref, xs_hbm.at[pl.ds(0, tmd)], sem).wait()


def _dispatch(u3, dest, xs_prev, pad_end, clear_from, R, tmd, bm):
    N, S, _ = u3.shape
    first = xs_prev is None
    in_specs = [
        pl.BlockSpec((tmd * TOP_K,), lambda i, pe, pd: (i,), memory_space=pltpu.SMEM),
        pl.BlockSpec((tmd, S, LANES), lambda i, pe, pd: (i, 0, 0)),
    ]
    scratch = [pltpu.SemaphoreType.DMA(())]
    if first:
        scratch += [pltpu.VMEM((ZERO_ROWS, S, LANES), u3.dtype), pltpu.SemaphoreType.DMA(())]
    else:
        in_specs.append(pl.BlockSpec(memory_space=pl.ANY))
    return pl.pallas_call(
        functools.partial(_dispatch_kernel, tmd=tmd, bm=bm, first=first),
        grid_spec=pltpu.PrefetchScalarGridSpec(
            num_scalar_prefetch=2,
            grid=(N // tmd,),
            in_specs=in_specs,
            out_specs=pl.BlockSpec(memory_space=pl.ANY),
            scratch_shapes=scratch,
        ),
        out_shape=jax.ShapeDtypeStruct((R, S, LANES), u3.dtype),
        input_output_aliases={} if first else {4: 0},
        compiler_params=_cparams("arbitrary"),
        name="moe_dispatch",
    )(pad_end, clear_from, dest, u3, *(() if first else (xs_prev,)))


def _combine_kernel(dest_ref, w_ref, y_hbm, o_ref, buf_ref, sem, *, tmc):
    def issue(t2, carry):
        first = t2 * (ISSUE_TOKENS * TOP_K)
        rows = [dest_ref[first + r] for r in range(ISSUE_TOKENS * TOP_K)]
        for r, row in enumerate(rows):
            pltpu.make_async_copy(y_hbm.at[row], buf_ref.at[first + r], sem).start(priority=r % 2)
        return carry

    lax.fori_loop(0, tmc // ISSUE_TOKENS, issue, 0)
    pltpu.make_async_copy(y_hbm.at[pl.ds(0, tmc * TOP_K)], buf_ref, sem).wait()

    def reduce(t, carry):
        acc = w_ref[t * TOP_K] * buf_ref[t * TOP_K]
        for k in range(1, TOP_K):
            acc = acc + w_ref[t * TOP_K + k] * buf_ref[t * TOP_K + k]
        o_ref[t] = acc
        return carry

    lax.fori_loop(0, tmc, reduce, 0)


def _combine(y3, dest, w, N, tmc):
    _, S, _ = y3.shape
    return pl.pallas_call(
        functools.partial(_combine_kernel, tmc=tmc),
        grid=(N // tmc,),
        in_specs=[
            pl.BlockSpec((tmc * TOP_K,), lambda i: (i,), memory_space=pltpu.SMEM),
            pl.BlockSpec((tmc * TOP_K,), lambda i: (i,), memory_space=pltpu.SMEM),
            pl.BlockSpec(memory_space=pl.ANY),
        ],
        out_specs=pl.BlockSpec((tmc, S, LANES), lambda i: (i, 0, 0)),
        out_shape=jax.ShapeDtypeStruct((N, S, LANES), F32),
        scratch_shapes=[pltpu.VMEM((tmc * TOP_K, S, LANES), F32), pltpu.SemaphoreType.DMA(())],
        compiler_params=_cparams("arbitrary"),
        name="moe_combine",
    )(dest, w, y3)


def _moe_final_kernel(u_ref, x_ref, r_ref, wgu_ref, wd_ref, gate_ref, lg_ref, lb_ref, xo_ref):
    gu = _dot(u_ref[...], wgu_ref[...])
    h = (_silu(gu[:, :D_SHARED]) * gu[:, D_SHARED:]).astype(BF16)
    y = _from_row_tiles(r_ref[...]) + _dot(h, wd_ref[...])
    hres = DEEPNORM_ALPHA * x_ref[...] + (1.0 + gate_ref[0]) * y
    xo_ref[...] = _layer_norm(hres, lg_ref[...], lb_ref[...])


def _moe_final(u, x, routed, wgu, wd, gate, lg, lb, T, tm):
    N, D = x.shape
    g3, g_spec = _row_mod(gate, T, tm)
    row = lambda i: (i, 0)
    fixed = lambda i: (0, 0)
    return pl.pallas_call(
        _moe_final_kernel,
        grid=(N // tm,),
        in_specs=[
            pl.BlockSpec((tm, D), row),
            pl.BlockSpec((tm, D), row),
            pl.BlockSpec((tm, D // LANES, LANES), lambda i: (i, 0, 0)),
            pl.BlockSpec(wgu.shape, fixed),
            pl.BlockSpec(wd.shape, fixed),
            g_spec,
            pl.BlockSpec((1, D), fixed),
            pl.BlockSpec((1, D), fixed),
        ],
        out_specs=pl.BlockSpec((tm, D), row),
        out_shape=jax.ShapeDtypeStruct((N, D), F32),
        compiler_params=_cparams("parallel"),
        name="moe_final",
    )(u, x, routed, wgu, wd, g3, lg.reshape(1, D), lb.reshape(1, D))


LOG2E = 1.4426950408889634


def _fox_kernel(qi_ref, kj_ref, q_ref, k_ref, v_ref, fq_ref, fk_ref, o_ref, m_ref, l_ref, acc_ref, *, bq):
    i = qi_ref[pl.program_id(1)]
    j = kj_ref[pl.program_id(1)]

    @pl.when(j == 0)
    def _():
        m_ref[...] = jnp.full_like(m_ref, MASK_NEG)
        l_ref[...] = jnp.zeros_like(l_ref)
        acc_ref[...] = jnp.zeros_like(acc_ref)

    def block(diagonal):
        if diagonal:
            visible = (lax.broadcasted_iota(jnp.int32, (bq, bq), 0) <= lax.broadcasted_iota(jnp.int32, (bq, bq), 1))
        qk_scale = FOX_HD ** -0.5 * LOG2E
        fq2 = fq_ref[...] * LOG2E
        fk2 = fk_ref[...] * LOG2E
        for h in range(FOX_HEADS):
            hs = slice(h * FOX_HD, (h + 1) * FOX_HD)
            s = _dot_nt(k_ref[:, hs], q_ref[:, hs]) * qk_scale + (fq2[h:h + 1, :] - fk2[:, h:h + 1])
            if diagonal:
                s = jnp.where(visible, s, MASK_NEG)
            m_prev = m_ref[h:h + 1, :]
            m_new = jnp.maximum(m_prev, jnp.max(s, axis=0, keepdims=True))
            alpha = jnp.exp2(m_prev - m_new)
            p = jnp.exp2(s - m_new)
            l_ref[h:h + 1, :] = alpha * l_ref[h:h + 1, :] + jnp.sum(p, axis=0, keepdims=True)
            acc_ref[hs, :] = alpha * acc_ref[hs, :] + _dot_tn(v_ref[:, hs], p.astype(BF16))
            m_ref[h:h + 1, :] = m_new

    pl.when(j < i)(lambda: block(False))

    @pl.when(j == i)
    def _():
        block(True)
        for h in range(FOX_HEADS):
            hs = slice(h * FOX_HD, (h + 1) * FOX_HD)
            o_ref[:, hs] = (acc_ref[hs, :] / l_ref[h:h + 1, :]).T.astype(o_ref.dtype)


def _fox_attention(q, kb, vb, cum, cum_t, B, T, bq):
    nq = T // bq
    pairs = [(i, j) for i in range(nq) for j in range(i + 1)]
    qi = jnp.array([p[0] for p in pairs], jnp.int32)
    kj = jnp.array([p[1] for p in pairs], jnp.int32)
    q_map = lambda b, s, qi, kj: (b * nq + qi[s], 0)
    kv_map = lambda b, s, qi, kj: (b * nq + kj[s], 0)
    return pl.pallas_call(
        functools.partial(_fox_kernel, bq=bq),
        grid_spec=pltpu.PrefetchScalarGridSpec(
            num_scalar_prefetch=2,
            grid=(B, len(pairs)),
            in_specs=[
                pl.BlockSpec((bq, FOX_DIM), q_map),
                pl.BlockSpec((bq, FOX_DIM), kv_map),
                pl.BlockSpec((bq, FOX_DIM), kv_map),
                pl.BlockSpec((FOX_HEADS, bq), lambda b, s, qi, kj: (b, qi[s])),
                pl.BlockSpec((bq, LANES), kv_map),
            ],
            out_specs=pl.BlockSpec((bq, FOX_DIM), q_map),
            scratch_shapes=[
                pltpu.VMEM((FOX_HEADS, bq), F32),
                pltpu.VMEM((FOX_HEADS, bq), F32),
                pltpu.VMEM((FOX_DIM, bq), F32),
            ],
        ),
        out_shape=jax.ShapeDtypeStruct((B * T, FOX_DIM), BF16),
        compiler_params=_cparams("parallel", "arbitrary"),
        name="fox_attention",
    )(qi, kj, q, kb, vb, cum_t, cum)


def _decode_kernel(pt_ref, q_ref, kn_ref, vn_ref, fq_ref, fkn_ref, *rest, tq, npg):
    kc, vc, lfc = rest[:npg], rest[npg:2 * npg], rest[2 * npg:3 * npg]
    o_ref, m_ref, l_ref, acc_ref, carry_ref = rest[3 * npg:]
    j = pl.program_id(1)
    H = FOX_HEADS
    scale = FOX_HD ** -0.5
    q = [q_ref[:, h * FOX_HD:(h + 1) * FOX_HD].astype(BF16) for h in range(H)]
    fq = jnp.concatenate([fq_ref[:, h:h + 1] for h in range(H)], axis=0)

    def per_head_rows(x):
        return jnp.concatenate([jnp.broadcast_to(x[h:h + 1, :], (tq, x.shape[1])) for h in range(H)], axis=0)

    def attend(s, values, state):
        m_prev, l_prev, acc_prev = state
        m_new = jnp.maximum(m_prev, jnp.max(s, axis=1, keepdims=True))
        alpha = jnp.exp(m_prev - m_new)
        p = jnp.exp(s - m_new)
        l_new = alpha * l_prev + jnp.sum(p, axis=1, keepdims=True)
        pv = []
        for h in range(H):
            o = None
            for i, value in enumerate(values):
                ph = p[h * tq:(h + 1) * tq, i * PAGE_SIZE:(i + 1) * PAGE_SIZE].astype(BF16)
                t = _dot(ph, value(h))
                o = t if o is None else o + t
            pv.append(o)
        return m_new, l_new, alpha * acc_prev + jnp.concatenate(pv, axis=0)

    @pl.when(j == 0)
    def _():
        carry_ref[...] = jnp.zeros_like(carry_ref)
        r = lax.broadcasted_iota(jnp.int32, (H * tq, PAGE_SIZE), 0) % tq
        c = lax.broadcasted_iota(jnp.int32, (H * tq, PAGE_SIZE), 1)
        s = jnp.concatenate(
            [_dot_nt(q[h], kn_ref[0, :, h * FOX_HD:(h + 1) * FOX_HD]) for h in range(H)], axis=0)
        s = s * scale + fq - per_head_rows(fkn_ref[0])
        s = jnp.where(c <= r, s, MASK_NEG)
        state = (jnp.full((H * tq, 1), MASK_NEG, F32), jnp.zeros((H * tq, 1), F32),
                 jnp.zeros((H * tq, FOX_HD), F32))
        m_ref[...], l_ref[...], acc_ref[...] = attend(
            s, [lambda h: vn_ref[0, :, h * FOX_HD:(h + 1) * FOX_HD]], state)

    r = lax.broadcasted_iota(jnp.int32, (PAGE_SIZE, PAGE_SIZE), 0)
    c = lax.broadcasted_iota(jnp.int32, (PAGE_SIZE, PAGE_SIZE), 1)
    after = (r > c).astype(BF16)
    carry = carry_ref[...]
    blocks = []
    for i in range(npg):
        lf = lfc[i][...]
        suffix = _dot_exact_rhs(lf, after) + carry
        carry = carry + jnp.sum(lf, axis=1, keepdims=True)
        kt = jnp.swapaxes(kc[i][...].astype(BF16), 0, 1)
        sk = jnp.concatenate([_dot_nt(q[h], kt[h]) for h in range(H)], axis=0)
        blocks.append(sk * scale + per_head_rows(suffix))
    carry_ref[...] = carry
    s = jnp.concatenate(blocks, axis=1) + fq
    vts = [jnp.swapaxes(vc[i][...].astype(BF16), 0, 1) for i in range(npg)]
    values = [lambda h, i=i: vts[i][h] for i in range(npg)]
    m_new, l_new, acc_new = attend(s, values, (m_ref[...], l_ref[...], acc_ref[...]))
    m_ref[...] = m_new
    l_ref[...] = l_new
    acc_ref[...] = acc_new

    @pl.when(j == pl.num_programs(1) - 1)
    def _():
        out = acc_new / l_new
        for h in range(H):
            o_ref[:, h * FOX_HD:(h + 1) * FOX_HD] = out[h * tq:(h + 1) * tq, :]


DECODE_PAGES_PER_STEP = 16


def _decode_attention(page_table, q, k_new, v_new, cum, cum_new_t, cache_k, cache_v, cache_lf_t, B, tq):
    n_pages = page_table.shape[1]
    npg = DECODE_PAGES_PER_STEP
    assert n_pages % npg == 0

    def page(i, nd):
        return lambda b, j, pt: (pt[b * n_pages + (n_pages - 1 - (j * npg + i))],) + (0,) * nd

    kv_specs = [pl.BlockSpec((None, PAGE_SIZE, FOX_HEADS, FOX_HD), page(i, 3)) for i in range(npg)]
    lf_specs = [pl.BlockSpec((None, FOX_HEADS, PAGE_SIZE), page(i, 2)) for i in range(npg)]
    return pl.pallas_call(
        functools.partial(_decode_kernel, tq=tq, npg=npg),
        grid_spec=pltpu.PrefetchScalarGridSpec(
            num_scalar_prefetch=1,
            grid=(B, n_pages // npg),
            in_specs=[
                pl.BlockSpec((tq, FOX_DIM), lambda b, j, pt: (b, 0)),
                pl.BlockSpec((1, PAGE_SIZE, FOX_DIM), lambda b, j, pt: (b, 0, 0)),
                pl.BlockSpec((1, PAGE_SIZE, FOX_DIM), lambda b, j, pt: (b, 0, 0)),
                pl.BlockSpec((tq, LANES), lambda b, j, pt: (b, 0)),
                pl.BlockSpec((1, FOX_HEADS, PAGE_SIZE), lambda b, j, pt: (b, 0, 0)),
            ] + kv_specs + kv_specs + lf_specs,
            out_specs=pl.BlockSpec((tq, FOX_DIM), lambda b, j, pt: (b, 0)),
            scratch_shapes=[
                pltpu.VMEM((FOX_HEADS * tq, 1), F32),
                pltpu.VMEM((FOX_HEADS * tq, 1), F32),
                pltpu.VMEM((FOX_HEADS * tq, FOX_HD), F32),
                pltpu.VMEM((FOX_HEADS, 1), F32),
            ],
        ),
        out_shape=jax.ShapeDtypeStruct((B * tq, FOX_DIM), F32),
        compiler_params=_cparams("parallel", "arbitrary"),
        name="decode_attention",
    )(page_table.reshape(-1), q, k_new, v_new, cum, cum_new_t, *([cache_k] * npg), *([cache_v] * npg),
      *([cache_lf_t] * npg))


def _moe(groups, W, l):
    bm = EXPERT_ROWS
    routes = [_router(g["u"], W["router_wt"][l], W["router_bias"][l].reshape(N_EXPERTS, 1), g["tm"])
              for g in groups]
    counts = [r[3][:, 0] for r in routes]
    total = sum(counts)
    padded = (total + bm - 1) // bm * bm
    pad_end = jnp.cumsum(padded)
    n_blocks = -(-sum(g["N"] for g in groups) * TOP_K // bm) + N_EXPERTS
    starts = jnp.arange(n_blocks, dtype=jnp.int32) * bm
    block_e = jnp.minimum(jnp.sum(pad_end[None, :] <= starts[:, None], axis=1), N_EXPERTS - 1).astype(jnp.int32)
    live = (starts < pad_end[-1]).astype(jnp.int32)
    offset = pad_end - padded
    clear_from = (offset + counts[0]) // bm * bm
    xs3, dests = None, []
    for g, (idx_t, _, rank_t, _), cnt in zip(groups, routes, counts):
        first = jnp.sum(jnp.where(idx_t[:, :, None] == jnp.arange(N_EXPERTS), offset, 0), axis=-1)
        dests.append((first + rank_t).T.reshape(-1))
        xs3 = _dispatch(g["u3"], dests[-1], xs3, pad_end, clear_from, n_blocks * bm, min(512, g["N"]), bm)
        offset = offset + cnt
    y3 = _experts(xs3, block_e, live, W["exp_w_gate"], W["exp_w_up"], W["exp_w_down"], l, bm)
    for g, (_, w_t, _, _), dest in zip(groups, routes, dests):
        routed3 = _combine(y3, dest, w_t.T.reshape(-1), g["N"], min(512, g["N"]))
        g["x"] = _moe_final(g["u"], g["x"], routed3, W["sh_wgu"][l], W["sh_wd"][l], g["g_m"], W["ln_g"][l, 1],
                            W["ln_b"][l, 1], g["T"], g["tm"])


def _mixer(g, W, l):
    B, T, N, tm, past = g["B"], g["T"], g["N"], g["tm"], g["past"]
    sh_a, sc_a, g_a, sh_m, sc_m, g_m = jnp.split(g["mods"][l], 6, axis=-1)
    if l < N_A:
        p = _modproj(g["x"], sc_a, sh_a, W["gla_w_in"][l], T, tm, F32)
        if T >= GLA_CHUNK:
            Tp, tc, hps = T, min(512, T), 2
        else:
            Tp, tc, hps = GLA_CHUNK, GLA_CHUNK, GLA_HEADS
            p = jnp.pad(p.reshape(B, T, -1), ((0, 0), (0, Tp - T), (0, 0))).reshape(B * Tp, -1)
        o, s_T = _gla(p, W["gla_w_a2"][l], W["gla_b_a"][l], W["gla_norm_g"][l], g["s0"][l], B, Tp, tc, T, hps)
        if Tp != T:
            o = o.reshape(B, Tp, -1)[:, :T].reshape(N, -1)
        g["gla"].append(s_T)
        w_o = W["gla_w_o"][l]
    else:
        q = _modproj(g["x"], sc_a, sh_a, W["fox_w_q"][l - N_A], T, tm, BF16 if past is None else F32)
        if past is None:
            o = _fox_attention(q, g["kb"], g["vb"], g["cum"], g["cum_t"], B, T, min(512, T))
        else:
            o = _decode_attention(past["page_table"], q, g["kb"], g["vb"], g["cum"], g["cum_t"], past["k"],
                                  past["v"], past["lf_t"], B, T)
        w_o = W["fox_w_o"][l - N_A]
    g["x"], g["u"], g["u3"] = _proj_res_ln(o, w_o, g["x"], g_a, W["ln_g"][l, 0], W["ln_b"][l, 0], sc_m, sh_m, T, tm)
    g["g_m"] = g_m


def _shared_kv(g, W):
    B, T = g["B"], g["T"]
    sh_kv, sc_kv = jnp.split(g["kv_mod"], 2, axis=-1)
    g["k"], g["v"], kb, vb, g["lf"], g["cum"] = _kv_proj(g["x"], sc_kv, sh_kv, W["kv_w"], W["fox_b_f"], T, g["tm"])
    cum_h = g["cum"][:, :FOX_HEADS].reshape(B, T, FOX_HEADS)
    if g["past"] is None:
        g["cum_t"] = cum_h.transpose(0, 2, 1).reshape(B * FOX_HEADS, T)
        g["kb"], g["vb"] = kb, vb
    else:
        rows = ((0, 0), (0, PAGE_SIZE - T), (0, 0))
        g["cum_t"] = jnp.pad(cum_h, rows).transpose(0, 2, 1)
        g["kb"] = jnp.pad(kb.reshape(B, T, FOX_DIM), rows)
        g["vb"] = jnp.pad(vb.reshape(B, T, FOX_DIM), rows)


def _group(x, mods, kv_mod, s0, past):
    B, T, D = x.shape
    return dict(x=x.reshape(B * T, D), B=B, T=T, N=B * T, tm=min(512, B * T), mods=mods, kv_mod=kv_mod, s0=s0,
                past=past, gla=[])


def _group_outputs(g):
    B, T = g["B"], g["T"]
    return (g["x"].reshape(B, T, D_MODEL), jnp.stack(g["gla"]),
            g["k"].reshape(B, T, FOX_HEADS, FOX_HD), g["v"].reshape(B, T, FOX_HEADS, FOX_HD),
            g["lf"][:, :FOX_HEADS].reshape(B, T, FOX_HEADS))


def kernel(x_prompt, x_sample, c_prompt, c_sample, state_gla, cache_k, cache_v, cache_logf, page_table, gla_w_in, gla_w_a2, gla_b_a, gla_norm_g, gla_w_o, kv_w_mod, kv_b_mod, kv_w, fox_b_f, fox_w_q, fox_w_o, ada_w, ada_b, ln_g, ln_b, router_w, router_bias, exp_w_gate, exp_w_up, exp_w_down, sh_w_gate, sh_w_up, sh_w_down):
    Bp, Tp, D = x_prompt.shape
    Bs, Ts, _ = x_sample.shape
    lane_pad = lambda w, n: jnp.pad(w, [(0, 0)] * (w.ndim - 1) + [(0, n - w.shape[-1])])
    W = dict(
        gla_w_in=jnp.concatenate(
            [gla_w_in[..., :GLA_IN_MAIN], lane_pad(gla_w_in[..., GLA_IN_MAIN:], LANES)], axis=-1).astype(BF16),
        gla_w_a2=jnp.pad(gla_w_a2, ((0, 0), (0, LANES - GLA_GATE_RANK), (0, 0))).astype(BF16),
        gla_b_a=gla_b_a.reshape(N_A, 1, GLA_DK),
        gla_norm_g=gla_norm_g.reshape(N_A, 1, GLA_DVH),
        gla_w_o=gla_w_o.astype(BF16),
        kv_w=jnp.concatenate([kv_w[:, :2 * FOX_DIM], lane_pad(kv_w[:, 2 * FOX_DIM:], LANES)], axis=-1).astype(BF16),
        fox_b_f=lane_pad(fox_b_f.reshape(1, FOX_HEADS), LANES),
        fox_w_q=fox_w_q.astype(BF16),
        fox_w_o=fox_w_o.astype(BF16),
        ln_g=ln_g, ln_b=ln_b,
        router_wt=jnp.swapaxes(router_w, 1, 2).astype(BF16),
        router_bias=router_bias,
        exp_w_gate=exp_w_gate, exp_w_up=exp_w_up, exp_w_down=exp_w_down,
        sh_wgu=jnp.concatenate([sh_w_gate, sh_w_up], axis=-1).astype(BF16),
        sh_wd=sh_w_down.astype(BF16),
    )
    c_all = jnp.concatenate([c_prompt, c_sample], axis=0)
    mods = _cond_linear(c_all, ada_w, ada_b, 1536)
    kv_mod = _cond_linear(c_all, kv_w_mod[None], kv_b_mod[None], 1024)[0]

    s0_prompt = jnp.zeros((N_A, Bp, GLA_HEADS, GLA_DKH, GLA_DVH), F32)
    past = dict(page_table=page_table, k=cache_k, v=cache_v, lf_t=jnp.swapaxes(cache_logf, 1, 2))
    groups = [_group(x_prompt, mods[:, :Bp], kv_mod[:Bp], s0_prompt, None),
              _group(x_sample, mods[:, Bp:], kv_mod[Bp:], state_gla, past)]
    for l in range(DEPTH):
        for g in groups:
            _mixer(g, W, l)
        _moe(groups, W, l)
        if l == N_A - 1:
            for g in groups:
                _shared_kv(g, W)
    y_p, gla_p, k_p, v_p, lf_p = _group_outputs(groups[0])
    y_s, gla_s, k_s, v_s, lf_s = _group_outputs(groups[1])
    return (y_p, y_s, gla_p, gla_s, k_p, v_p, lf_p, k_s, v_s, lf_s)
```
